```python
import jax, jax.numpy as jnp
from jax import lax
import numpy as np

D_MODEL = 2048
BATCH = 1
SEQ = 16384
DEPTH = 4
DEC_BATCH = 2
DEC_SEQ = 4096
PAST_LEN = 128

HEAD_DIM = 128
GRID_W = 64
N_HEADS_A = 6
DILATED_BRANCHES = ((128, 1), (512, 4), (2048, 16))
N_HEADS_B = 10
N_KV_B = 2
N_HEADS_C = 16
NA_ROWS = 8
NA_COLS = 16
D_FF = 5632
N_EXPERTS = 8
TOP_K = 2
D_FF_EXPERT = 2816
ROPE_THETA = 10000.0
LN_EPS = 1e-5
QK_EPS = 1e-6
NEG = -1e30
SCALE = HEAD_DIM ** -0.5
DN_ALPHA = (2 * DEPTH) ** 0.25
DN_BETA = (8 * DEPTH) ** -0.25
LOCAL_BLOCK = 128
DENSE_BLOCK = 128
N_EVEN = (DEPTH + 1) // 2
N_ODD = DEPTH // 2
W_A = N_HEADS_A * HEAD_DIM
W_BQ = N_HEADS_B * HEAD_DIM
W_BKV = N_KV_B * HEAD_DIM
W_IN_EVEN = 3 * W_A + W_BQ + 2 * W_BKV
W_MIX_EVEN = W_A + W_BQ
W_C = N_HEADS_C * HEAD_DIM

kernel_name = 'hybrid_dilated_gqa_natten_moe_encoder'


def layer_norm(x, g, b):
    xf = x.astype(jnp.float32)
    mu = jnp.mean(xf, axis=-1, keepdims=True)
    var = jnp.mean(jnp.square(xf - mu), axis=-1, keepdims=True)
    y = (xf - mu) * lax.rsqrt(var + LN_EPS) * g.astype(jnp.float32) + b.astype(jnp.float32)
    return y.astype(x.dtype)


def head_rms_norm(x, g):
    xf = x.astype(jnp.float32)
    y = xf * lax.rsqrt(jnp.mean(xf * xf, axis=-1, keepdims=True) + QK_EPS) * g.astype(jnp.float32)
    return y.astype(x.dtype)


def swiglu(x, w1, w3, w2):
    return jnp.matmul(jax.nn.silu(jnp.matmul(x, w1)) * jnp.matmul(x, w3), w2)


def alibi_slopes(n_heads):
    return 2.0 ** (-8.0 * jnp.arange(1, n_heads + 1, dtype=jnp.float32) / n_heads)


def banded_attention(q, k, v, radius, step, slopes):
    n, L, h, dh = q.shape
    qb = min(LOCAL_BLOCK, L)
    pad_q = (-L) % qb
    nb = (L + pad_q) // qb
    kw = qb + 2 * radius
    qp = jnp.pad(q, ((0, 0), (0, pad_q), (0, 0), (0, 0))).reshape(n, nb, qb, h, dh)
    kv_pad = ((0, 0), (radius, radius + pad_q), (0, 0), (0, 0))
    kp = jnp.pad(k, kv_pad)
    vp = jnp.pad(v, kv_pad)
    idx = jnp.arange(nb)[:, None] * qb + jnp.arange(kw)[None, :]
    kb = kp[:, idx]
    vb = vp[:, idx]
    rel = jnp.arange(kw)[None, :] - radius - jnp.arange(qb)[:, None]
    key_pos = (idx - radius)[:, None, :]
    valid = (jnp.abs(rel)[None] <= radius) & (key_pos >= 0) & (key_pos < L)
    dist = (step * jnp.abs(rel)).astype(jnp.float32)
    s = jnp.einsum('nbqhd,nbkhd->nhbqk', qp, kb, preferred_element_type=jnp.float32) * SCALE
    s = s - slopes[:, None, None, None] * dist
    s = jnp.where(valid, s, NEG)
    m = jnp.max(s, axis=-1, keepdims=True)
    p = jnp.exp(s - m)
    den = jnp.sum(p, axis=-1, keepdims=True)
    o = jnp.einsum('nhbqk,nbkhd->nbqhd', (p / den).astype(v.dtype), vb,
                   preferred_element_type=jnp.float32)
    lse = (m + jnp.log(den))[..., 0]
    o = o.reshape(n, nb * qb, h, dh)[:, :L]
    lse = lse.transpose(0, 2, 3, 1).reshape(n, nb * qb, h)[:, :L]
    return o, lse


def dilated_branch(q, k, v, window, dilation, slopes):
    b, s, h, dh = q.shape
    L = s // dilation
    def to_res(t):
        return t.reshape(b, L, dilation, h, dh).transpose(0, 2, 1, 3, 4).reshape(b * dilation, L, h, dh)
    o, lse = banded_attention(to_res(q), to_res(k), to_res(v), window // (2 * dilation), dilation, slopes)
    o = o.reshape(b, dilation, L, h, dh).transpose(0, 2, 1, 3, 4).reshape(b, s, h, dh)
    lse = lse.reshape(b, dilation, L, h).transpose(0, 2, 1, 3).reshape(b, s, h)
    return o, lse


def dilated_mixture_attention(q, k, v):
    slopes = alibi_slopes(N_HEADS_A)
    outs, lses = [], []
    for window, dilation in DILATED_BRANCHES:
        o, lse = dilated_branch(q, k, v, window, dilation, slopes)
        outs.append(o)
        lses.append(lse)
    wts = jax.nn.softmax(jnp.stack(lses, axis=0), axis=0)
    o = jnp.einsum('gbsh,gbshd->bshd', wts, jnp.stack(outs, axis=0))
    return o.astype(q.dtype)


def axial_rope_tables(s):
    t = jnp.arange(s)
    row = (t // GRID_W).astype(jnp.float32)
    col = (t % GRID_W).astype(jnp.float32)
    axis_dim = HEAD_DIM // 2
    inv = ROPE_THETA ** (-jnp.arange(0, axis_dim, 2, dtype=jnp.float32) / axis_dim)
    ang = jnp.concatenate([row[:, None] * inv, col[:, None] * inv], axis=-1)
    return jnp.cos(ang), jnp.sin(ang)


def apply_rope(x, cos, sin):
    xf = x.astype(jnp.float32).reshape(*x.shape[:-1], HEAD_DIM // 2, 2)
    x1, x2 = xf[..., 0], xf[..., 1]
    c = cos[None, :, None, :]
    sn = sin[None, :, None, :]
    out = jnp.stack([x1 * c - x2 * sn, x1 * sn + x2 * c], axis=-1).reshape(x.shape)
    return out.astype(x.dtype)


def gqa_attention(q, k, v):
    b, s, _, dh = q.shape
    rep = N_HEADS_B // N_KV_B
    nblk = s // DENSE_BLOCK
    qg = jnp.moveaxis(q.reshape(b, nblk, DENSE_BLOCK, N_KV_B, rep, dh), 1, 0)
    def one_block(qblk):
        sc = jnp.einsum('bqgrd,bkgd->bgrqk', qblk, k, preferred_element_type=jnp.float32) * SCALE
        p = jax.nn.softmax(sc, axis=-1)
        o = jnp.einsum('bgrqk,bkgd->bqgrd', p.astype(v.dtype), v, preferred_element_type=jnp.float32)
        return o.astype(q.dtype)
    o = lax.map(one_block, qg)
    return jnp.moveaxis(o, 0, 1).reshape(b, s, N_HEADS_B, dh)


def neighbourhood_attention(q, k, v, rpb):
    b, s, h, dh = q.shape
    rows = s // GRID_W
    kr = min(NA_ROWS, rows)
    qg = q.reshape(b, rows, GRID_W, h, dh)
    kg = k.reshape(b, rows, GRID_W, h, dh)
    vg = v.reshape(b, rows, GRID_W, h, dh)
    cols = jnp.arange(GRID_W)
    col_start = jnp.clip(cols - NA_COLS // 2, 0, GRID_W - NA_COLS)
    col_in = (cols[None, :] >= col_start[:, None]) & (cols[None, :] < col_start[:, None] + NA_COLS)
    col_idx = jnp.clip(cols[None, :] - cols[:, None] + NA_COLS - 1, 0, 2 * NA_COLS - 2)
    def one_row(r):
        rs = jnp.clip(r - kr // 2, 0, rows - kr)
        kb = lax.dynamic_slice_in_dim(kg, rs, kr, axis=1)
        vb = lax.dynamic_slice_in_dim(vg, rs, kr, axis=1)
        qr = lax.dynamic_index_in_dim(qg, r, axis=1, keepdims=False)
        row_off = rs + jnp.arange(kr) - r
        bias = rpb[:, row_off + NA_ROWS - 1][:, :, col_idx]
        sc = jnp.einsum('bchd,bwkhd->bhcwk', qr, kb, preferred_element_type=jnp.float32) * SCALE
        sc = sc + bias.transpose(0, 2, 1, 3).astype(jnp.float32)
        sc = jnp.where(col_in[:, None, :], sc, NEG)
        p = jax.nn.softmax(sc.reshape(b, h, GRID_W, kr * GRID_W), axis=-1).reshape(sc.shape)
        o = jnp.einsum('bhcwk,bwkhd->bchd', p.astype(v.dtype), vb, preferred_element_type=jnp.float32)
        return o.astype(q.dtype)
    o = lax.map(one_row, jnp.arange(rows))
    return o.transpose(1, 0, 2, 3, 4).reshape(b, s, h, dh)


def even_mixer(x, w_in, q_gain, k_gain, w_out):
    b, s, _ = x.shape
    proj = jnp.matmul(x, w_in)
    qa, ka, va, qb, kb, vb = jnp.split(
        proj, [W_A, 2 * W_A, 3 * W_A, 3 * W_A + W_BQ, 3 * W_A + W_BQ + W_BKV], axis=-1)
    heads = lambda t, n: t.reshape(b, s, n, HEAD_DIM)
    oa = dilated_mixture_attention(heads(qa, N_HEADS_A), heads(ka, N_HEADS_A), heads(va, N_HEADS_A))
    cos, sin = axial_rope_tables(s)
    qb = apply_rope(head_rms_norm(heads(qb, N_HEADS_B), q_gain), cos, sin)
    kb = apply_rope(head_rms_norm(heads(kb, N_KV_B), k_gain), cos, sin)
    ob = gqa_attention(qb, kb, heads(vb, N_KV_B))
    mix = jnp.concatenate([oa.reshape(b, s, W_A), ob.reshape(b, s, W_BQ)], axis=-1)
    return jnp.matmul(mix, w_out)


def odd_mixer(x, w_in, rpb, w_out):
    b, s, _ = x.shape
    q, k, v = jnp.split(jnp.matmul(x, w_in), [W_C, 2 * W_C], axis=-1)
    heads = lambda t: t.reshape(b, s, N_HEADS_C, HEAD_DIM)
    o = neighbourhood_attention(heads(q), heads(k), heads(v), rpb)
    return jnp.matmul(o.reshape(b, s, W_C), w_out)


def moe_swiglu(x, w_router, b_router, w1, w3, w2):
    b, s, d = x.shape
    xt = x.reshape(b * s, d)
    logits = jnp.matmul(xt, w_router, preferred_element_type=jnp.float32) + b_router.astype(jnp.float32)
    top_val, top_idx = lax.top_k(logits, TOP_K)
    gates = jax.nn.softmax(top_val, axis=-1)
    combine = jnp.einsum('nk,nke->ne', gates, jax.nn.one_hot(top_idx, N_EXPERTS, dtype=jnp.float32))
    y = jnp.zeros((b * s, d), jnp.float32)
    for e in range(N_EXPERTS):
        y = y + combine[:, e:e + 1] * swiglu(xt, w1[e], w3[e], w2[e]).astype(jnp.float32)
    return y.astype(x.dtype).reshape(b, s, d)


def setup_inputs(seed: int = 0) -> dict:
    key = jax.random.key(seed)
    ks = jax.random.split(key, 20)
    f32 = jnp.float32
    d = D_MODEL
    def nrm(k, shape, scale):
        return jax.random.normal(k, shape, f32) * scale
    x_prompt = nrm(ks[0], (BATCH, SEQ, d), 1.0)
    x_sample = nrm(ks[1], (DEC_BATCH, DEC_SEQ, d), 1.0)
    ln_g = 1.0 + nrm(ks[2], (DEPTH, 2, d), 0.02)
    ln_b = nrm(ks[3], (DEPTH, 2, d), 0.02)
    col_scale_even = jnp.concatenate([
        jnp.ones((2 * W_A,), f32), jnp.full((W_A,), DN_BETA, f32),
        jnp.ones((W_BQ + W_BKV,), f32), jnp.full((W_BKV,), DN_BETA, f32)])
    w_in_even = nrm(ks[4], (N_EVEN, d, W_IN_EVEN), d ** -0.5) * col_scale_even
    qk_gain_b = 1.0 + nrm(ks[5], (N_EVEN, 2, HEAD_DIM), 0.02)
    w_out_even = nrm(ks[6], (N_EVEN, W_MIX_EVEN, d), DN_BETA * W_MIX_EVEN ** -0.5)
    ffn_w1 = nrm(ks[7], (N_EVEN, d, D_FF), d ** -0.5)
    ffn_w3 = nrm(ks[8], (N_EVEN, d, D_FF), d ** -0.5)
    ffn_w2 = nrm(ks[9], (N_EVEN, D_FF, d), DN_BETA * D_FF ** -0.5)
    col_scale_odd = jnp.concatenate([jnp.ones((2 * W_C,), f32), jnp.full((W_C,), DN_BETA, f32)])
    w_in_odd = nrm(ks[10], (N_ODD, d, 3 * W_C), d ** -0.5) * col_scale_odd
    rpb = nrm(ks[11], (N_ODD, N_HEADS_C, 2 * NA_ROWS - 1, 2 * NA_COLS - 1), 0.1)
    w_out_odd = nrm(ks[12], (N_ODD, W_C, d), DN_BETA * W_C ** -0.5)
    w_router = nrm(ks[13], (N_ODD, d, N_EXPERTS), d ** -0.5)
    b_router = nrm(ks[14], (N_ODD, N_EXPERTS), 0.01)
    moe_w1 = nrm(ks[15], (N_ODD, N_EXPERTS, d, D_FF_EXPERT), d ** -0.5)
    moe_w3 = nrm(ks[16], (N_ODD, N_EXPERTS, d, D_FF_EXPERT), d ** -0.5)
    moe_w2 = nrm(ks[17], (N_ODD, N_EXPERTS, D_FF_EXPERT, d), DN_BETA * D_FF_EXPERT ** -0.5)
    return {'x_prompt': x_prompt, 'x_sample': x_sample, 'ln_g': ln_g, 'ln_b': ln_b,
            'w_in_even': w_in_even, 'qk_gain_b': qk_gain_b, 'w_out_even': w_out_even,
            'ffn_w1': ffn_w1, 'ffn_w3': ffn_w3, 'ffn_w2': ffn_w2,
            'w_in_odd': w_in_odd, 'rpb': rpb, 'w_out_odd': w_out_odd,
            'w_router': w_router, 'b_router': b_router,
            'moe_w1': moe_w1, 'moe_w3': moe_w3, 'moe_w2': moe_w2}


def reference(x_prompt, x_sample, ln_g, ln_b, w_in_even, qk_gain_b, w_out_even, ffn_w1, ffn_w3, ffn_w2,
              w_in_odd, rpb, w_out_odd, w_router, b_router, moe_w1, moe_w3, moe_w2):
    def trunk(x):
        for i in range(DEPTH):
            j = i // 2
            if i % 2 == 0:
                h = even_mixer(x, w_in_even[j], qk_gain_b[j, 0], qk_gain_b[j, 1], w_out_even[j])
            else:
                h = odd_mixer(x, w_in_odd[j], rpb[j], w_out_odd[j])
            x = layer_norm(DN_ALPHA * x + h, ln_g[i, 0], ln_b[i, 0])
            if i % 2 == 0:
                f = swiglu(x, ffn_w1[j], ffn_w3[j], ffn_w2[j])
            else:
                f = moe_swiglu(x, w_router[j], b_router[j], moe_w1[j], moe_w3[j], moe_w2[j])
            x = layer_norm(DN_ALPHA * x + f, ln_g[i, 1], ln_b[i, 1])
        return x
    y_prompt = trunk(x_prompt)
    y_sample = trunk(x_sample)
    return (y_prompt, y_sample)
```

```python
import functools
import math

import numpy as np
import jax
import jax.numpy as jnp
from jax import lax
from jax.experimental import pallas as pl
from jax.experimental.pallas import tpu as pltpu

HEAD_DIM = 128
GRID_W = 64
N_HEADS_A = 6
DILATED_BRANCHES = ((128, 1), (512, 4), (2048, 16))
N_HEADS_B = 10
N_KV_B = 2
N_HEADS_C = 16
NA_ROWS = 8
NA_COLS = 16
N_EXPERTS = 8
TOP_K = 2
DEPTH = 4
ROPE_THETA = 10000.0
LN_EPS = 1e-5
QK_EPS = 1e-6
NEG = -1e30
SCALE = HEAD_DIM ** -0.5
LOG2E = math.log2(math.e)
DN_ALPHA = (2 * DEPTH) ** 0.25

V7X_VMEM_BYTES = 64 * 2 ** 20
VMEM_LIMIT = V7X_VMEM_BYTES - 8 * 2 ** 20
LANES = 128
ATTN_TQ = 256
ROUTER_LANES = LANES

F32 = jnp.float32
BF16 = jnp.bfloat16


def _params(*sem):
    return pltpu.CompilerParams(dimension_semantics=sem, vmem_limit_bytes=VMEM_LIMIT)


def _tile(n, pref):
    if n <= pref:
        return n
    t = (pref // LANES) * LANES
    while t >= LANES:
        if n % t == 0:
            return t
        t -= LANES
    return n


def _proj_kernel(x_ref, w_ref, cs_ref, o_ref):
    acc = jnp.dot(x_ref[...], w_ref[...], preferred_element_type=F32)
    o_ref[...] = (acc * cs_ref[...]).astype(o_ref.dtype)


def _project(xb, w, col_scale):
    t, k = xb.shape
    n = w.shape[1]
    tm, tn = _tile(t, 1024), _tile(n, 1024)
    return pl.pallas_call(
        _proj_kernel,
        grid=(t // tm, n // tn),
        in_specs=[pl.BlockSpec((tm, k), lambda i, j: (i, 0)),
                  pl.BlockSpec((k, tn), lambda i, j: (0, j)),
                  pl.BlockSpec((1, tn), lambda i, j: (0, j))],
        out_specs=pl.BlockSpec((tm, tn), lambda i, j: (i, j)),
        out_shape=jax.ShapeDtypeStruct((t, n), BF16),
        compiler_params=_params("arbitrary", "arbitrary"),
    )(xb, w, col_scale)


def _layer_norm(z, g, b):
    mu = jnp.mean(z, axis=-1, keepdims=True)
    zc = z - mu
    var = jnp.mean(zc * zc, axis=-1, keepdims=True)
    return zc * lax.rsqrt(var + LN_EPS) * g + b


def _qk_prep_kernel(p_ref, g_ref, sc_ref, cos_ref, sin_ref, o_ref, *, heads):
    c = cos_ref[...]
    s = sin_ref[...]
    for r in range(heads):
        x = p_ref[:, r * HEAD_DIM:(r + 1) * HEAD_DIM].astype(F32)
        ms = jnp.mean(x * x, axis=-1, keepdims=True)
        xn = x * lax.rsqrt(ms + QK_EPS) * g_ref[0, r:r + 1, :]
        y = xn * c + pltpu.roll(xn, HEAD_DIM // 2, 1) * s
        o_ref[:, r * HEAD_DIM:(r + 1) * HEAD_DIM] = (y * sc_ref[0, r:r + 1, :]).astype(o_ref.dtype)


def _qk_prep(proj, gains, scales, cos_t, sin_t, col0):
    t = proj.shape[0]
    n_heads = N_HEADS_B + N_KV_B
    half = n_heads // 2
    wblk = half * HEAD_DIM
    assert col0 % wblk == 0
    tm = _tile(t, 512)
    return pl.pallas_call(
        functools.partial(_qk_prep_kernel, heads=half),
        grid=(t // tm, 2),
        in_specs=[pl.BlockSpec((tm, wblk), lambda i, j: (i, col0 // wblk + j)),
                  pl.BlockSpec((1, half, HEAD_DIM), lambda i, j: (j, 0, 0)),
                  pl.BlockSpec((1, half, HEAD_DIM), lambda i, j: (j, 0, 0)),
                  pl.BlockSpec((tm, HEAD_DIM), lambda i, j: (i, 0)),
                  pl.BlockSpec((tm, HEAD_DIM), lambda i, j: (i, 0))],
        out_specs=pl.BlockSpec((tm, wblk), lambda i, j: (i, j)),
        out_shape=jax.ShapeDtypeStruct((t, n_heads * HEAD_DIM), BF16),
        compiler_params=_params("arbitrary", "arbitrary"),
    )(proj, gains, scales, cos_t, sin_t)


def _window_start_blk(i, halo_blks, win_blks, n_blks):
    return jnp.clip(i - halo_blks, 0, n_blks - win_blks)


def _win_attn_kernel(q_ref, k_ref, v_ref, tb_ref, o_ref, *, halo_blks, win_blks, n_blks, axis):
    i = pl.program_id(axis)
    tq = q_ref.shape[0]
    w = win_blks * tq
    start = pl.multiple_of(_window_start_blk(i, halo_blks, win_blks, n_blks) * tq, tq)
    k = k_ref[pl.ds(start, w), :]
    v = v_ref[pl.ds(start, w), :]
    s = lax.dot_general(q_ref[...], k, (((1,), (1,)), ((), ())), preferred_element_type=F32)
    s = s + tb_ref[0, 0]
    m = jnp.max(s, axis=-1, keepdims=True)
    p = jnp.exp2(s - m)
    den = jnp.sum(p, axis=-1, keepdims=True)
    o = jnp.dot(p.astype(BF16), v, preferred_element_type=F32)
    o_ref[...] = (o / den).astype(o_ref.dtype)


def _win_attn(src, table, *, row0, batch, seq, n_heads, q_blk0, k_blk0, v_blk0, halo_blks, win_blks):
    tq = ATTN_TQ
    n_blks = seq // tq
    assert seq % tq == 0 and n_blks >= win_blks and row0 % seq == 0
    seq_blk0 = row0 // seq
    qrow0 = row0 // tq

    def variant(i):
        return i - _window_start_blk(i, halo_blks, win_blks, n_blks)

    kern = functools.partial(_win_attn_kernel, halo_blks=halo_blks, win_blks=win_blks,
                             n_blks=n_blks, axis=2)
    return pl.pallas_call(
        kern,
        grid=(batch, n_heads, n_blks),
        in_specs=[pl.BlockSpec((tq, HEAD_DIM), lambda b, h, i: (qrow0 + b * n_blks + i, q_blk0 + h)),
                  pl.BlockSpec((seq, HEAD_DIM), lambda b, h, i: (seq_blk0 + b, k_blk0 + h)),
                  pl.BlockSpec((seq, HEAD_DIM), lambda b, h, i: (seq_blk0 + b, v_blk0 + h)),
                  pl.BlockSpec((1, 1, tq, win_blks * tq), lambda b, h, i: (h, variant(i), 0, 0))],
        out_specs=pl.BlockSpec((tq, HEAD_DIM), lambda b, h, i: (b * n_blks + i, h)),
        out_shape=jax.ShapeDtypeStruct((batch * seq, n_heads * HEAD_DIM), BF16),
        compiler_params=_params("arbitrary", "arbitrary", "arbitrary"),
    )(src, src, src, table)


def _dilated_table():
    tq = ATTN_TQ
    halo = max(w // 2 for w, _ in DILATED_BRANCHES)
    halo_blks = halo // tq
    win_blks = 2 * halo_blks + 1
    v = jnp.arange(win_blks)[:, None, None]
    r = jnp.arange(tq)[None, :, None]
    c = jnp.arange(win_blks * tq)[None, None, :]
    delta = c - v * tq - r
    ad = jnp.abs(delta)
    cnt = jnp.zeros(delta.shape, F32)
    for window, dil in DILATED_BRANCHES:
        cnt = cnt + ((ad <= window // 2) & (delta % dil == 0)).astype(F32)
    slopes = 2.0 ** (-8.0 * jnp.arange(1, N_HEADS_A + 1, dtype=F32) / N_HEADS_A)
    bias = -slopes[:, None, None, None] * ad.astype(F32)[None] + jnp.log(jnp.maximum(cnt, 1.0))[None]
    table = jnp.where(cnt[None] > 0, bias * LOG2E, NEG)
    return table, halo_blks, win_blks


def _natten_table(rpb):
    tq = ATTN_TQ
    rows_per_tile = tq // GRID_W
    halo_blks = 1
    win_blks = 3
    assert NA_ROWS // 2 == rows_per_tile
    win_rows = win_blks * rows_per_tile
    qt = np.arange(win_blks)[:, None] * tq + np.arange(tq)[None, :]
    rq, qc = qt // GRID_W, qt % GRID_W
    kt = np.arange(win_blks * tq)
    rk, kc = kt // GRID_W, kt % GRID_W
    rs = np.clip(rq - NA_ROWS // 2, 0, win_rows - NA_ROWS)
    cs = np.clip(qc - NA_COLS // 2, 0, GRID_W - NA_COLS)
    valid = ((rk[None, None, :] >= rs[..., None]) & (rk[None, None, :] < rs[..., None] + NA_ROWS)
             & (kc[None, None, :] >= cs[..., None]) & (kc[None, None, :] < cs[..., None] + NA_COLS))
    ri = np.clip(rk[None, None, :] - rq[..., None] + NA_ROWS - 1, 0, 2 * NA_ROWS - 2)
    ci = np.clip(kc[None, None, :] - qc[..., None] + NA_COLS - 1, 0, 2 * NA_COLS - 2)
    flat = rpb.astype(F32).reshape(rpb.shape[0], -1)
    bias = jnp.take(flat, jnp.asarray(ri * (2 * NA_COLS - 1) + ci, jnp.int32), axis=1)
    table = jnp.where(jnp.asarray(valid)[None], bias * LOG2E, NEG)
    return table, halo_blks, win_blks


def _gqa_kernel(q_ref, k_ref, v_ref, o_ref, qs_ref, acc_ref, m_ref, l_ref, *, rep, tk):
    tq = q_ref.shape[0]
    seq = k_ref.shape[0]
    for r in range(rep):
        qs_ref[r * tq:(r + 1) * tq, :] = q_ref[:, r * HEAD_DIM:(r + 1) * HEAD_DIM]
    m_ref[...] = jnp.full(m_ref.shape, -jnp.inf, F32)
    l_ref[...] = jnp.zeros(l_ref.shape, F32)
    acc_ref[...] = jnp.zeros(acc_ref.shape, F32)

    def body(c, carry):
        off = pl.multiple_of(c * tk, tk)
        k = k_ref[pl.ds(off, tk), :]
        v = v_ref[pl.ds(off, tk), :]
        s = lax.dot_general(qs_ref[...], k, (((1,), (1,)), ((), ())), preferred_element_type=F32)
        m_prev = m_ref[...]
        m_new = jnp.maximum(m_prev, jnp.max(s, axis=-1, keepdims=True))
        alpha = jnp.exp2(m_prev - m_new)
        p = jnp.exp2(s - m_new)
        l_ref[...] = alpha * l_ref[...] + jnp.sum(p, axis=-1, keepdims=True)
        acc_ref[...] = alpha * acc_ref[...] + jnp.dot(p.astype(BF16), v, preferred_element_type=F32)
        m_ref[...] = m_new
        return carry

    lax.fori_loop(0, seq // tk, body, 0)
    o = acc_ref[...] / l_ref[...]
    for r in range(rep):
        o_ref[:, r * HEAD_DIM:(r + 1) * HEAD_DIM] = o[r * tq:(r + 1) * tq].astype(o_ref.dtype)


def _gqa(qk, proj, *, row0, batch, seq, v_blk0):
    rep = N_HEADS_B // N_KV_B
    tq = 256
    tk = _tile(seq, 512)
    n_blks = seq // tq
    assert seq % tq == 0 and row0 % seq == 0
    seq_blk0 = row0 // seq
    qrow0 = row0 // tq
    wq = rep * HEAD_DIM
    return pl.pallas_call(
        functools.partial(_gqa_kernel, rep=rep, tk=tk),
        grid=(batch, N_KV_B, n_blks),
        in_specs=[pl.BlockSpec((tq, wq), lambda b, g, i: (qrow0 + b * n_blks + i, g)),
                  pl.BlockSpec((seq, HEAD_DIM), lambda b, g, i: (seq_blk0 + b, N_HEADS_B + g)),
                  pl.BlockSpec((seq, HEAD_DIM), lambda b, g, i: (seq_blk0 + b, v_blk0 + g))],
        out_specs=pl.BlockSpec((tq, wq), lambda b, g, i: (b * n_blks + i, g)),
        out_shape=jax.ShapeDtypeStruct((batch * seq, N_HEADS_B * HEAD_DIM), BF16),
        scratch_shapes=[pltpu.VMEM((rep * tq, HEAD_DIM), BF16),
                        pltpu.VMEM((rep * tq, HEAD_DIM), F32),
                        pltpu.VMEM((rep * tq, 1), F32),
                        pltpu.VMEM((rep * tq, 1), F32)],
        compiler_params=_params("arbitrary", "arbitrary", "arbitrary"),
    )(qk, qk, proj)


def _out_ln_kernel(x_ref, *refs, n_in, router):
    ins = refs[:n_in]
    ws = refs[n_in:2 * n_in]
    g_ref, b_ref = refs[2 * n_in:2 * n_in + 2]
    rest = refs[2 * n_in + 2:]
    h = jnp.dot(ins[0][...], ws[0][...], preferred_element_type=F32)
    for a, w in zip(ins[1:], ws[1:]):
        h = h + jnp.dot(a[...], w[...], preferred_element_type=F32)
    y = _layer_norm(DN_ALPHA * x_ref[...] + h, g_ref[...], b_ref[...])
    if not router:
        y_ref, yb_ref = rest
        y_ref[...] = y
        yb_ref[...] = y.astype(BF16)
        return
    wr_hi_ref, wr_lo_ref, br_ref, y_ref, yb_ref, r_ref = rest
    y_hi = y.astype(BF16)
    y_lo = (y - y_hi.astype(F32)).astype(BF16)
    y_ref[...] = y
    yb_ref[...] = y_hi
    logits = (jnp.dot(y_hi, wr_hi_ref[...], preferred_element_type=F32)
              + jnp.dot(y_lo, wr_hi_ref[...], preferred_element_type=F32)
              + jnp.dot(y_hi, wr_lo_ref[...], preferred_element_type=F32)) + br_ref[...]
    lane = lax.broadcasted_iota(jnp.int32, logits.shape, 1).astype(F32)
    m1 = jnp.max(logits, axis=-1, keepdims=True)
    i1 = jnp.min(jnp.where(logits == m1, lane, float(ROUTER_LANES)), axis=-1, keepdims=True)
    rest_l = jnp.where(lane == i1, -jnp.inf, logits)
    m2 = jnp.max(rest_l, axis=-1, keepdims=True)
    i2 = jnp.min(jnp.where(rest_l == m2, lane, float(ROUTER_LANES)), axis=-1, keepdims=True)
    e2 = jnp.exp(m2 - m1)
    den = 1.0 + e2
    g1 = 1.0 / den
    g2 = e2 / den
    sel = ((lane == i1) | (lane == i2)).astype(F32)
    e = N_EXPERTS
    out = jnp.where(lane < e, sel, 0.0)
    out = jnp.where(lane == e, i1, out)
    out = jnp.where(lane == e + 1, i2, out)
    out = jnp.where(lane == e + 2, g1, out)
    out = jnp.where(lane == e + 3, g2, out)
    r_ref[...] = out


def _out_ln(x, acts, weights, g, b, router=None):
    t, d = x.shape
    tm = _tile(t, 512)
    n_in = len(acts)
    row = lambda i: (i, 0)
    const = lambda i: (0, 0)
    in_specs = [pl.BlockSpec((tm, d), row)]
    in_specs += [pl.BlockSpec((tm, a.shape[1]), row) for a in acts]
    in_specs += [pl.BlockSpec(w.shape, const, pipeline_mode=pl.Buffered(1)) for w in weights]
    in_specs += [pl.BlockSpec((1, d), const), pl.BlockSpec((1, d), const)]
    out_specs = [pl.BlockSpec((tm, d), row), pl.BlockSpec((tm, d), row)]
    out_shape = [jax.ShapeDtypeStruct((t, d), F32), jax.ShapeDtypeStruct((t, d), BF16)]
    args = [x, *acts, *weights, g, b]
    if router is not None:
        in_specs += [pl.BlockSpec((d, ROUTER_LANES), const), pl.BlockSpec((d, ROUTER_LANES), const),
                     pl.BlockSpec((1, ROUTER_LANES), const)]
        out_specs.append(pl.BlockSpec((tm, ROUTER_LANES), row))
        out_shape.append(jax.ShapeDtypeStruct((t, ROUTER_LANES), F32))
        args += list(router)
    return pl.pallas_call(
        functools.partial(_out_ln_kernel, n_in=n_in, router=router is not None),
        grid=(t // tm,),
        in_specs=in_specs, out_specs=out_specs, out_shape=out_shape,
        compiler_params=_params("arbitrary"),
    )(*args)


def _swiglu_partial(xb, w1, w3, w2):
    h1 = jnp.dot(xb, w1, preferred_element_type=F32)
    h3 = jnp.dot(xb, w3, preferred_element_type=F32)
    h = (h1 * (1.0 / (1.0 + jnp.exp(-h1)))) * h3
    return jnp.dot(h.astype(BF16), w2, preferred_element_type=F32)


def _ffn_ln_kernel(x_ref, xb_ref, w1_ref, w3_ref, w2_ref, g_ref, b_ref, y_ref, yb_ref, acc_ref):
    j = pl.program_id(1)
    part = _swiglu_partial(xb_ref[...], w1_ref[...], w3_ref[...], w2_ref[...])

    @pl.when(j == 0)
    def _():
        acc_ref[...] = part

    @pl.when(j > 0)
    def _():
        acc_ref[...] += part

    @pl.when(j == pl.num_programs(1) - 1)
    def _():
        y = _layer_norm(DN_ALPHA * x_ref[...] + acc_ref[...], g_ref[...], b_ref[...])
        y_ref[...] = y
        yb_ref[...] = y.astype(BF16)


def _ffn_ln(x, xb, w1, w3, w2, g, b):
    t, d = x.shape
    f = w1.shape[1]
    tm, tf = _tile(t, 512), _tile(f, 512)
    return pl.pallas_call(
        _ffn_ln_kernel,
        grid=(t // tm, f // tf),
        in_specs=[pl.BlockSpec((tm, d), lambda i, j: (i, 0)),
                  pl.BlockSpec((tm, d), lambda i, j: (i, 0)),
                  pl.BlockSpec((d, tf), lambda i, j: (0, j)),
                  pl.BlockSpec((d, tf), lambda i, j: (0, j)),
                  pl.BlockSpec((tf, d), lambda i, j: (j, 0)),
                  pl.BlockSpec((1, d), lambda i, j: (0, 0)),
                  pl.BlockSpec((1, d), lambda i, j: (0, 0))],
        out_specs=[pl.BlockSpec((tm, d), lambda i, j: (i, 0)),
                   pl.BlockSpec((tm, d), lambda i, j: (i, 0))],
        out_shape=[jax.ShapeDtypeStruct((t, d), F32), jax.ShapeDtypeStruct((t, d), BF16)],
        scratch_shapes=[pltpu.VMEM((tm, d), F32)],
        compiler_params=_params("arbitrary", "arbitrary"),
    )(x, xb, w1, w3, w2, g, b)


MOE_TM = 512


def _moe_expert_kernel(te_ref, act_ref, xs_ref, w1_ref, w3_ref, w2_ref, y_ref, acc_ref):
    i = pl.program_id(0)
    j = pl.program_id(1)
    active = act_ref[i] > 0

    @pl.when(active)
    def _():
        part = _swiglu_partial(xs_ref[...], w1_ref[...], w3_ref[...], w2_ref[...])

        @pl.when(j == 0)
        def _():
            acc_ref[...] = part

        @pl.when(j > 0)
        def _():
            acc_ref[...] += part

    last = j == pl.num_programs(1) - 1

    @pl.when(last & active)
    def _():
        y_ref[...] = acc_ref[...]

    @pl.when(last & jnp.logical_not(active))
    def _():
        y_ref[...] = jnp.zeros(y_ref.shape, y_ref.dtype)


def _moe_experts(xs, tile_expert, tile_active, w1, w3, w2):
    p, d = xs.shape
    f = w1.shape[2]
    tm, tf = MOE_TM, _tile(f, 256)
    grid_spec = pltpu.PrefetchScalarGridSpec(
        num_scalar_prefetch=2,
        grid=(p // tm, f // tf),
        in_specs=[pl.BlockSpec((tm, d), lambda i, j, te, ac: (i, 0)),
                  pl.BlockSpec((None, d, tf), lambda i, j, te, ac: (te[i], 0, j)),
                  pl.BlockSpec((None, d, tf), lambda i, j, te, ac: (te[i], 0, j)),
                  pl.BlockSpec((None, tf, d), lambda i, j, te, ac: (te[i], j, 0))],
        out_specs=pl.BlockSpec((tm, d), lambda i, j, te, ac: (i, 0)),
        scratch_shapes=[pltpu.VMEM((tm, d), F32)])
    return pl.pallas_call(
        _moe_expert_kernel,
        grid_spec=grid_spec,
        out_shape=jax.ShapeDtypeStruct((p, d), F32),
        compiler_params=_params("arbitrary", "arbitrary"),
    )(tile_expert, tile_active, xs, w1, w3, w2)


def _combine_ln_kernel(x_ref, y1_ref, y2_ref, gt_ref, g_ref, b_ref, y_ref, yb_ref):
    g1 = gt_ref[:, 0:1]
    g2 = gt_ref[:, 1:2]
    f = g1 * y1_ref[...] + g2 * y2_ref[...]
    y = _layer_norm(DN_ALPHA * x_ref[...] + f, g_ref[...], b_ref[...])
    y_ref[...] = y
    yb_ref[...] = y.astype(BF16)


def _combine_ln(x, y1, y2, gates, g, b):
    t, d = x.shape
    tm = _tile(t, 512)
    row = lambda i: (i, 0)
    const = lambda i: (0, 0)
    return pl.pallas_call(
        _combine_ln_kernel,
        grid=(t // tm,),
        in_specs=[pl.BlockSpec((tm, d), row), pl.BlockSpec((tm, d), row), pl.BlockSpec((tm, d), row),
                  pl.BlockSpec((tm, LANES), row), pl.BlockSpec((1, d), const), pl.BlockSpec((1, d), const)],
        out_specs=[pl.BlockSpec((tm, d), row), pl.BlockSpec((tm, d), row)],
        out_shape=[jax.ShapeDtypeStruct((t, d), F32), jax.ShapeDtypeStruct((t, d), BF16)],
        compiler_params=_params("arbitrary"),
    )(x, y1, y2, gates, g, b)


def _moe_routing(r, tm):
    t = r.shape[0]
    e = N_EXPERTS
    sel = r[:, :e].astype(jnp.int32)
    i1 = r[:, e].astype(jnp.int32)
    i2 = r[:, e + 1].astype(jnp.int32)
    cnt = jnp.cumsum(sel, axis=0)
    rank = cnt - sel
    padded = ((cnt[-1] + tm - 1) // tm) * tm
    ends = jnp.cumsum(padded)
    pos = (ends - padded)[None, :] + rank
    p1 = jnp.take_along_axis(pos, i1[:, None], axis=1)[:, 0]
    p2 = jnp.take_along_axis(pos, i2[:, None], axis=1)[:, 0]
    n_rows = TOP_K * t + e * tm
    tile_start = jnp.arange(n_rows // tm, dtype=jnp.int32) * tm
    tile_expert = jnp.minimum(jnp.searchsorted(ends, tile_start, side='right'), e - 1).astype(jnp.int32)
    tile_active = (tile_start < ends[-1]).astype(jnp.int32)
    return p1, p2, tile_expert, tile_active, n_rows


def _deinterleave_perm():
    half = HEAD_DIM // 2
    return np.concatenate([np.arange(half) * 2, np.arange(half) * 2 + 1])


def _rope_tables(groups):
    pos = np.concatenate([np.tile(np.arange(seq), batch) for batch, seq in groups])
    pos = jnp.asarray(pos, jnp.int32)
    row = (pos // GRID_W).astype(F32)
    col = (pos % GRID_W).astype(F32)
    axis_dim = HEAD_DIM // 2
    inv = ROPE_THETA ** (-jnp.arange(0, axis_dim, 2, dtype=F32) / axis_dim)
    ang = jnp.concatenate([row[:, None] * inv, col[:, None] * inv], axis=-1)
    cos, sin = jnp.cos(ang), jnp.sin(ang)
    return jnp.concatenate([cos, cos], axis=-1), jnp.concatenate([-sin, sin], axis=-1)


def _group_rows(groups):
    out, row0 = [], 0
    for batch, seq in groups:
        out.append((row0, batch, seq))
        row0 += batch * seq
    return out


def _even_layer(x, xb, groups, w_in, qk_gain, w_out, w1, w3, w2, ln_g, ln_b, rope, dil):
    d = x.shape[1]
    w_a = N_HEADS_A * HEAD_DIM
    w_bq = N_HEADS_B * HEAD_DIM
    w_bkv = N_KV_B * HEAD_DIM
    perm = _deinterleave_perm()
    col_perm = np.arange(w_in.shape[1])
    for hd in range(N_HEADS_B + N_KV_B):
        c0 = 3 * w_a + hd * HEAD_DIM
        col_perm[c0:c0 + HEAD_DIM] = c0 + perm
    w_in_b = w_in[:, col_perm].astype(BF16)
    col_scale = jnp.concatenate([jnp.full((w_a,), SCALE * LOG2E, F32),
                                 jnp.ones((w_in.shape[1] - w_a,), F32)])[None, :]
    proj = _project(xb, w_in_b, col_scale)

    n_b = N_HEADS_B + N_KV_B
    gains = jnp.concatenate([jnp.tile(qk_gain[0][perm][None], (N_HEADS_B, 1)),
                             jnp.tile(qk_gain[1][perm][None], (N_KV_B, 1))]).astype(F32)
    scales = jnp.concatenate([jnp.full((N_HEADS_B, HEAD_DIM), SCALE * LOG2E, F32),
                              jnp.ones((N_KV_B, HEAD_DIM), F32)])
    qk = _qk_prep(proj, gains.reshape(2, n_b // 2, HEAD_DIM), scales.reshape(2, n_b // 2, HEAD_DIM),
                  rope[0], rope[1], 3 * w_a)

    table, halo_blks, win_blks = dil
    oa, ob = [], []
    v_blk0 = (3 * w_a + w_bq + w_bkv) // HEAD_DIM
    for row0, batch, seq in _group_rows(groups):
        oa.append(_win_attn(proj, table, row0=row0, batch=batch, seq=seq, n_heads=N_HEADS_A,
                            q_blk0=0, k_blk0=N_HEADS_A, v_blk0=2 * N_HEADS_A,
                            halo_blks=halo_blks, win_blks=win_blks))
        ob.append(_gqa(qk, proj, row0=row0, batch=batch, seq=seq, v_blk0=v_blk0))
    oa = jnp.concatenate(oa, axis=0)
    ob = jnp.concatenate(ob, axis=0)

    w_out_b = w_out.astype(BF16)
    x, xb = _out_ln(x, [oa, ob], [w_out_b[:w_a], w_out_b[w_a:]], ln_g[0][None], ln_b[0][None])
    return _ffn_ln(x, xb, w1.astype(BF16), w3.astype(BF16), w2.astype(BF16), ln_g[1][None], ln_b[1][None])


def _odd_layer(x, xb, groups, w_in, rpb, w_out, w_router, b_router, w1, w3, w2, ln_g, ln_b):
    t, d = x.shape
    w_c = N_HEADS_C * HEAD_DIM
    col_scale = jnp.concatenate([jnp.full((w_c,), SCALE * LOG2E, F32), jnp.ones((2 * w_c,), F32)])[None, :]
    proj = _project(xb, w_in.astype(BF16), col_scale)
    table, halo_blks, win_blks = _natten_table(rpb)
    o = []
    for row0, batch, seq in _group_rows(groups):
        o.append(_win_attn(proj, table, row0=row0, batch=batch, seq=seq, n_heads=N_HEADS_C,
                           q_blk0=0, k_blk0=N_HEADS_C, v_blk0=2 * N_HEADS_C,
                           halo_blks=halo_blks, win_blks=win_blks))
    o = jnp.concatenate(o, axis=0)

    wr = jnp.pad(w_router.astype(F32), ((0, 0), (0, ROUTER_LANES - N_EXPERTS)))
    wr_hi = wr.astype(BF16)
    wr_lo = (wr - wr_hi.astype(F32)).astype(BF16)
    br = jnp.concatenate([b_router.astype(F32), jnp.full((ROUTER_LANES - N_EXPERTS,), NEG, F32)])[None, :]
    x, xb, r = _out_ln(x, [o], [w_out.astype(BF16)], ln_g[0][None], ln_b[0][None], router=(wr_hi, wr_lo, br))

    p1, p2, tile_expert, tile_active, n_rows = _moe_routing(r, MOE_TM)
    tok = jnp.arange(t, dtype=jnp.int32)
    src = jnp.zeros((n_rows,), jnp.int32).at[p1].set(tok).at[p2].set(tok)
    xs = jnp.take(xb, src, axis=0)
    ys = _moe_experts(xs, tile_expert, tile_active, w1.astype(BF16), w3.astype(BF16), w2.astype(BF16))
    y1 = jnp.take(ys, p1, axis=0)
    y2 = jnp.take(ys, p2, axis=0)
    gates = jnp.pad(r[:, N_EXPERTS + 2:N_EXPERTS + 4], ((0, 0), (0, LANES - 2)))
    return _combine_ln(x, y1, y2, gates, ln_g[1][None], ln_b[1][None])


def kernel(x_prompt, x_sample, ln_g, ln_b, w_in_even, qk_gain_b, w_out_even, ffn_w1, ffn_w3, ffn_w2,
           w_in_odd, rpb, w_out_odd, w_router, b_router, moe_w1, moe_w3, moe_w2):
    d = x_prompt.shape[-1]
    groups = [(x_prompt.shape[0], x_prompt.shape[1]), (x_sample.shape[0], x_sample.shape[1])]
    x = jnp.concatenate([x_prompt.reshape(-1, d), x_sample.reshape(-1, d)], axis=0)
    xb = x.astype(BF16)
    rope = _rope_tables(groups)
    dil = _dilated_table()
    for i in range(ln_g.shape[0]):
        j = i // 2
        if i % 2 == 0:
            x, xb = _even_layer(x, xb, groups, w_in_even[j], qk_gain_b[j], w_out_even[j],
                                ffn_w1[j], ffn_w3[j], ffn_w2[j], ln_g[i], ln_b[i], rope, dil)
        else:
            x, xb = _odd_layer(x, xb, groups, w_in_odd[j], rpb[j], w_out_odd[j], w_router[j], b_router[j],
                               moe_w1[j], moe_w3[j], moe_w2[j], ln_g[i], ln_b[i])
    n_p = x_prompt.shape[0] * x_prompt.shape[1]
    return (x[:n_p].reshape(x_prompt.shape), x[n_p:].reshape(x_sample.shape))
```

```python
import functools
import math

import numpy as np
import jax
import jax.numpy as jnp
from jax import lax
from jax.experimental import pallas as pl
from jax.experimental.pallas import tpu as pltpu

HEAD_DIM = 128
GRID_W = 64
N_HEADS_A = 6
DILATED_BRANCHES = ((128, 1), (512, 4), (2048, 16))
N_HEADS_B = 10
N_KV_B = 2
N_HEADS_C = 16
NA_ROWS = 8
NA_COLS = 16
N_EXPERTS = 8
TOP_K = 2
DEPTH = 4
ROPE_THETA = 10000.0
LN_EPS = 1e-5
QK_EPS = 1e-6
NEG = -1e30
SCALE = HEAD_DIM ** -0.5
LOG2E = math.log2(math.e)
DN_ALPHA = (2 * DEPTH) ** 0.25

V7X_VMEM_BYTES = 64 * 2 ** 20
VMEM_LIMIT = V7X_VMEM_BYTES - 8 * 2 ** 20
LANES = 128
ATTN_TQ = 256
ATTN_ROWS = 128
ROUTER_LANES = LANES

F32 = jnp.float32
BF16 = jnp.bfloat16


def _params(*sem):
    return pltpu.CompilerParams(dimension_semantics=sem, vmem_limit_bytes=VMEM_LIMIT)


def _tile(n, pref):
    if n <= pref:
        return n
    t = (pref // LANES) * LANES
    while t >= LANES:
        if n % t == 0:
            return t
        t -= LANES
    return n


def _lane_tiles(s):
    return [s[:, t * LANES:(t + 1) * LANES] for t in range(s.shape[1] // LANES)]


def _with_ones(v):
    return jnp.concatenate([v, jnp.ones_like(v)], axis=1)


def _proj_kernel(x_ref, w_ref, cs_ref, o_ref):
    acc = jnp.dot(x_ref[...], w_ref[...], preferred_element_type=F32)
    o_ref[...] = (acc * cs_ref[...]).astype(o_ref.dtype)


def _project(xb, w, col_scale):
    t, k = xb.shape
    n = w.shape[1]
    tm, tn = _tile(t, 1024), _tile(n, 1024)
    return pl.pallas_call(
        _proj_kernel,
        grid=(t // tm, n // tn),
        in_specs=[pl.BlockSpec((tm, k), lambda i, j: (i, 0)),
                  pl.BlockSpec((k, tn), lambda i, j: (0, j)),
                  pl.BlockSpec((1, tn), lambda i, j: (0, j))],
        out_specs=pl.BlockSpec((tm, tn), lambda i, j: (i, j)),
        out_shape=jax.ShapeDtypeStruct((t, n), BF16),
        compiler_params=_params("arbitrary", "arbitrary"),
    )(xb, w, col_scale)


def _layer_norm(z, g, b):
    mu = jnp.mean(z, axis=-1, keepdims=True)
    zc = z - mu
    var = jnp.mean(zc * zc, axis=-1, keepdims=True)
    return zc * lax.rsqrt(var + LN_EPS) * g + b


def _qk_prep_kernel(p_ref, g_ref, sc_ref, cos_ref, sin_ref, o_ref, *, heads):
    c = cos_ref[...]
    s = sin_ref[...]
    for r in range(heads):
        x = p_ref[:, r * HEAD_DIM:(r + 1) * HEAD_DIM].astype(F32)
        ms = jnp.mean(x * x, axis=-1, keepdims=True)
        xn = x * lax.rsqrt(ms + QK_EPS) * g_ref[0, r:r + 1, :]
        y = xn * c + pltpu.roll(xn, HEAD_DIM // 2, 1) * s
        o_ref[:, r * HEAD_DIM:(r + 1) * HEAD_DIM] = (y * sc_ref[0, r:r + 1, :]).astype(o_ref.dtype)


def _qk_prep(proj, gains, scales, cos_t, sin_t, col0):
    t = proj.shape[0]
    n_heads = N_HEADS_B + N_KV_B
    half = n_heads // 2
    wblk = half * HEAD_DIM
    assert col0 % wblk == 0
    tm = _tile(t, 512)
    return pl.pallas_call(
        functools.partial(_qk_prep_kernel, heads=half),
        grid=(t // tm, 2),
        in_specs=[pl.BlockSpec((tm, wblk), lambda i, j: (i, col0 // wblk + j)),
                  pl.BlockSpec((1, half, HEAD_DIM), lambda i, j: (j, 0, 0)),
                  pl.BlockSpec((1, half, HEAD_DIM), lambda i, j: (j, 0, 0)),
                  pl.BlockSpec((tm, HEAD_DIM), lambda i, j: (i, 0)),
                  pl.BlockSpec((tm, HEAD_DIM), lambda i, j: (i, 0))],
        out_specs=pl.BlockSpec((tm, wblk), lambda i, j: (i, j)),
        out_shape=jax.ShapeDtypeStruct((t, n_heads * HEAD_DIM), BF16),
        compiler_params=_params("arbitrary", "arbitrary"),
    )(proj, gains, scales, cos_t, sin_t)


def _window_start_blk(i, halo_blks, win_blks, n_blks):
    return jnp.clip(i - halo_blks, 0, n_blks - win_blks)


def _win_attn_kernel(q_ref, k_ref, v_ref, tb_ref, o_ref, *, halo_blks, win_blks, n_blks, axis):
    i = pl.program_id(axis)
    tq = q_ref.shape[0]
    w = win_blks * tq
    start = pl.multiple_of(_window_start_blk(i, halo_blks, win_blks, n_blks) * tq, tq)
    k = k_ref[pl.ds(start, w), :]
    v2 = _with_ones(v_ref[pl.ds(start, w), :])
    for rb in range(tq // ATTN_ROWS):
        sl = slice(rb * ATTN_ROWS, (rb + 1) * ATTN_ROWS)
        s = lax.dot_general(q_ref[sl, :], k, (((1,), (1,)), ((), ())), preferred_element_type=F32)
        tiles = _lane_tiles(s + tb_ref[0, 0, sl, :])
        m = jnp.max(functools.reduce(jnp.maximum, tiles), axis=-1, keepdims=True)
        p = jnp.concatenate([jnp.exp2(t - m) for t in tiles], axis=1).astype(BF16)
        pv = jnp.dot(p, v2, preferred_element_type=F32)
        o_ref[sl, :] = (pv[:, :HEAD_DIM] / pv[:, HEAD_DIM:]).astype(o_ref.dtype)


def _win_attn(src, table, *, row0, batch, seq, n_heads, q_blk0, k_blk0, v_blk0, halo_blks, win_blks):
    tq = ATTN_TQ
    n_blks = seq // tq
    assert seq % tq == 0 and n_blks >= win_blks and row0 % seq == 0
    seq_blk0 = row0 // seq
    qrow0 = row0 // tq

    def variant(i):
        return i - _window_start_blk(i, halo_blks, win_blks, n_blks)

    kern = functools.partial(_win_attn_kernel, halo_blks=halo_blks, win_blks=win_blks,
                             n_blks=n_blks, axis=2)
    return pl.pallas_call(
        kern,
        grid=(batch, n_heads, n_blks),
        in_specs=[pl.BlockSpec((tq, HEAD_DIM), lambda b, h, i: (qrow0 + b * n_blks + i, q_blk0 + h)),
                  pl.BlockSpec((seq, HEAD_DIM), lambda b, h, i: (seq_blk0 + b, k_blk0 + h)),
                  pl.BlockSpec((seq, HEAD_DIM), lambda b, h, i: (seq_blk0 + b, v_blk0 + h)),
                  pl.BlockSpec((1, 1, tq, win_blks * tq), lambda b, h, i: (h, variant(i), 0, 0))],
        out_specs=pl.BlockSpec((tq, HEAD_DIM), lambda b, h, i: (b * n_blks + i, h)),
        out_shape=jax.ShapeDtypeStruct((batch * seq, n_heads * HEAD_DIM), BF16),
        compiler_params=_params("arbitrary", "arbitrary", "arbitrary"),
    )(src, src, src, table)


def _dilated_table():
    tq = ATTN_TQ
    halo = max(w // 2 for w, _ in DILATED_BRANCHES)
    halo_blks = halo // tq
    win_blks = 2 * halo_blks + 1
    v = jnp.arange(win_blks)[:, None, None]
    r = jnp.arange(tq)[None, :, None]
    c = jnp.arange(win_blks * tq)[None, None, :]
    delta = c - v * tq - r
    ad = jnp.abs(delta)
    cnt = jnp.zeros(delta.shape, F32)
    for window, dil in DILATED_BRANCHES:
        cnt = cnt + ((ad <= window // 2) & (delta % dil == 0)).astype(F32)
    slopes = 2.0 ** (-8.0 * jnp.arange(1, N_HEADS_A + 1, dtype=F32) / N_HEADS_A)
    bias = -slopes[:, None, None, None] * ad.astype(F32)[None] + jnp.log(jnp.maximum(cnt, 1.0))[None]
    table = jnp.where(cnt[None] > 0, bias * LOG2E, NEG)
    return table, halo_blks, win_blks


def _natten_table(rpb):
    tq = ATTN_TQ
    rows_per_tile = tq // GRID_W
    halo_blks = 1
    win_blks = 3
    assert NA_ROWS // 2 == rows_per_tile
    win_rows = win_blks * rows_per_tile
    qt = np.arange(win_blks)[:, None] * tq + np.arange(tq)[None, :]
    rq, qc = qt // GRID_W, qt % GRID_W
    kt = np.arange(win_blks * tq)
    rk, kc = kt // GRID_W, kt % GRID_W
    rs = np.clip(rq - NA_ROWS // 2, 0, win_rows - NA_ROWS)
    cs = np.clip(qc - NA_COLS // 2, 0, GRID_W - NA_COLS)
    valid = ((rk[None, None, :] >= rs[..., None]) & (rk[None, None, :] < rs[..., None] + NA_ROWS)
             & (kc[None, None, :] >= cs[..., None]) & (kc[None, None, :] < cs[..., None] + NA_COLS))
    ri = np.clip(rk[None, None, :] - rq[..., None] + NA_ROWS - 1, 0, 2 * NA_ROWS - 2)
    ci = np.clip(kc[None, None, :] - qc[..., None] + NA_COLS - 1, 0, 2 * NA_COLS - 2)
    flat = rpb.astype(F32).reshape(rpb.shape[0], -1)
    bias = jnp.take(flat, jnp.asarray(ri * (2 * NA_COLS - 1) + ci, jnp.int32), axis=1)
    table = jnp.where(jnp.asarray(valid)[None], bias * LOG2E, NEG)
    return table, halo_blks, win_blks


GQA_TQ = 512
GQA_TK = 1024
GQA_ROWS = 256


def _gqa_kernel(q_ref, k_ref, v_ref, o_ref, qs_ref, acc_ref, m_ref, *, rep, tk, rows):
    tq = q_ref.shape[0]
    seq = k_ref.shape[0]
    for r in range(rep):
        qs_ref[r * tq:(r + 1) * tq, :] = q_ref[:, r * HEAD_DIM:(r + 1) * HEAD_DIM]
    m_ref[...] = jnp.full(m_ref.shape, -jnp.inf, F32)
    acc_ref[...] = jnp.zeros(acc_ref.shape, F32)

    def body(c, carry):
        off = pl.multiple_of(c * tk, tk)
        k = k_ref[pl.ds(off, tk), :]
        v2 = _with_ones(v_ref[pl.ds(off, tk), :])
        for rb in range(rep * tq // rows):
            sl = slice(rb * rows, (rb + 1) * rows)
            s = lax.dot_general(qs_ref[sl, :], k, (((1,), (1,)), ((), ())), preferred_element_type=F32)
            tiles = _lane_tiles(s)
            m_prev = m_ref[sl, :]
            m_new = jnp.maximum(m_prev, jnp.max(functools.reduce(jnp.maximum, tiles), axis=-1, keepdims=True))
            alpha = jnp.exp2(m_prev - m_new)
            p = jnp.concatenate([jnp.exp2(t - m_new) for t in tiles], axis=1).astype(BF16)
            pv = jnp.dot(p, v2, preferred_element_type=F32)
            acc_ref[sl, :] = jnp.concatenate([alpha, alpha], axis=1) * acc_ref[sl, :] + pv
            m_ref[sl, :] = m_new
        return carry

    lax.fori_loop(0, seq // tk, body, 0)
    acc = acc_ref[...]
    o = acc[:, :HEAD_DIM] / acc[:, HEAD_DIM:]
    for r in range(rep):
        o_ref[:, r * HEAD_DIM:(r + 1) * HEAD_DIM] = o[r * tq:(r + 1) * tq].astype(o_ref.dtype)


def _gqa(qk, proj, *, row0, batch, seq, v_blk0):
    rep = N_HEADS_B // N_KV_B
    tq = GQA_TQ
    tk = _tile(seq, GQA_TK)
    n_blks = seq // tq
    assert seq % tq == 0 and row0 % seq == 0 and (rep * tq) % GQA_ROWS == 0
    seq_blk0 = row0 // seq
    qrow0 = row0 // tq
    wq = rep * HEAD_DIM
    return pl.pallas_call(
        functools.partial(_gqa_kernel, rep=rep, tk=tk, rows=GQA_ROWS),
        grid=(batch, N_KV_B, n_blks),
        in_specs=[pl.BlockSpec((tq, wq), lambda b, g, i: (qrow0 + b * n_blks + i, g)),
                  pl.BlockSpec((seq, HEAD_DIM), lambda b, g, i: (seq_blk0 + b, N_HEADS_B + g)),
                  pl.BlockSpec((seq, HEAD_DIM), lambda b, g, i: (seq_blk0 + b, v_blk0 + g))],
        out_specs=pl.BlockSpec((tq, wq), lambda b, g, i: (b * n_blks + i, g)),
        out_shape=jax.ShapeDtypeStruct((batch * seq, N_HEADS_B * HEAD_DIM), BF16),
        scratch_shapes=[pltpu.VMEM((rep * tq, HEAD_DIM), BF16),
                        pltpu.VMEM((rep * tq, 2 * HEAD_DIM), F32),
                        pltpu.VMEM((rep * tq, HEAD_DIM), F32)],
        compiler_params=_params("arbitrary", "arbitrary", "arbitrary"),
    )(qk, qk, proj)


def _out_ln_kernel(x_ref, *refs, n_in, router):
    ins = refs[:n_in]
    ws = refs[n_in:2 * n_in]
    g_ref, b_ref = refs[2 * n_in:2 * n_in + 2]
    rest = refs[2 * n_in + 2:]
    h = jnp.dot(ins[0][...], ws[0][...], preferred_element_type=F32)
    for a, w in zip(ins[1:], ws[1:]):
        h = h + jnp.dot(a[...], w[...], preferred_element_type=F32)
    y = _layer_norm(DN_ALPHA * x_ref[...] + h, g_ref[...], b_ref[...])
    if not router:
        y_ref, yb_ref = rest
        y_ref[...] = y
        yb_ref[...] = y.astype(BF16)
        return
    wr_hi_ref, wr_lo_ref, br_ref, y_ref, r_ref = rest
    y_hi = y.astype(BF16)
    y_lo = (y - y_hi.astype(F32)).astype(BF16)
    y_ref[...] = y
    logits = (jnp.dot(y_hi, wr_hi_ref[...], preferred_element_type=F32)
              + jnp.dot(y_lo, wr_hi_ref[...], preferred_element_type=F32)
              + jnp.dot(y_hi, wr_lo_ref[...], preferred_element_type=F32)) + br_ref[...]
    lane = lax.broadcasted_iota(jnp.int32, logits.shape, 1).astype(F32)
    m1 = jnp.max(logits, axis=-1, keepdims=True)
    i1 = jnp.min(jnp.where(logits == m1, lane, float(ROUTER_LANES)), axis=-1, keepdims=True)
    rest_l = jnp.where(lane == i1, -jnp.inf, logits)
    m2 = jnp.max(rest_l, axis=-1, keepdims=True)
    i2 = jnp.min(jnp.where(rest_l == m2, lane, float(ROUTER_LANES)), axis=-1, keepdims=True)
    e2 = jnp.exp(m2 - m1)
    den = 1.0 + e2
    g1 = 1.0 / den
    g2 = e2 / den
    sel = ((lane == i1) | (lane == i2)).astype(F32)
    e = N_EXPERTS
    out = jnp.where(lane < e, sel, 0.0)
    out = jnp.where(lane == e, i1, out)
    out = jnp.where(lane == e + 1, i2, out)
    out = jnp.where(lane == e + 2, g1, out)
    out = jnp.where(lane == e + 3, g2, out)
    r_ref[...] = out


def _out_ln(x, acts, weights, g, b, router=None):
    t, d = x.shape
    tm = _tile(t, 512)
    n_in = len(acts)
    row = lambda i: (i, 0)
    const = lambda i: (0, 0)
    in_specs = [pl.BlockSpec((tm, d), row)]
    in_specs += [pl.BlockSpec((tm, a.shape[1]), row) for a in acts]
    in_specs += [pl.BlockSpec(w.shape, const, pipeline_mode=pl.Buffered(1)) for w in weights]
    in_specs += [pl.BlockSpec((1, d), const), pl.BlockSpec((1, d), const)]
    out_specs = [pl.BlockSpec((tm, d), row)]
    out_shape = [jax.ShapeDtypeStruct((t, d), F32)]
    args = [x, *acts, *weights, g, b]
    if router is None:
        out_specs.append(pl.BlockSpec((tm, d), row))
        out_shape.append(jax.ShapeDtypeStruct((t, d), BF16))
    else:
        in_specs += [pl.BlockSpec((d, ROUTER_LANES), const), pl.BlockSpec((d, ROUTER_LANES), const),
                     pl.BlockSpec((1, ROUTER_LANES), const)]
        out_specs.append(pl.BlockSpec((tm, ROUTER_LANES), row))
        out_shape.append(jax.ShapeDtypeStruct((t, ROUTER_LANES), F32))
        args += list(router)
    return pl.pallas_call(
        functools.partial(_out_ln_kernel, n_in=n_in, router=router is not None),
        grid=(t // tm,),
        in_specs=in_specs, out_specs=out_specs, out_shape=out_shape,
        compiler_params=_params("arbitrary"),
    )(*args)


def _swiglu_partial(xb, w1, w3, w2):
    h1 = jnp.dot(xb, w1, preferred_element_type=F32)
    h3 = jnp.dot(xb, w3, preferred_element_type=F32)
    h = (h1 * (1.0 / (1.0 + jnp.exp(-h1)))) * h3
    return jnp.dot(h.astype(BF16), w2, preferred_element_type=F32)


def _ffn_ln_kernel(x_ref, xb_ref, w1_ref, w3_ref, w2_ref, g_ref, b_ref, y_ref, yb_ref, acc_ref):
    j = pl.program_id(1)
    part = _swiglu_partial(xb_ref[...], w1_ref[...], w3_ref[...], w2_ref[...])

    @pl.when(j == 0)
    def _():
        acc_ref[...] = part

    @pl.when(j > 0)
    def _():
        acc_ref[...] += part

    @pl.when(j == pl.num_programs(1) - 1)
    def _():
        y = _layer_norm(DN_ALPHA * x_ref[...] + acc_ref[...], g_ref[...], b_ref[...])
        y_ref[...] = y
        yb_ref[...] = y.astype(BF16)


def _ffn_ln(x, xb, w1, w3, w2, g, b):
    t, d = x.shape
    f = w1.shape[1]
    tm, tf = _tile(t, 512), _tile(f, 512)
    return pl.pallas_call(
        _ffn_ln_kernel,
        grid=(t // tm, f // tf),
        in_specs=[pl.BlockSpec((tm, d), lambda i, j: (i, 0)),
                  pl.BlockSpec((tm, d), lambda i, j: (i, 0)),
                  pl.BlockSpec((d, tf), lambda i, j: (0, j)),
                  pl.BlockSpec((d, tf), lambda i, j: (0, j)),
                  pl.BlockSpec((tf, d), lambda i, j: (j, 0)),
                  pl.BlockSpec((1, d), lambda i, j: (0, 0)),
                  pl.BlockSpec((1, d), lambda i, j: (0, 0))],
        out_specs=[pl.BlockSpec((tm, d), lambda i, j: (i, 0)),
                   pl.BlockSpec((tm, d), lambda i, j: (i, 0))],
        out_shape=[jax.ShapeDtypeStruct((t, d), F32), jax.ShapeDtypeStruct((t, d), BF16)],
        scratch_shapes=[pltpu.VMEM((tm, d), F32)],
        compiler_params=_params("arbitrary", "arbitrary"),
    )(x, xb, w1, w3, w2, g, b)


MOE_TM = 512
DISPATCH_TM = 512
COMBINE_TM = 256


def _dispatch_kernel(p1_ref, p2_ref, x_hbm, xs_init_hbm, xs_hbm, sem, *, tm):
    del xs_init_hbm
    base = pl.program_id(0) * tm

    def issue(t, carry):
        row = base + t
        src = x_hbm.at[pl.ds(row, 1)]
        pltpu.make_async_copy(src, xs_hbm.at[pl.ds(p1_ref[row], 1)], sem).start()
        pltpu.make_async_copy(src, xs_hbm.at[pl.ds(p2_ref[row], 1)], sem).start()
        return carry

    lax.fori_loop(0, tm, issue, 0)
    pltpu.make_async_copy(x_hbm.at[pl.ds(0, 2 * tm)], xs_hbm.at[pl.ds(0, 2 * tm)], sem).wait()


def _dispatch(x, p1, p2, n_rows):
    t, d = x.shape
    tm = _tile(t, DISPATCH_TM)
    grid_spec = pltpu.PrefetchScalarGridSpec(
        num_scalar_prefetch=2,
        grid=(t // tm,),
        in_specs=[pl.BlockSpec(memory_space=pl.ANY), pl.BlockSpec(memory_space=pl.ANY)],
        out_specs=pl.BlockSpec(memory_space=pl.ANY),
        scratch_shapes=[pltpu.SemaphoreType.DMA(())])
    return pl.pallas_call(
        functools.partial(_dispatch_kernel, tm=tm),
        grid_spec=grid_spec,
        out_shape=jax.ShapeDtypeStruct((n_rows, d), x.dtype),
        input_output_aliases={3: 0},
        compiler_params=_params("arbitrary"),
    )(p1, p2, x, jnp.zeros((n_rows, d), x.dtype))


def _moe_expert_kernel(te_ref, act_ref, xs_ref, w1_ref, w3_ref, w2_ref, y_ref, xb_ref, acc_ref):
    i = pl.program_id(0)
    j = pl.program_id(1)
    active = act_ref[i] > 0

    @pl.when(active)
    def _():
        @pl.when(j == 0)
        def _():
            xb_ref[...] = xs_ref[...].astype(BF16)

        part = _swiglu_partial(xb_ref[...], w1_ref[...], w3_ref[...], w2_ref[...])

        @pl.when(j == 0)
        def _():
            acc_ref[...] = part

        @pl.when(j > 0)
        def _():
            acc_ref[...] += part

    last = j == pl.num_programs(1) - 1

    @pl.when(last & active)
    def _():
        y_ref[...] = acc_ref[...]

    @pl.when(last & jnp.logical_not(active))
    def _():
        y_ref[...] = jnp.zeros(y_ref.shape, y_ref.dtype)


def _moe_experts(xs, tile_expert, tile_active, w1, w3, w2):
    p, d = xs.shape
    f = w1.shape[2]
    tm, tf = MOE_TM, _tile(f, 256)
    grid_spec = pltpu.PrefetchScalarGridSpec(
        num_scalar_prefetch=2,
        grid=(p // tm, f // tf),
        in_specs=[pl.BlockSpec((tm, d), lambda i, j, te, ac: (i, 0)),
                  pl.BlockSpec((None, d, tf), lambda i, j, te, ac: (te[i], 0, j)),
                  pl.BlockSpec((None, d, tf), lambda i, j, te, ac: (te[i], 0, j)),
                  pl.BlockSpec((None, tf, d), lambda i, j, te, ac: (te[i], j, 0))],
        out_specs=pl.BlockSpec((tm, d), lambda i, j, te, ac: (i, 0)),
        scratch_shapes=[pltpu.VMEM((tm, d), BF16), pltpu.VMEM((tm, d), F32)])
    return pl.pallas_call(
        _moe_expert_kernel,
        grid_spec=grid_spec,
        out_shape=jax.ShapeDtypeStruct((p, d), F32),
        compiler_params=_params("arbitrary", "arbitrary"),
    )(tile_expert, tile_active, xs, w1, w3, w2)


def _combine_ln_kernel(p1_ref, p2_ref, x_ref, gt_ref, g_ref, b_ref, ys_hbm, y_ref, yb_ref, buf, sem, *, tm):
    i = pl.program_id(0)
    n = pl.num_programs(0)

    def issue(tile, slot):
        base = tile * tm

        def body(t, carry):
            pltpu.make_async_copy(ys_hbm.at[pl.ds(p1_ref[base + t], 1)],
                                  buf.at[slot, pl.ds(t, 1)], sem.at[slot]).start()
            pltpu.make_async_copy(ys_hbm.at[pl.ds(p2_ref[base + t], 1)],
                                  buf.at[slot, pl.ds(tm + t, 1)], sem.at[slot]).start()
            return carry

        lax.fori_loop(0, tm, body, 0)

    @pl.when(i == 0)
    def _():
        issue(0, 0)

    @pl.when(i + 1 < n)
    def _():
        issue(i + 1, (i + 1) % 2)

    slot = i % 2
    pltpu.make_async_copy(ys_hbm.at[pl.ds(0, 2 * tm)], buf.at[slot], sem.at[slot]).wait()
    g1 = gt_ref[:, 0:1]
    g2 = gt_ref[:, 1:2]
    f = g1 * buf[slot, pl.ds(0, tm), :] + g2 * buf[slot, pl.ds(tm, tm), :]
    y = _layer_norm(DN_ALPHA * x_ref[...] + f, g_ref[...], b_ref[...])
    y_ref[...] = y
    yb_ref[...] = y.astype(BF16)


def _combine_ln(x, ys, p1, p2, gates, g, b):
    t, d = x.shape
    tm = _tile(t, COMBINE_TM)
    row = lambda i, a, c: (i, 0)
    const = lambda i, a, c: (0, 0)
    grid_spec = pltpu.PrefetchScalarGridSpec(
        num_scalar_prefetch=2,
        grid=(t // tm,),
        in_specs=[pl.BlockSpec((tm, d), row), pl.BlockSpec((tm, LANES), row),
                  pl.BlockSpec((1, d), const), pl.BlockSpec((1, d), const),
                  pl.BlockSpec(memory_space=pl.ANY)],
        out_specs=[pl.BlockSpec((tm, d), row), pl.BlockSpec((tm, d), row)],
        scratch_shapes=[pltpu.VMEM((2, 2 * tm, d), F32), pltpu.SemaphoreType.DMA((2,))])
    return pl.pallas_call(
        functools.partial(_combine_ln_kernel, tm=tm),
        grid_spec=grid_spec,
        out_shape=[jax.ShapeDtypeStruct((t, d), F32), jax.ShapeDtypeStruct((t, d), BF16)],
        compiler_params=_params("arbitrary"),
    )(p1, p2, x, gates, g, b, ys)


def _moe_routing(r, tm):
    t = r.shape[0]
    e = N_EXPERTS
    sel = r[:, :e].astype(jnp.int32)
    i1 = r[:, e].astype(jnp.int32)
    i2 = r[:, e + 1].astype(jnp.int32)
    cnt = jnp.cumsum(sel, axis=0)
    rank = cnt - sel
    padded = ((cnt[-1] + tm - 1) // tm) * tm
    ends = jnp.cumsum(padded)
    pos = (ends - padded)[None, :] + rank
    lane = jnp.arange(e, dtype=jnp.int32)[None, :]
    p1 = jnp.sum(jnp.where(lane == i1[:, None], pos, 0), axis=1).astype(jnp.int32)
    p2 = jnp.sum(jnp.where(lane == i2[:, None], pos, 0), axis=1).astype(jnp.int32)
    n_rows = TOP_K * t + e * tm
    tile_start = jnp.arange(n_rows // tm, dtype=jnp.int32) * tm
    tile_expert = jnp.sum((tile_start[:, None] >= ends[None, :]).astype(jnp.int32), axis=1)
    tile_expert = jnp.minimum(tile_expert, e - 1).astype(jnp.int32)
    tile_active = (tile_start < ends[-1]).astype(jnp.int32)
    return p1, p2, tile_expert, tile_active, n_rows


def _deinterleave_perm():
    half = HEAD_DIM // 2
    return np.concatenate([np.arange(half) * 2, np.arange(half) * 2 + 1])


def _rope_tables(groups):
    pos = np.concatenate([np.tile(np.arange(seq), batch) for batch, seq in groups])
    pos = jnp.asarray(pos, jnp.int32)
    row = (pos // GRID_W).astype(F32)
    col = (pos % GRID_W).astype(F32)
    axis_dim = HEAD_DIM // 2
    inv = ROPE_THETA ** (-jnp.arange(0, axis_dim, 2, dtype=F32) / axis_dim)
    ang = jnp.concatenate([row[:, None] * inv, col[:, None] * inv], axis=-1)
    cos, sin = jnp.cos(ang), jnp.sin(ang)
    return jnp.concatenate([cos, cos], axis=-1), jnp.concatenate([-sin, sin], axis=-1)


def _group_rows(groups):
    out, row0 = [], 0
    for batch, seq in groups:
        out.append((row0, batch, seq))
        row0 += batch * seq
    return out


def _even_layer(x, xb, groups, w_in, qk_gain, w_out, w1, w3, w2, ln_g, ln_b, rope, dil):
    w_a = N_HEADS_A * HEAD_DIM
    w_bq = N_HEADS_B * HEAD_DIM
    w_bkv = N_KV_B * HEAD_DIM
    perm = _deinterleave_perm()
    col_perm = np.arange(w_in.shape[1])
    for hd in range(N_HEADS_B + N_KV_B):
        c0 = 3 * w_a + hd * HEAD_DIM
        col_perm[c0:c0 + HEAD_DIM] = c0 + perm
    w_in_b = w_in[:, col_perm].astype(BF16)
    col_scale = jnp.concatenate([jnp.full((w_a,), SCALE * LOG2E, F32),
                                 jnp.ones((w_in.shape[1] - w_a,), F32)])[None, :]
    proj = _project(xb, w_in_b, col_scale)

    n_b = N_HEADS_B + N_KV_B
    gains = jnp.concatenate([jnp.tile(qk_gain[0][perm][None], (N_HEADS_B, 1)),
                             jnp.tile(qk_gain[1][perm][None], (N_KV_B, 1))]).astype(F32)
    scales = jnp.concatenate([jnp.full((N_HEADS_B, HEAD_DIM), SCALE * LOG2E, F32),
                              jnp.ones((N_KV_B, HEAD_DIM), F32)])
    qk = _qk_prep(proj, gains.reshape(2, n_b // 2, HEAD_DIM), scales.reshape(2, n_b // 2, HEAD_DIM),
                  rope[0], rope[1], 3 * w_a)

    table, halo_blks, win_blks = dil
    oa, ob = [], []
    v_blk0 = (3 * w_a + w_bq + w_bkv) // HEAD_DIM
    for row0, batch, seq in _group_rows(groups):
        oa.append(_win_attn(proj, table, row0=row0, batch=batch, seq=seq, n_heads=N_HEADS_A,
                            q_blk0=0, k_blk0=N_HEADS_A, v_blk0=2 * N_HEADS_A,
                            halo_blks=halo_blks, win_blks=win_blks))
        ob.append(_gqa(qk, proj, row0=row0, batch=batch, seq=seq, v_blk0=v_blk0))
    oa = jnp.concatenate(oa, axis=0)
    ob = jnp.concatenate(ob, axis=0)

    w_out_b = w_out.astype(BF16)
    x, xb = _out_ln(x, [oa, ob], [w_out_b[:w_a], w_out_b[w_a:]], ln_g[0][None], ln_b[0][None])
    return _ffn_ln(x, xb, w1.astype(BF16), w3.astype(BF16), w2.astype(BF16), ln_g[1][None], ln_b[1][None])


def _odd_layer(x, xb, groups, w_in, rpb, w_out, w_router, b_router, w1, w3, w2, ln_g, ln_b):
    w_c = N_HEADS_C * HEAD_DIM
    col_scale = jnp.concatenate([jnp.full((w_c,), SCALE * LOG2E, F32), jnp.ones((2 * w_c,), F32)])[None, :]
    proj = _project(xb, w_in.astype(BF16), col_scale)
    table, halo_blks, win_blks = _natten_table(rpb)
    o = []
    for row0, batch, seq in _group_rows(groups):
        o.append(_win_attn(proj, table, row0=row0, batch=batch, seq=seq, n_heads=N_HEADS_C,
                           q_blk0=0, k_blk0=N_HEADS_C, v_blk0=2 * N_HEADS_C,
                           halo_blks=halo_blks, win_blks=win_blks))
    o = jnp.concatenate(o, axis=0)

    wr = jnp.pad(w_router.astype(F32), ((0, 0), (0, ROUTER_LANES - N_EXPERTS)))
    wr_hi = wr.astype(BF16)
    wr_lo = (wr - wr_hi.astype(F32)).astype(BF16)
    br = jnp.concatenate([b_router.astype(F32), jnp.full((ROUTER_LANES - N_EXPERTS,), NEG, F32)])[None, :]
    x, r = _out_ln(x, [o], [w_out.astype(BF16)], ln_g[0][None], ln_b[0][None], router=(wr_hi, wr_lo, br))

    p1, p2, tile_expert, tile_active, n_rows = _moe_routing(r, MOE_TM)
    xs = _dispatch(x, p1, p2, n_rows)
    ys = _moe_experts(xs, tile_expert, tile_active, w1.astype(BF16), w3.astype(BF16), w2.astype(BF16))
    gates = jnp.pad(r[:, N_EXPERTS + 2:N_EXPERTS + 4], ((0, 0), (0, LANES - 2)))
    return _combine_ln(x, ys, p1, p2, gates, ln_g[1][None], ln_b[1][None])


def kernel(x_prompt, x_sample, ln_g, ln_b, w_in_even, qk_gain_b, w_out_even, ffn_w1, ffn_w3, ffn_w2,
           w_in_odd, rpb, w_out_odd, w_router, b_router, moe_w1, moe_w3, moe_w2):
    d = x_prompt.shape[-1]
    groups = [(x_prompt.shape[0], x_prompt.shape[1]), (x_sample.shape[0], x_sample.shape[1])]
    x = jnp.concatenate([x_prompt.reshape(-1, d), x_sample.reshape(-1, d)], axis=0)
    xb = x.astype(BF16)
    rope = _rope_tables(groups)
    dil = _dilated_table()
    for i in range(ln_g.shape[0]):
        j = i // 2
        if i % 2 == 0:
            x, xb = _even_layer(x, xb, groups, w_in_even[j], qk_gain_b[j], w_out_even[j],
                                ffn_w1[j], ffn_w3[j], ffn_w2[j], ln_g[i], ln_b[i], rope, dil)
        else:
            x, xb = _odd_layer(x, xb, groups, w_in_odd[j], rpb[j], w_out_odd[j], w_router[j], b_router[j],
                               moe_w1[j], moe_w3[j], moe_w2[j], ln_g[i], ln_b[i])
    n_p = x_prompt.shape[0] * x_prompt.shape[1]
    return (x[:n_p].reshape(x_prompt.shape), x[n_p:].reshape(x_sample.shape))
```

```python
import functools
import math

import numpy as np
import jax
import jax.numpy as jnp
from jax import lax
from jax.experimental import pallas as pl
from jax.experimental.pallas import tpu as pltpu

HEAD_DIM = 128
GRID_W = 64
N_HEADS_A = 6
DILATED_BRANCHES = ((128, 1), (512, 4), (2048, 16))
N_HEADS_B = 10
N_KV_B = 2
N_HEADS_C = 16
NA_ROWS = 8
NA_COLS = 16
N_EXPERTS = 8
TOP_K = 2
DEPTH = 4
ROPE_THETA = 10000.0
LN_EPS = 1e-5
QK_EPS = 1e-6
NEG = -1e30
SCALE = HEAD_DIM ** -0.5
LOG2E = math.log2(math.e)
DN_ALPHA = (2 * DEPTH) ** 0.25

V7X_VMEM_BYTES = 64 * 2 ** 20
VMEM_LIMIT = V7X_VMEM_BYTES - 8 * 2 ** 20
LANES = 128
ATTN_TQ = 256
ATTN_ROWS = 128
HEADS_PER_STEP_A = 2
HEADS_PER_STEP_C = 4
ROUTER_LANES = LANES

F32 = jnp.float32
BF16 = jnp.bfloat16


def _params(*sem):
    return pltpu.CompilerParams(dimension_semantics=sem, vmem_limit_bytes=VMEM_LIMIT)


def _tile(n, pref):
    if n <= pref:
        return n
    t = (pref // LANES) * LANES
    while t >= LANES:
        if n % t == 0:
            return t
        t -= LANES
    return n


def _lane_tiles(s):
    return [s[:, t * LANES:(t + 1) * LANES] for t in range(s.shape[1] // LANES)]


def _with_ones(v):
    return jnp.concatenate([v, jnp.ones_like(v)], axis=1)


def _proj_kernel(x_ref, w_ref, cs_ref, o_ref):
    acc = jnp.dot(x_ref[...], w_ref[...], preferred_element_type=F32)
    o_ref[...] = (acc * cs_ref[...]).astype(o_ref.dtype)


def _project(xb, w, col_scale):
    t, k = xb.shape
    n = w.shape[1]
    tm, tn = _tile(t, 1024), _tile(n, 1024)
    return pl.pallas_call(
        _proj_kernel,
        grid=(t // tm, n // tn),
        in_specs=[pl.BlockSpec((tm, k), lambda i, j: (i, 0)),
                  pl.BlockSpec((k, tn), lambda i, j: (0, j)),
                  pl.BlockSpec((1, tn), lambda i, j: (0, j))],
        out_specs=pl.BlockSpec((tm, tn), lambda i, j: (i, j)),
        out_shape=jax.ShapeDtypeStruct((t, n), BF16),
        compiler_params=_params("arbitrary", "arbitrary"),
    )(xb, w, col_scale)


def _layer_norm(z, g, b):
    mu = jnp.mean(z, axis=-1, keepdims=True)
    zc = z - mu
    var = jnp.mean(zc * zc, axis=-1, keepdims=True)
    return zc * lax.rsqrt(var + LN_EPS) * g + b


def _qk_prep_kernel(p_ref, g_ref, sc_ref, cos_ref, sin_ref, o_ref, *, heads):
    c = cos_ref[...]
    s = sin_ref[...]
    for r in range(heads):
        x = p_ref[:, r * HEAD_DIM:(r + 1) * HEAD_DIM].astype(F32)
        ms = jnp.mean(x * x, axis=-1, keepdims=True)
        xn = x * lax.rsqrt(ms + QK_EPS) * g_ref[0, r:r + 1, :]
        y = xn * c + pltpu.roll(xn, HEAD_DIM // 2, 1) * s
        o_ref[:, r * HEAD_DIM:(r + 1) * HEAD_DIM] = (y * sc_ref[0, r:r + 1, :]).astype(o_ref.dtype)


def _qk_prep(proj, gains, scales, cos_t, sin_t, col0):
    t = proj.shape[0]
    n_heads = N_HEADS_B + N_KV_B
    half = n_heads // 2
    wblk = half * HEAD_DIM
    assert col0 % wblk == 0
    tm = _tile(t, 512)
    return pl.pallas_call(
        functools.partial(_qk_prep_kernel, heads=half),
        grid=(t // tm, 2),
        in_specs=[pl.BlockSpec((tm, wblk), lambda i, j: (i, col0 // wblk + j)),
                  pl.BlockSpec((1, half, HEAD_DIM), lambda i, j: (j, 0, 0)),
                  pl.BlockSpec((1, half, HEAD_DIM), lambda i, j: (j, 0, 0)),
                  pl.BlockSpec((tm, HEAD_DIM), lambda i, j: (i, 0)),
                  pl.BlockSpec((tm, HEAD_DIM), lambda i, j: (i, 0))],
        out_specs=pl.BlockSpec((tm, wblk), lambda i, j: (i, j)),
        out_shape=jax.ShapeDtypeStruct((t, n_heads * HEAD_DIM), BF16),
        compiler_params=_params("arbitrary", "arbitrary"),
    )(proj, gains, scales, cos_t, sin_t)


def _window_start_blk(i, halo_blks, win_blks, n_blks):
    return jnp.clip(i - halo_blks, 0, n_blks - win_blks)


def _win_attn_kernel(q_ref, k_ref, v_ref, tb_ref, o_ref, *, halo_blks, win_blks, n_blks, axis):
    i = pl.program_id(axis)
    tq = q_ref.shape[0]
    w = win_blks * tq
    start = pl.multiple_of(_window_start_blk(i, halo_blks, win_blks, n_blks) * tq, tq)
    hp = tb_ref.shape[0]
    for hd in range(hp):
        cols = slice(hd * HEAD_DIM, (hd + 1) * HEAD_DIM)
        k = k_ref[pl.ds(start, w), cols]
        v2 = _with_ones(v_ref[pl.ds(start, w), cols])
        for rb in range(tq // ATTN_ROWS):
            sl = slice(rb * ATTN_ROWS, (rb + 1) * ATTN_ROWS)
            s = lax.dot_general(q_ref[sl, cols], k, (((1,), (1,)), ((), ())), preferred_element_type=F32)
            tiles = _lane_tiles(s + tb_ref[hd, 0, sl, :])
            m = jnp.max(functools.reduce(jnp.maximum, tiles), axis=-1, keepdims=True)
            p = jnp.concatenate([jnp.exp2(t - m) for t in tiles], axis=1).astype(BF16)
            pv = jnp.dot(p, v2, preferred_element_type=F32)
            o_ref[sl, cols] = (pv[:, :HEAD_DIM] / pv[:, HEAD_DIM:]).astype(o_ref.dtype)


def _win_attn(src, table, *, row0, batch, seq, n_heads, q_blk0, k_blk0, v_blk0, halo_blks, win_blks, hp):
    tq = ATTN_TQ
    n_blks = seq // tq
    assert seq % tq == 0 and n_blks >= win_blks and row0 % seq == 0
    assert n_heads % hp == 0 and q_blk0 % hp == 0 and k_blk0 % hp == 0 and v_blk0 % hp == 0
    seq_blk0 = row0 // seq
    qrow0 = row0 // tq
    wh = hp * HEAD_DIM
    qb, kb, vb = q_blk0 // hp, k_blk0 // hp, v_blk0 // hp

    def variant(i):
        return i - _window_start_blk(i, halo_blks, win_blks, n_blks)

    kern = functools.partial(_win_attn_kernel, halo_blks=halo_blks, win_blks=win_blks,
                             n_blks=n_blks, axis=2)
    return pl.pallas_call(
        kern,
        grid=(batch, n_heads // hp, n_blks),
        in_specs=[pl.BlockSpec((tq, wh), lambda b, h, i: (qrow0 + b * n_blks + i, qb + h)),
                  pl.BlockSpec((seq, wh), lambda b, h, i: (seq_blk0 + b, kb + h), pipeline_mode=pl.Buffered(1)),
                  pl.BlockSpec((seq, wh), lambda b, h, i: (seq_blk0 + b, vb + h), pipeline_mode=pl.Buffered(1)),
                  pl.BlockSpec((hp, 1, tq, win_blks * tq), lambda b, h, i: (h, variant(i), 0, 0))],
        out_specs=pl.BlockSpec((tq, wh), lambda b, h, i: (b * n_blks + i, h)),
        out_shape=jax.ShapeDtypeStruct((batch * seq, n_heads * HEAD_DIM), BF16),
        compiler_params=_params("arbitrary", "arbitrary", "arbitrary"),
    )(src, src, src, table)


def _dilated_table():
    tq = ATTN_TQ
    halo = max(w // 2 for w, _ in DILATED_BRANCHES)
    halo_blks = halo // tq
    win_blks = 2 * halo_blks + 1
    v = jnp.arange(win_blks)[:, None, None]
    r = jnp.arange(tq)[None, :, None]
    c = jnp.arange(win_blks * tq)[None, None, :]
    delta = c - v * tq - r
    ad = jnp.abs(delta)
    cnt = jnp.zeros(delta.shape, F32)
    for window, dil in DILATED_BRANCHES:
        cnt = cnt + ((ad <= window // 2) & (delta % dil == 0)).astype(F32)
    slopes = 2.0 ** (-8.0 * jnp.arange(1, N_HEADS_A + 1, dtype=F32) / N_HEADS_A)
    bias = -slopes[:, None, None, None] * ad.astype(F32)[None] + jnp.log(jnp.maximum(cnt, 1.0))[None]
    table = jnp.where(cnt[None] > 0, bias * LOG2E, NEG)
    return table, halo_blks, win_blks


def _natten_table(rpb):
    tq = ATTN_TQ
    rows_per_tile = tq // GRID_W
    halo_blks = 1
    win_blks = 3
    assert NA_ROWS // 2 == rows_per_tile
    win_rows = win_blks * rows_per_tile
    qt = np.arange(win_blks)[:, None] * tq + np.arange(tq)[None, :]
    rq, qc = qt // GRID_W, qt % GRID_W
    kt = np.arange(win_blks * tq)
    rk, kc = kt // GRID_W, kt % GRID_W
    rs = np.clip(rq - NA_ROWS // 2, 0, win_rows - NA_ROWS)
    cs = np.clip(qc - NA_COLS // 2, 0, GRID_W - NA_COLS)
    valid = ((rk[None, None, :] >= rs[..., None]) & (rk[None, None, :] < rs[..., None] + NA_ROWS)
             & (kc[None, None, :] >= cs[..., None]) & (kc[None, None, :] < cs[..., None] + NA_COLS))
    n_ro, n_co = 2 * NA_ROWS - 1, 2 * NA_COLS - 1
    rows = np.arange(win_rows)
    ri = np.clip(rows[None, :] - rows[:, None] + NA_ROWS - 1, 0, n_ro - 1)
    cols = np.arange(GRID_W)
    ci = np.clip(cols[None, :] - cols[:, None] + NA_COLS - 1, 0, n_co - 1)
    onehot = (np.arange(n_co)[:, None, None] == ci[None]).astype(np.float32).reshape(n_co, GRID_W * GRID_W)
    t1 = rpb.astype(F32)[:, ri.reshape(-1), :]
    hi = t1.astype(BF16)
    mid = (t1 - hi.astype(F32)).astype(BF16)
    lo = (t1 - hi.astype(F32) - mid.astype(F32)).astype(BF16)
    oh = jnp.asarray(onehot, BF16)
    t2 = sum(jnp.einsum('hpb,bq->hpq', part, oh, preferred_element_type=F32) for part in (hi, mid, lo))
    t2 = t2.reshape(-1, win_rows, win_rows, GRID_W, GRID_W).transpose(0, 1, 3, 2, 4)
    bias = t2.reshape(-1, win_blks, tq, win_blks * tq)
    table = jnp.where(jnp.asarray(valid)[None], bias * LOG2E, NEG)
    return table, halo_blks, win_blks


GQA_TQ = 512
GQA_TK = 1024
GQA_ROWS = 256


def _gqa_kernel(q_ref, k_ref, v_ref, o_ref, qs_ref, acc_ref, m_ref, *, rep, tk, rows):
    tq = q_ref.shape[0]
    seq = k_ref.shape[0]
    for r in range(rep):
        qs_ref[r * tq:(r + 1) * tq, :] = q_ref[:, r * HEAD_DIM:(r + 1) * HEAD_DIM]
    m_ref[...] = jnp.full(m_ref.shape, -jnp.inf, F32)
    acc_ref[...] = jnp.zeros(acc_ref.shape, F32)

    def body(c, carry):
        off = pl.multiple_of(c * tk, tk)
        k = k_ref[pl.ds(off, tk), :]
        v2 = _with_ones(v_ref[pl.ds(off, tk), :])
        for rb in range(rep * tq // rows):
            sl = slice(rb * rows, (rb + 1) * rows)
            s = lax.dot_general(qs_ref[sl, :], k, (((1,), (1,)), ((), ())), preferred_element_type=F32)
            tiles = _lane_tiles(s)
            m_prev = m_ref[sl, :]
            m_new = jnp.maximum(m_prev, jnp.max(functools.reduce(jnp.maximum, tiles), axis=-1, keepdims=True))
            alpha = jnp.exp2(m_prev - m_new)
            p = jnp.concatenate([jnp.exp2(t - m_new) for t in tiles], axis=1).astype(BF16)
            pv = jnp.dot(p, v2, preferred_element_type=F32)
            acc_ref[sl, :] = jnp.concatenate([alpha, alpha], axis=1) * acc_ref[sl, :] + pv
            m_ref[sl, :] = m_new
        return carry

    lax.fori_loop(0, seq // tk, body, 0)
    acc = acc_ref[...]
    o = acc[:, :HEAD_DIM] / acc[:, HEAD_DIM:]
    for r in range(rep):
        o_ref[:, r * HEAD_DIM:(r + 1) * HEAD_DIM] = o[r * tq:(r + 1) * tq].astype(o_ref.dtype)


def _gqa(qk, proj, *, row0, batch, seq, v_blk0):
    rep = N_HEADS_B // N_KV_B
    tq = GQA_TQ
    tk = _tile(seq, GQA_TK)
    n_blks = seq // tq
    assert seq % tq == 0 and row0 % seq == 0 and (rep * tq) % GQA_ROWS == 0
    seq_blk0 = row0 // seq
    qrow0 = row0 // tq
    wq = rep * HEAD_DIM
    return pl.pallas_call(
        functools.partial(_gqa_kernel, rep=rep, tk=tk, rows=GQA_ROWS),
        grid=(batch, N_KV_B, n_blks),
        in_specs=[pl.BlockSpec((tq, wq), lambda b, g, i: (qrow0 + b * n_blks + i, g)),
                  pl.BlockSpec((seq, HEAD_DIM), lambda b, g, i: (seq_blk0 + b, N_HEADS_B + g)),
                  pl.BlockSpec((seq, HEAD_DIM), lambda b, g, i: (seq_blk0 + b, v_blk0 + g))],
        out_specs=pl.BlockSpec((tq, wq), lambda b, g, i: (b * n_blks + i, g)),
        out_shape=jax.ShapeDtypeStruct((batch * seq, N_HEADS_B * HEAD_DIM), BF16),
        scratch_shapes=[pltpu.VMEM((rep * tq, HEAD_DIM), BF16),
                        pltpu.VMEM((rep * tq, 2 * HEAD_DIM), F32),
                        pltpu.VMEM((rep * tq, HEAD_DIM), F32)],
        compiler_params=_params("arbitrary", "arbitrary", "arbitrary"),
    )(qk, qk, proj)


def _out_ln_kernel(x_ref, *refs, n_in, router):
    ins = refs[:n_in]
    ws = refs[n_in:2 * n_in]
    g_ref, b_ref = refs[2 * n_in:2 * n_in + 2]
    rest = refs[2 * n_in + 2:]
    h = jnp.dot(ins[0][...], ws[0][...], preferred_element_type=F32)
    for a, w in zip(ins[1:], ws[1:]):
        h = h + jnp.dot(a[...], w[...], preferred_element_type=F32)
    y = _layer_norm(DN_ALPHA * x_ref[...] + h, g_ref[...], b_ref[...])
    if not router:
        y_ref, yb_ref = rest
        y_ref[...] = y
        yb_ref[...] = y.astype(BF16)
        return
    wr_hi_ref, wr_lo_ref, br_ref, y_ref, r_ref = rest
    y_hi = y.astype(BF16)
    y_lo = (y - y_hi.astype(F32)).astype(BF16)
    y_ref[...] = y
    logits = (jnp.dot(y_hi, wr_hi_ref[...], preferred_element_type=F32)
              + jnp.dot(y_lo, wr_hi_ref[...], preferred_element_type=F32)
              + jnp.dot(y_hi, wr_lo_ref[...], preferred_element_type=F32)) + br_ref[...]
    lane = lax.broadcasted_iota(jnp.int32, logits.shape, 1).astype(F32)
    m1 = jnp.max(logits, axis=-1, keepdims=True)
    i1 = jnp.min(jnp.where(logits == m1, lane, float(ROUTER_LANES)), axis=-1, keepdims=True)
    rest_l = jnp.where(lane == i1, -jnp.inf, logits)
    m2 = jnp.max(rest_l, axis=-1, keepdims=True)
    i2 = jnp.min(jnp.where(rest_l == m2, lane, float(ROUTER_LANES)), axis=-1, keepdims=True)
    e2 = jnp.exp(m2 - m1)
    den = 1.0 + e2
    g1 = 1.0 / den
    g2 = e2 / den
    sel = ((lane == i1) | (lane == i2)).astype(F32)
    e = N_EXPERTS
    out = jnp.where(lane < e, sel, 0.0)
    out = jnp.where(lane == e, i1, out)
    out = jnp.where(lane == e + 1, i2, out)
    out = jnp.where(lane == e + 2, g1, out)
    out = jnp.where(lane == e + 3, g2, out)
    r_ref[...] = out


def _out_ln(x, acts, weights, g, b, router=None):
    t, d = x.shape
    tm = _tile(t, 512)
    n_in = len(acts)
    row = lambda i: (i, 0)
    const = lambda i: (0, 0)
    in_specs = [pl.BlockSpec((tm, d), row)]
    in_specs += [pl.BlockSpec((tm, a.shape[1]), row) for a in acts]
    in_specs += [pl.BlockSpec(w.shape, const, pipeline_mode=pl.Buffered(1)) for w in weights]
    in_specs += [pl.BlockSpec((1, d), const), pl.BlockSpec((1, d), const)]
    out_specs = [pl.BlockSpec((tm, d), row)]
    out_shape = [jax.ShapeDtypeStruct((t, d), F32)]
    args = [x, *acts, *weights, g, b]
    if router is None:
        out_specs.append(pl.BlockSpec((tm, d), row))
        out_shape.append(jax.ShapeDtypeStruct((t, d), BF16))
    else:
        in_specs += [pl.BlockSpec((d, ROUTER_LANES), const), pl.BlockSpec((d, ROUTER_LANES), const),
                     pl.BlockSpec((1, ROUTER_LANES), const)]
        out_specs.append(pl.BlockSpec((tm, ROUTER_LANES), row))
        out_shape.append(jax.ShapeDtypeStruct((t, ROUTER_LANES), F32))
        args += list(router)
    return pl.pallas_call(
        functools.partial(_out_ln_kernel, n_in=n_in, router=router is not None),
        grid=(t // tm,),
        in_specs=in_specs, out_specs=out_specs, out_shape=out_shape,
        compiler_params=_params("arbitrary"),
    )(*args)


def _swiglu_partial(xb, w1, w3, w2):
    h1 = jnp.dot(xb, w1, preferred_element_type=F32)
    h3 = jnp.dot(xb, w3, preferred_element_type=F32)
    h = (h1 * (1.0 / (1.0 + jnp.exp(-h1)))) * h3
    return jnp.dot(h.astype(BF16), w2, preferred_element_type=F32)


def _ffn_ln_kernel(x_ref, xb_ref, w1_ref, w3_ref, w2_ref, g_ref, b_ref, y_ref, yb_ref, acc_ref):
    j = pl.program_id(1)
    part = _swiglu_partial(xb_ref[...], w1_ref[...], w3_ref[...], w2_ref[...])

    @pl.when(j == 0)
    def _():
        acc_ref[...] = part

    @pl.when(j > 0)
    def _():
        acc_ref[...] += part

    @pl.when(j == pl.num_programs(1) - 1)
    def _():
        y = _layer_norm(DN_ALPHA * x_ref[...] + acc_ref[...], g_ref[...], b_ref[...])
        y_ref[...] = y
        yb_ref[...] = y.astype(BF16)


def _ffn_ln(x, xb, w1, w3, w2, g, b):
    t, d = x.shape
    f = w1.shape[1]
    tm, tf = _tile(t, 512), _tile(f, 512)
    return pl.pallas_call(
        _ffn_ln_kernel,
        grid=(t // tm, f // tf),
        in_specs=[pl.BlockSpec((tm, d), lambda i, j: (i, 0)),
                  pl.BlockSpec((tm, d), lambda i, j: (i, 0)),
                  pl.BlockSpec((d, tf), lambda i, j: (0, j)),
                  pl.BlockSpec((d, tf), lambda i, j: (0, j)),
                  pl.BlockSpec((tf, d), lambda i, j: (j, 0)),
                  pl.BlockSpec((1, d), lambda i, j: (0, 0)),
                  pl.BlockSpec((1, d), lambda i, j: (0, 0))],
        out_specs=[pl.BlockSpec((tm, d), lambda i, j: (i, 0)),
                   pl.BlockSpec((tm, d), lambda i, j: (i, 0))],
        out_shape=[jax.ShapeDtypeStruct((t, d), F32), jax.ShapeDtypeStruct((t, d), BF16)],
        scratch_shapes=[pltpu.VMEM((tm, d), F32)],
        compiler_params=_params("arbitrary", "arbitrary"),
    )(x, xb, w1, w3, w2, g, b)


MOE_TM = 512
DISPATCH_TM = 512
COMBINE_TM = 256


def _dispatch_kernel(p1_ref, p2_ref, x_ref, xs_init_hbm, xs_hbm, sem, *, tm):
    del xs_init_hbm
    base = pl.program_id(0) * tm

    def issue(t, carry):
        src = x_ref.at[pl.ds(t, 1)]
        pltpu.make_async_copy(src, xs_hbm.at[pl.ds(p1_ref[base + t], 1)], sem).start()
        pltpu.make_async_copy(src, xs_hbm.at[pl.ds(p2_ref[base + t], 1)], sem).start()
        return carry

    lax.fori_loop(0, tm, issue, 0)
    for _ in range(TOP_K):
        pltpu.make_async_copy(x_ref, xs_hbm.at[pl.ds(0, tm)], sem).wait()


def _dispatch(x, p1, p2, n_rows):
    t, d = x.shape
    tm = _tile(t, DISPATCH_TM)
    grid_spec = pltpu.PrefetchScalarGridSpec(
        num_scalar_prefetch=2,
        grid=(t // tm,),
        in_specs=[pl.BlockSpec((tm, d), lambda i, a, c: (i, 0)), pl.BlockSpec(memory_space=pl.ANY)],
        out_specs=pl.BlockSpec(memory_space=pl.ANY),
        scratch_shapes=[pltpu.SemaphoreType.DMA(())])
    return pl.pallas_call(
        functools.partial(_dispatch_kernel, tm=tm),
        grid_spec=grid_spec,
        out_shape=jax.ShapeDtypeStruct((n_rows, d), x.dtype),
        input_output_aliases={3: 0},
        compiler_params=_params("arbitrary"),
    )(p1, p2, x, jnp.zeros((n_rows, d), x.dtype))


def _moe_expert_kernel(te_ref, act_ref, xs_ref, w1_ref, w3_ref, w2_ref, y_ref, xb_ref, acc_ref):
    i = pl.program_id(0)
    j = pl.program_id(1)
    active = act_ref[i] > 0

    @pl.when(active)
    def _():
        @pl.when(j == 0)
        def _():
            xb_ref[...] = xs_ref[...].astype(BF16)

        part = _swiglu_partial(xb_ref[...], w1_ref[...], w3_ref[...], w2_ref[...])

        @pl.when(j == 0)
        def _():
            acc_ref[...] = part

        @pl.when(j > 0)
        def _():
            acc_ref[...] += part

    last = j == pl.num_programs(1) - 1

    @pl.when(last & active)
    def _():
        y_ref[...] = acc_ref[...]

    @pl.when(last & jnp.logical_not(active))
    def _():
        y_ref[...] = jnp.zeros(y_ref.shape, y_ref.dtype)


def _moe_experts(xs, tile_expert, tile_active, w1, w3, w2):
    p, d = xs.shape
    f = w1.shape[2]
    tm, tf = MOE_TM, _tile(f, 256)
    grid_spec = pltpu.PrefetchScalarGridSpec(
        num_scalar_prefetch=2,
        grid=(p // tm, f // tf),
        in_specs=[pl.BlockSpec((tm, d), lambda i, j, te, ac: (i, 0)),
                  pl.BlockSpec((None, d, tf), lambda i, j, te, ac: (te[i], 0, j)),
                  pl.BlockSpec((None, d, tf), lambda i, j, te, ac: (te[i], 0, j)),
                  pl.BlockSpec((None, tf, d), lambda i, j, te, ac: (te[i], j, 0))],
        out_specs=pl.BlockSpec((tm, d), lambda i, j, te, ac: (i, 0)),
        scratch_shapes=[pltpu.VMEM((tm, d), BF16), pltpu.VMEM((tm, d), F32)])
    return pl.pallas_call(
        _moe_expert_kernel,
        grid_spec=grid_spec,
        out_shape=jax.ShapeDtypeStruct((p, d), F32),
        compiler_params=_params("arbitrary", "arbitrary"),
    )(tile_expert, tile_active, xs, w1, w3, w2)


def _combine_ln_kernel(p1_ref, p2_ref, x_ref, gt_ref, g_ref, b_ref, ys_hbm, y_ref, yb_ref, buf, sem, *, tm):
    i = pl.program_id(0)
    n = pl.num_programs(0)

    def issue(tile, slot):
        base = tile * tm

        def body(t, carry):
            pltpu.make_async_copy(ys_hbm.at[pl.ds(p1_ref[base + t], 1)],
                                  buf.at[slot, pl.ds(t, 1)], sem.at[slot]).start()
            pltpu.make_async_copy(ys_hbm.at[pl.ds(p2_ref[base + t], 1)],
                                  buf.at[slot, pl.ds(tm + t, 1)], sem.at[slot]).start()
            return carry

        lax.fori_loop(0, tm, body, 0)

    @pl.when(i == 0)
    def _():
        issue(0, 0)

    @pl.when(i + 1 < n)
    def _():
        issue(i + 1, (i + 1) % 2)

    slot = i % 2
    pltpu.make_async_copy(ys_hbm.at[pl.ds(0, 2 * tm)], buf.at[slot], sem.at[slot]).wait()
    g1 = gt_ref[:, 0:1]
    g2 = gt_ref[:, 1:2]
    f = g1 * buf[slot, pl.ds(0, tm), :] + g2 * buf[slot, pl.ds(tm, tm), :]
    y = _layer_norm(DN_ALPHA * x_ref[...] + f, g_ref[...], b_ref[...])
    y_ref[...] = y
    yb_ref[...] = y.astype(BF16)


def _combine_ln(x, ys, p1, p2, gates, g, b):
    t, d = x.shape
    tm = _tile(t, COMBINE_TM)
    row = lambda i, a, c: (i, 0)
    const = lambda i, a, c: (0, 0)
    grid_spec = pltpu.PrefetchScalarGridSpec(
        num_scalar_prefetch=2,
        grid=(t // tm,),
        in_specs=[pl.BlockSpec((tm, d), row), pl.BlockSpec((tm, LANES), row),
                  pl.BlockSpec((1, d), const), pl.BlockSpec((1, d), const),
                  pl.BlockSpec(memory_space=pl.ANY)],
        out_specs=[pl.BlockSpec((tm, d), row), pl.BlockSpec((tm, d), row)],
        scratch_shapes=[pltpu.VMEM((2, 2 * tm, d), F32), pltpu.SemaphoreType.DMA((2,))])
    return pl.pallas_call(
        functools.partial(_combine_ln_kernel, tm=tm),
        grid_spec=grid_spec,
        out_shape=[jax.ShapeDtypeStruct((t, d), F32), jax.ShapeDtypeStruct((t, d), BF16)],
        compiler_params=_params("arbitrary"),
    )(p1, p2, x, gates, g, b, ys)


def _moe_routing(r, tm):
    t = r.shape[0]
    e = N_EXPERTS
    sel = r[:, :e].astype(jnp.int32)
    i1 = r[:, e].astype(jnp.int32)
    i2 = r[:, e + 1].astype(jnp.int32)
    cnt = jnp.cumsum(sel, axis=0)
    rank = cnt - sel
    padded = ((cnt[-1] + tm - 1) // tm) * tm
    ends = jnp.cumsum(padded)
    pos = (ends - padded)[None, :] + rank
    lane = jnp.arange(e, dtype=jnp.int32)[None, :]
    p1 = jnp.sum(jnp.where(lane == i1[:, None], pos, 0), axis=1).astype(jnp.int32)
    p2 = jnp.sum(jnp.where(lane == i2[:, None], pos, 0), axis=1).astype(jnp.int32)
    n_rows = TOP_K * t + e * tm
    tile_start = jnp.arange(n_rows // tm, dtype=jnp.int32) * tm
    tile_expert = jnp.sum((tile_start[:, None] >= ends[None, :]).astype(jnp.int32), axis=1)
    tile_expert = jnp.minimum(tile_expert, e - 1).astype(jnp.int32)
    tile_active = (tile_start < ends[-1]).astype(jnp.int32)
    return p1, p2, tile_expert, tile_active, n_rows


def _deinterleave_perm():
    half = HEAD_DIM // 2
    return np.concatenate([np.arange(half) * 2, np.arange(half) * 2 + 1])


def _rope_tables(groups):
    pos = np.concatenate([np.tile(np.arange(seq), batch) for batch, seq in groups])
    pos = jnp.asarray(pos, jnp.int32)
    row = (pos // GRID_W).astype(F32)
    col = (pos % GRID_W).astype(F32)
    axis_dim = HEAD_DIM // 2
    inv = ROPE_THETA ** (-jnp.arange(0, axis_dim, 2, dtype=F32) / axis_dim)
    ang = jnp.concatenate([row[:, None] * inv, col[:, None] * inv], axis=-1)
    cos, sin = jnp.cos(ang), jnp.sin(ang)
    return jnp.concatenate([cos, cos], axis=-1), jnp.concatenate([-sin, sin], axis=-1)


def _group_rows(groups):
    out, row0 = [], 0
    for batch, seq in groups:
        out.append((row0, batch, seq))
        row0 += batch * seq
    return out


def _even_layer(x, xb, groups, w_in, qk_gain, w_out, w1, w3, w2, ln_g, ln_b, rope, dil):
    w_a = N_HEADS_A * HEAD_DIM
    w_bq = N_HEADS_B * HEAD_DIM
    w_bkv = N_KV_B * HEAD_DIM
    perm = _deinterleave_perm()
    n_qk = N_HEADS_B + N_KV_B
    c0 = 3 * w_a
    pmat = np.zeros((HEAD_DIM, HEAD_DIM), np.float32)
    pmat[perm, np.arange(HEAD_DIM)] = 1.0
    w_in_b = w_in.astype(BF16)
    w_qk = w_in_b[:, c0:c0 + n_qk * HEAD_DIM].reshape(w_in.shape[0], n_qk, HEAD_DIM)
    w_qk = jnp.einsum('dhk,kn->dhn', w_qk, jnp.asarray(pmat, BF16), preferred_element_type=F32)
    w_in_b = jnp.concatenate([w_in_b[:, :c0], w_qk.astype(BF16).reshape(w_in.shape[0], n_qk * HEAD_DIM),
                              w_in_b[:, c0 + n_qk * HEAD_DIM:]], axis=1)
    col_scale = jnp.concatenate([jnp.full((w_a,), SCALE * LOG2E, F32),
                                 jnp.ones((w_in.shape[1] - w_a,), F32)])[None, :]
    proj = _project(xb, w_in_b, col_scale)

    n_b = N_HEADS_B + N_KV_B
    gains = jnp.concatenate([jnp.tile(qk_gain[0][perm][None], (N_HEADS_B, 1)),
                             jnp.tile(qk_gain[1][perm][None], (N_KV_B, 1))]).astype(F32)
    scales = jnp.concatenate([jnp.full((N_HEADS_B, HEAD_DIM), SCALE * LOG2E, F32),
                              jnp.ones((N_KV_B, HEAD_DIM), F32)])
    qk = _qk_prep(proj, gains.reshape(2, n_b // 2, HEAD_DIM), scales.reshape(2, n_b // 2, HEAD_DIM),
                  rope[0], rope[1], 3 * w_a)

    table, halo_blks, win_blks = dil
    oa, ob = [], []
    v_blk0 = (3 * w_a + w_bq + w_bkv) // HEAD_DIM
    for row0, batch, seq in _group_rows(groups):
        oa.append(_win_attn(proj, table, row0=row0, batch=batch, seq=seq, n_heads=N_HEADS_A,
                            q_blk0=0, k_blk0=N_HEADS_A, v_blk0=2 * N_HEADS_A,
                            halo_blks=halo_blks, win_blks=win_blks, hp=HEADS_PER_STEP_A))
        ob.append(_gqa(qk, proj, row0=row0, batch=batch, seq=seq, v_blk0=v_blk0))
    oa = jnp.concatenate(oa, axis=0)
    ob = jnp.concatenate(ob, axis=0)

    w_out_b = w_out.astype(BF16)
    x, xb = _out_ln(x, [oa, ob], [w_out_b[:w_a], w_out_b[w_a:]], ln_g[0][None], ln_b[0][None])
    return _ffn_ln(x, xb, w1.astype(BF16), w3.astype(BF16), w2.astype(BF16), ln_g[1][None], ln_b[1][None])


def _odd_layer(x, xb, groups, w_in, rpb, w_out, w_router, b_router, w1, w3, w2, ln_g, ln_b):
    w_c = N_HEADS_C * HEAD_DIM
    col_scale = jnp.concatenate([jnp.full((w_c,), SCALE * LOG2E, F32), jnp.ones((2 * w_c,), F32)])[None, :]
    proj = _project(xb, w_in.astype(BF16), col_scale)
    table, halo_blks, win_blks = _natten_table(rpb)
    o = []
    for row0, batch, seq in _group_rows(groups):
        o.append(_win_attn(proj, table, row0=row0, batch=batch, seq=seq, n_heads=N_HEADS_C,
                           q_blk0=0, k_blk0=N_HEADS_C, v_blk0=2 * N_HEADS_C,
                           halo_blks=halo_blks, win_blks=win_blks, hp=HEADS_PER_STEP_C))
    o = jnp.concatenate(o, axis=0)

    wr = jnp.pad(w_router.astype(F32), ((0, 0), (0, ROUTER_LANES - N_EXPERTS)))
    wr_hi = wr.astype(BF16)
    wr_lo = (wr - wr_hi.astype(F32)).astype(BF16)
    br = jnp.concatenate([b_router.astype(F32), jnp.full((ROUTER_LANES - N_EXPERTS,), NEG, F32)])[None, :]
    x, r = _out_ln(x, [o], [w_out.astype(BF16)], ln_g[0][None], ln_b[0][None], router=(wr_hi, wr_lo, br))

    p1, p2, tile_expert, tile_active, n_rows = _moe_routing(r, MOE_TM)
    xs = _dispatch(x, p1, p2, n_rows)
    ys = _moe_experts(xs, tile_expert, tile_active, w1.astype(BF16), w3.astype(BF16), w2.astype(BF16))
    gates = jnp.pad(r[:, N_EXPERTS + 2:N_EXPERTS + 4], ((0, 0), (0, LANES - 2)))
    return _combine_ln(x, ys, p1, p2, gates, ln_g[1][None], ln_b[1][None])


def kernel(x_prompt, x_sample, ln_g, ln_b, w_in_even, qk_gain_b, w_out_even, ffn_w1, ffn_w3, ffn_w2,
           w_in_odd, rpb, w_out_odd, w_router, b_router, moe_w1, moe_w3, moe_w2):
    d = x_prompt.shape[-1]
    groups = [(x_prompt.shape[0], x_prompt.shape[1]), (x_sample.shape[0], x_sample.shape[1])]
    x = jnp.concatenate([x_prompt.reshape(-1, d), x_sample.reshape(-1, d)], axis=0)
    xb = x.astype(BF16)
    rope = _rope_tables(groups)
    dil = _dilated_table()
    for i in range(ln_g.shape[0]):
        j = i // 2
        if i % 2 == 0:
            x, xb = _even_layer(x, xb, groups, w_in_even[j], qk_gain_b[j], w_out_even[j],
                                ffn_w1[j], ffn_w3[j], ffn_w2[j], ln_g[i], ln_b[i], rope, dil)
        else:
            x, xb = _odd_layer(x, xb, groups, w_in_odd[j], rpb[j], w_out_odd[j], w_router[j], b_router[j],
                               moe_w1[j], moe_w3[j], moe_w2[j], ln_g[i], ln_b[i])
    n_p = x_prompt.shape[0] * x_prompt.shape[1]
    return (x[:n_p].reshape(x_prompt.shape), x[n_p:].reshape(x_sample.shape))
```

```python
import functools
import math

import numpy as np
import jax
import jax.numpy as jnp
from jax import lax
from jax.experimental import pallas as pl
from jax.experimental.pallas import tpu as pltpu

HEAD_DIM = 128
GRID_W = 64
N_HEADS_A = 6
DILATED_BRANCHES = ((128, 1), (512, 4), (2048, 16))
N_HEADS_B = 10
N_KV_B = 2
N_HEADS_C = 16
NA_ROWS = 8
NA_COLS = 16
N_EXPERTS = 8
TOP_K = 2
DEPTH = 4
ROPE_THETA = 10000.0
LN_EPS = 1e-5
QK_EPS = 1e-6
NEG = -1e30
SCALE = HEAD_DIM ** -0.5
LOG2E = math.log2(math.e)
DN_ALPHA = (2 * DEPTH) ** 0.25

V7X_VMEM_BYTES = 64 * 2 ** 20
VMEM_LIMIT = V7X_VMEM_BYTES - 8 * 2 ** 20
LANES = 128
ATTN_TQ = 256
ATTN_ROWS = 128
HEADS_PER_STEP_A = 2
HEADS_PER_STEP_C = 4
ROUTER_LANES = LANES

F32 = jnp.float32
BF16 = jnp.bfloat16


def _params(*sem):
    return pltpu.CompilerParams(dimension_semantics=sem, vmem_limit_bytes=VMEM_LIMIT)


def _tile(n, pref):
    if n <= pref:
        return n
    t = (pref // LANES) * LANES
    while t >= LANES:
        if n % t == 0:
            return t
        t -= LANES
    return n


def _lane_tiles(s):
    return [s[:, t * LANES:(t + 1) * LANES] for t in range(s.shape[1] // LANES)]


def _with_ones(v):
    return jnp.concatenate([v, jnp.ones_like(v)], axis=1)


def _proj_kernel(x_ref, w_ref, cs_ref, o_ref):
    acc = jnp.dot(x_ref[...], w_ref[...], preferred_element_type=F32)
    o_ref[...] = (acc * cs_ref[...]).astype(o_ref.dtype)


def _project(xb, w_stack, layer, col_scale):
    t, k = xb.shape
    n = w_stack.shape[2]
    tm, tn = _tile(t, 1024), _tile(n, 1024)
    return pl.pallas_call(
        _proj_kernel,
        grid=(t // tm, n // tn),
        in_specs=[pl.BlockSpec((tm, k), lambda i, j: (i, 0)),
                  pl.BlockSpec((None, k, tn), lambda i, j: (layer, 0, j)),
                  pl.BlockSpec((1, tn), lambda i, j: (0, j))],
        out_specs=pl.BlockSpec((tm, tn), lambda i, j: (i, j)),
        out_shape=jax.ShapeDtypeStruct((t, n), BF16),
        compiler_params=_params("arbitrary", "arbitrary"),
    )(xb, w_stack, col_scale)


def _layer_norm(z, g, b):
    mu = jnp.mean(z, axis=-1, keepdims=True)
    zc = z - mu
    var = jnp.mean(zc * zc, axis=-1, keepdims=True)
    return zc * lax.rsqrt(var + LN_EPS) * g + b


def _qk_prep_kernel(p_ref, g_ref, sc_ref, cos_ref, sin_ref, o_ref, *, heads):
    c = cos_ref[...]
    s = sin_ref[...]
    for r in range(heads):
        x = p_ref[:, r * HEAD_DIM:(r + 1) * HEAD_DIM].astype(F32)
        ms = jnp.mean(x * x, axis=-1, keepdims=True)
        xn = x * lax.rsqrt(ms + QK_EPS) * g_ref[0, r:r + 1, :]
        y = xn * c + pltpu.roll(xn, HEAD_DIM // 2, 1) * s
        o_ref[:, r * HEAD_DIM:(r + 1) * HEAD_DIM] = (y * sc_ref[0, r:r + 1, :]).astype(o_ref.dtype)


def _qk_prep(proj, gains, scales, cos_t, sin_t, col0):
    t = proj.shape[0]
    n_heads = N_HEADS_B + N_KV_B
    half = n_heads // 2
    wblk = half * HEAD_DIM
    assert col0 % wblk == 0
    tm = _tile(t, 512)
    return pl.pallas_call(
        functools.partial(_qk_prep_kernel, heads=half),
        grid=(t // tm, 2),
        in_specs=[pl.BlockSpec((tm, wblk), lambda i, j: (i, col0 // wblk + j)),
                  pl.BlockSpec((1, half, HEAD_DIM), lambda i, j: (j, 0, 0)),
                  pl.BlockSpec((1, half, HEAD_DIM), lambda i, j: (j, 0, 0)),
                  pl.BlockSpec((tm, HEAD_DIM), lambda i, j: (i, 0)),
                  pl.BlockSpec((tm, HEAD_DIM), lambda i, j: (i, 0))],
        out_specs=pl.BlockSpec((tm, wblk), lambda i, j: (i, j)),
        out_shape=jax.ShapeDtypeStruct((t, n_heads * HEAD_DIM), BF16),
        compiler_params=_params("arbitrary", "arbitrary"),
    )(proj, gains, scales, cos_t, sin_t)


def _window_start_blk(i, halo_blks, win_blks, n_blks):
    return jnp.clip(i - halo_blks, 0, n_blks - win_blks)


def _win_attn_kernel(q_ref, k_ref, v_ref, tb_ref, o_ref, *, halo_blks, win_blks, n_blks, axis):
    i = pl.program_id(axis)
    tq = q_ref.shape[0]
    w = win_blks * tq
    start = pl.multiple_of(_window_start_blk(i, halo_blks, win_blks, n_blks) * tq, tq)
    hp = tb_ref.shape[0]
    for hd in range(hp):
        cols = slice(hd * HEAD_DIM, (hd + 1) * HEAD_DIM)
        k = k_ref[pl.ds(start, w), cols]
        v2 = _with_ones(v_ref[pl.ds(start, w), cols])
        for rb in range(tq // ATTN_ROWS):
            sl = slice(rb * ATTN_ROWS, (rb + 1) * ATTN_ROWS)
            s = lax.dot_general(q_ref[sl, cols], k, (((1,), (1,)), ((), ())), preferred_element_type=F32)
            tiles = _lane_tiles(s + tb_ref[hd, 0, sl, :])
            m = jnp.max(functools.reduce(jnp.maximum, tiles), axis=-1, keepdims=True)
            p = jnp.concatenate([jnp.exp2(t - m) for t in tiles], axis=1).astype(BF16)
            pv = jnp.dot(p, v2, preferred_element_type=F32)
            o_ref[sl, cols] = (pv[:, :HEAD_DIM] / pv[:, HEAD_DIM:]).astype(o_ref.dtype)


def _win_attn(src, table, *, row0, batch, seq, n_heads, q_blk0, k_blk0, v_blk0, halo_blks, win_blks, hp):
    tq = ATTN_TQ
    n_blks = seq // tq
    assert seq % tq == 0 and n_blks >= win_blks and row0 % seq == 0
    assert n_heads % hp == 0 and q_blk0 % hp == 0 and k_blk0 % hp == 0 and v_blk0 % hp == 0
    seq_blk0 = row0 // seq
    qrow0 = row0 // tq
    wh = hp * HEAD_DIM
    qb, kb, vb = q_blk0 // hp, k_blk0 // hp, v_blk0 // hp

    def variant(i):
        return i - _window_start_blk(i, halo_blks, win_blks, n_blks)

    kern = functools.partial(_win_attn_kernel, halo_blks=halo_blks, win_blks=win_blks,
                             n_blks=n_blks, axis=2)
    return pl.pallas_call(
        kern,
        grid=(batch, n_heads // hp, n_blks),
        in_specs=[pl.BlockSpec((tq, wh), lambda b, h, i: (qrow0 + b * n_blks + i, qb + h)),
                  pl.BlockSpec((seq, wh), lambda b, h, i: (seq_blk0 + b, kb + h), pipeline_mode=pl.Buffered(1)),
                  pl.BlockSpec((seq, wh), lambda b, h, i: (seq_blk0 + b, vb + h), pipeline_mode=pl.Buffered(1)),
                  pl.BlockSpec((hp, 1, tq, win_blks * tq), lambda b, h, i: (h, variant(i), 0, 0))],
        out_specs=pl.BlockSpec((tq, wh), lambda b, h, i: (b * n_blks + i, h)),
        out_shape=jax.ShapeDtypeStruct((batch * seq, n_heads * HEAD_DIM), BF16),
        compiler_params=_params("arbitrary", "arbitrary", "arbitrary"),
    )(src, src, src, table)


def _dilated_table():
    tq = ATTN_TQ
    halo = max(w // 2 for w, _ in DILATED_BRANCHES)
    halo_blks = halo // tq
    win_blks = 2 * halo_blks + 1
    v = jnp.arange(win_blks)[:, None, None]
    r = jnp.arange(tq)[None, :, None]
    c = jnp.arange(win_blks * tq)[None, None, :]
    delta = c - v * tq - r
    ad = jnp.abs(delta)
    cnt = jnp.zeros(delta.shape, F32)
    for window, dil in DILATED_BRANCHES:
        cnt = cnt + ((ad <= window // 2) & (delta % dil == 0)).astype(F32)
    slopes = 2.0 ** (-8.0 * jnp.arange(1, N_HEADS_A + 1, dtype=F32) / N_HEADS_A)
    bias = -slopes[:, None, None, None] * ad.astype(F32)[None] + jnp.log(jnp.maximum(cnt, 1.0))[None]
    table = jnp.where(cnt[None] > 0, bias * LOG2E, NEG)
    return table, halo_blks, win_blks


def _natten_table(rpb):
    tq = ATTN_TQ
    rows_per_tile = tq // GRID_W
    halo_blks = 1
    win_blks = 3
    assert NA_ROWS // 2 == rows_per_tile
    win_rows = win_blks * rows_per_tile
    qt = np.arange(win_blks)[:, None] * tq + np.arange(tq)[None, :]
    rq, qc = qt // GRID_W, qt % GRID_W
    kt = np.arange(win_blks * tq)
    rk, kc = kt // GRID_W, kt % GRID_W
    rs = np.clip(rq - NA_ROWS // 2, 0, win_rows - NA_ROWS)
    cs = np.clip(qc - NA_COLS // 2, 0, GRID_W - NA_COLS)
    valid = ((rk[None, None, :] >= rs[..., None]) & (rk[None, None, :] < rs[..., None] + NA_ROWS)
             & (kc[None, None, :] >= cs[..., None]) & (kc[None, None, :] < cs[..., None] + NA_COLS))
    n_ro, n_co = 2 * NA_ROWS - 1, 2 * NA_COLS - 1
    rows = np.arange(win_rows)
    ri = np.clip(rows[None, :] - rows[:, None] + NA_ROWS - 1, 0, n_ro - 1)
    cols = np.arange(GRID_W)
    ci = np.clip(cols[None, :] - cols[:, None] + NA_COLS - 1, 0, n_co - 1)
    onehot = (np.arange(n_co)[:, None, None] == ci[None]).astype(np.float32).reshape(n_co, GRID_W * GRID_W)
    t1 = rpb.astype(F32)[:, ri.reshape(-1), :]
    hi = t1.astype(BF16)
    mid = (t1 - hi.astype(F32)).astype(BF16)
    lo = (t1 - hi.astype(F32) - mid.astype(F32)).astype(BF16)
    oh = jnp.asarray(onehot, BF16)
    t2 = sum(jnp.einsum('hpb,bq->hpq', part, oh, preferred_element_type=F32) for part in (hi, mid, lo))
    t2 = t2.reshape(-1, win_rows, win_rows, GRID_W, GRID_W).transpose(0, 1, 3, 2, 4)
    bias = t2.reshape(-1, win_blks, tq, win_blks * tq)
    table = jnp.where(jnp.asarray(valid)[None], bias * LOG2E, NEG)
    return table, halo_blks, win_blks


GQA_TQ = 512
GQA_TK = 2048
GQA_ROWS = 256


def _gqa_kernel(q_ref, k_ref, v_ref, o_ref, qs_ref, acc_ref, m_ref, *, rep, tk, rows):
    tq = q_ref.shape[0]
    seq = k_ref.shape[0]
    for r in range(rep):
        qs_ref[r * tq:(r + 1) * tq, :] = q_ref[:, r * HEAD_DIM:(r + 1) * HEAD_DIM]
    m_ref[...] = jnp.full(m_ref.shape, -jnp.inf, F32)
    acc_ref[...] = jnp.zeros(acc_ref.shape, F32)

    def body(c, carry):
        off = pl.multiple_of(c * tk, tk)
        k = k_ref[pl.ds(off, tk), :]
        v2 = _with_ones(v_ref[pl.ds(off, tk), :])
        for rb in range(rep * tq // rows):
            sl = slice(rb * rows, (rb + 1) * rows)
            s = lax.dot_general(qs_ref[sl, :], k, (((1,), (1,)), ((), ())), preferred_element_type=F32)
            tiles = _lane_tiles(s)
            m_prev = m_ref[sl, :]
            m_new = jnp.maximum(m_prev, jnp.max(functools.reduce(jnp.maximum, tiles), axis=-1, keepdims=True))
            alpha = jnp.exp2(m_prev - m_new)
            p = jnp.concatenate([jnp.exp2(t - m_new) for t in tiles], axis=1).astype(BF16)
            pv = jnp.dot(p, v2, preferred_element_type=F32)
            acc_ref[sl, :] = jnp.concatenate([alpha, alpha], axis=1) * acc_ref[sl, :] + pv
            m_ref[sl, :] = m_new
        return carry

    lax.fori_loop(0, seq // tk, body, 0)
    acc = acc_ref[...]
    o = acc[:, :HEAD_DIM] / acc[:, HEAD_DIM:]
    for r in range(rep):
        o_ref[:, r * HEAD_DIM:(r + 1) * HEAD_DIM] = o[r * tq:(r + 1) * tq].astype(o_ref.dtype)


def _gqa(qk, proj, *, row0, batch, seq, v_blk0):
    rep = N_HEADS_B // N_KV_B
    tq = GQA_TQ
    tk = _tile(seq, GQA_TK)
    n_blks = seq // tq
    assert seq % tq == 0 and row0 % seq == 0 and (rep * tq) % GQA_ROWS == 0
    seq_blk0 = row0 // seq
    qrow0 = row0 // tq
    wq = rep * HEAD_DIM
    return pl.pallas_call(
        functools.partial(_gqa_kernel, rep=rep, tk=tk, rows=GQA_ROWS),
        grid=(batch, N_KV_B, n_blks),
        in_specs=[pl.BlockSpec((tq, wq), lambda b, g, i: (qrow0 + b * n_blks + i, g)),
                  pl.BlockSpec((seq, HEAD_DIM), lambda b, g, i: (seq_blk0 + b, N_HEADS_B + g)),
                  pl.BlockSpec((seq, HEAD_DIM), lambda b, g, i: (seq_blk0 + b, v_blk0 + g))],
        out_specs=pl.BlockSpec((tq, wq), lambda b, g, i: (b * n_blks + i, g)),
        out_shape=jax.ShapeDtypeStruct((batch * seq, N_HEADS_B * HEAD_DIM), BF16),
        scratch_shapes=[pltpu.VMEM((rep * tq, HEAD_DIM), BF16),
                        pltpu.VMEM((rep * tq, 2 * HEAD_DIM), F32),
                        pltpu.VMEM((rep * tq, HEAD_DIM), F32)],
        compiler_params=_params("arbitrary", "arbitrary", "arbitrary"),
    )(qk, qk, proj)


def _out_ln_kernel(x_ref, *refs, widths, bounds, router):
    i = pl.program_id(0)
    n_groups = len(bounds) + 1
    n_act = len(widths)
    w_ref, g_ref, b_ref = refs[n_act * n_groups:n_act * n_groups + 3]
    rest = refs[n_act * n_groups + 3:]
    h = None
    row0 = 0
    for a, width in enumerate(widths):
        grp = refs[a * n_groups:(a + 1) * n_groups]
        val = grp[-1][...]
        for gi in reversed(range(n_groups - 1)):
            val = jnp.where(i < bounds[gi], grp[gi][...], val)
        part = jnp.dot(val, w_ref[row0:row0 + width, :], preferred_element_type=F32)
        h = part if h is None else h + part
        row0 += width
    y = _layer_norm(DN_ALPHA * x_ref[...] + h, g_ref[...], b_ref[...])
    if not router:
        y_ref, yb_ref = rest
        y_ref[...] = y
        yb_ref[...] = y.astype(BF16)
        return
    wr_hi_ref, wr_lo_ref, br_ref, y_ref, r_ref = rest
    y_hi = y.astype(BF16)
    y_lo = (y - y_hi.astype(F32)).astype(BF16)
    y_ref[...] = y
    logits = (jnp.dot(y_hi, wr_hi_ref[...], preferred_element_type=F32)
              + jnp.dot(y_lo, wr_hi_ref[...], preferred_element_type=F32)
              + jnp.dot(y_hi, wr_lo_ref[...], preferred_element_type=F32)) + br_ref[...]
    lane = lax.broadcasted_iota(jnp.int32, logits.shape, 1).astype(F32)
    m1 = jnp.max(logits, axis=-1, keepdims=True)
    i1 = jnp.min(jnp.where(logits == m1, lane, float(ROUTER_LANES)), axis=-1, keepdims=True)
    rest_l = jnp.where(lane == i1, -jnp.inf, logits)
    m2 = jnp.max(rest_l, axis=-1, keepdims=True)
    i2 = jnp.min(jnp.where(rest_l == m2, lane, float(ROUTER_LANES)), axis=-1, keepdims=True)
    e2 = jnp.exp(m2 - m1)
    den = 1.0 + e2
    g1 = 1.0 / den
    g2 = e2 / den
    sel = ((lane == i1) | (lane == i2)).astype(F32)
    e = N_EXPERTS
    out = jnp.where(lane < e, sel, 0.0)
    out = jnp.where(lane == e, i1, out)
    out = jnp.where(lane == e + 1, i2, out)
    out = jnp.where(lane == e + 2, g1, out)
    out = jnp.where(lane == e + 3, g2, out)
    r_ref[...] = out


def _out_ln(x, acts, w_stack, layer, g, b, router=None):
    t, d = x.shape
    group_rows = [a.shape[0] for a in acts[0]]
    tm = _tile(math.gcd(*group_rows), 512)
    starts = np.cumsum([0] + [r // tm for r in group_rows])
    row = lambda i: (i, 0)
    const = lambda i: (0, 0)
    in_specs = [pl.BlockSpec((tm, d), row)]
    args = [x]
    for per_group in acts:
        for gi, a in enumerate(per_group):
            lo, n = int(starts[gi]), int(starts[gi + 1] - starts[gi])
            in_specs.append(pl.BlockSpec((tm, a.shape[1]),
                                         lambda i, lo=lo, n=n: (jnp.clip(i - lo, 0, n - 1), 0)))
            args.append(a)
    in_specs += [pl.BlockSpec((None,) + w_stack.shape[1:], lambda i: (layer, 0, 0), pipeline_mode=pl.Buffered(1)),
                 pl.BlockSpec((1, d), const), pl.BlockSpec((1, d), const)]
    args += [w_stack, g, b]
    out_specs = [pl.BlockSpec((tm, d), row)]
    out_shape = [jax.ShapeDtypeStruct((t, d), F32)]
    widths = tuple(per_group[0].shape[1] for per_group in acts)
    bounds = tuple(int(s) for s in starts[1:-1])
    if router is None:
        out_specs.append(pl.BlockSpec((tm, d), row))
        out_shape.append(jax.ShapeDtypeStruct((t, d), BF16))
    else:
        in_specs += [pl.BlockSpec((d, ROUTER_LANES), const), pl.BlockSpec((d, ROUTER_LANES), const),
                     pl.BlockSpec((1, ROUTER_LANES), const)]
        out_specs.append(pl.BlockSpec((tm, ROUTER_LANES), row))
        out_shape.append(jax.ShapeDtypeStruct((t, ROUTER_LANES), F32))
        args += list(router)
    return pl.pallas_call(
        functools.partial(_out_ln_kernel, widths=widths, bounds=bounds, router=router is not None),
        grid=(t // tm,),
        in_specs=in_specs, out_specs=out_specs, out_shape=out_shape,
        compiler_params=_params("arbitrary"),
    )(*args)


def _swiglu_accumulate(acc_ref, xb, w1, w3, w2):
    h1 = jnp.dot(xb, w1, preferred_element_type=F32)
    h3 = jnp.dot(xb, w3, preferred_element_type=F32)
    h = (h1 * (1.0 / (1.0 + jnp.exp(-h1)))) * h3
    acc_ref[...] += jnp.dot(h.astype(BF16), w2, preferred_element_type=F32)


def _ffn_ln_kernel(x_ref, xb_ref, w1_ref, w3_ref, w2_ref, g_ref, b_ref, y_ref, yb_ref, acc_ref):
    j = pl.program_id(1)

    @pl.when(j == 0)
    def _():
        acc_ref[...] = jnp.zeros(acc_ref.shape, F32)

    _swiglu_accumulate(acc_ref, xb_ref[...], w1_ref[...], w3_ref[...], w2_ref[...])

    @pl.when(j == pl.num_programs(1) - 1)
    def _():
        y = _layer_norm(DN_ALPHA * x_ref[...] + acc_ref[...], g_ref[...], b_ref[...])
        y_ref[...] = y
        yb_ref[...] = y.astype(BF16)


def _ffn_ln(x, xb, w1, w3, w2, layer, g, b):
    t, d = x.shape
    f = w1.shape[2]
    tm, tf = _tile(t, 512), _tile(f, 512)
    return pl.pallas_call(
        _ffn_ln_kernel,
        grid=(t // tm, f // tf),
        in_specs=[pl.BlockSpec((tm, d), lambda i, j: (i, 0)),
                  pl.BlockSpec((tm, d), lambda i, j: (i, 0)),
                  pl.BlockSpec((None, d, tf), lambda i, j: (layer, 0, j)),
                  pl.BlockSpec((None, d, tf), lambda i, j: (layer, 0, j)),
                  pl.BlockSpec((None, tf, d), lambda i, j: (layer, j, 0)),
                  pl.BlockSpec((1, d), lambda i, j: (0, 0)),
                  pl.BlockSpec((1, d), lambda i, j: (0, 0))],
        out_specs=[pl.BlockSpec((tm, d), lambda i, j: (i, 0)),
                   pl.BlockSpec((tm, d), lambda i, j: (i, 0))],
        out_shape=[jax.ShapeDtypeStruct((t, d), F32), jax.ShapeDtypeStruct((t, d), BF16)],
        scratch_shapes=[pltpu.VMEM((tm, d), F32)],
        compiler_params=_params("arbitrary", "arbitrary"),
    )(x, xb, w1, w3, w2, g, b)


MOE_TM = 1024
DISPATCH_TM = 512
COMBINE_TM = 256


def _dispatch_kernel(p1_ref, p2_ref, x_ref, xs_init_hbm, xs_hbm, sem, *, tm):
    del xs_init_hbm
    base = pl.program_id(0) * tm

    def issue(t, carry):
        src = x_ref.at[pl.ds(t, 1)]
        pltpu.make_async_copy(src, xs_hbm.at[pl.ds(p1_ref[base + t], 1)], sem).start()
        pltpu.make_async_copy(src, xs_hbm.at[pl.ds(p2_ref[base + t], 1)], sem).start()
        return carry

    lax.fori_loop(0, tm, issue, 0)
    for _ in range(TOP_K):
        pltpu.make_async_copy(x_ref, xs_hbm.at[pl.ds(0, tm)], sem).wait()


def _dispatch(x, p1, p2, n_rows):
    t, d = x.shape
    tm = _tile(t, DISPATCH_TM)
    grid_spec = pltpu.PrefetchScalarGridSpec(
        num_scalar_prefetch=2,
        grid=(t // tm,),
        in_specs=[pl.BlockSpec((tm, d), lambda i, a, c: (i, 0)), pl.BlockSpec(memory_space=pl.ANY)],
        out_specs=pl.BlockSpec(memory_space=pl.ANY),
        scratch_shapes=[pltpu.SemaphoreType.DMA(())])
    return pl.pallas_call(
        functools.partial(_dispatch_kernel, tm=tm),
        grid_spec=grid_spec,
        out_shape=jax.ShapeDtypeStruct((n_rows, d), x.dtype),
        input_output_aliases={3: 0},
        compiler_params=_params("arbitrary"),
    )(p1, p2, x, jnp.zeros((n_rows, d), x.dtype))


def _moe_expert_kernel(te_ref, act_ref, xs_ref, w1_ref, w3_ref, w2_ref, y_ref, xb_ref):
    i = pl.program_id(0)
    j = pl.program_id(1)
    active = act_ref[i] > 0

    @pl.when(j == 0)
    def _():
        y_ref[...] = jnp.zeros(y_ref.shape, F32)

    @pl.when(active)
    def _():
        @pl.when(j == 0)
        def _():
            xb_ref[...] = xs_ref[...].astype(BF16)

        _swiglu_accumulate(y_ref, xb_ref[...], w1_ref[...], w3_ref[...], w2_ref[...])


def _moe_experts(xs, tile_expert, tile_active, w1, w3, w2, layer):
    p, d = xs.shape
    f = w1.shape[3]
    tm, tf = MOE_TM, _tile(f, 256)
    grid_spec = pltpu.PrefetchScalarGridSpec(
        num_scalar_prefetch=2,
        grid=(p // tm, f // tf),
        in_specs=[pl.BlockSpec((tm, d), lambda i, j, te, ac: (i, 0)),
                  pl.BlockSpec((None, None, d, tf), lambda i, j, te, ac: (layer, te[i], 0, j)),
                  pl.BlockSpec((None, None, d, tf), lambda i, j, te, ac: (layer, te[i], 0, j)),
                  pl.BlockSpec((None, None, tf, d), lambda i, j, te, ac: (layer, te[i], j, 0))],
        out_specs=pl.BlockSpec((tm, d), lambda i, j, te, ac: (i, 0)),
        scratch_shapes=[pltpu.VMEM((tm, d), BF16)])
    return pl.pallas_call(
        _moe_expert_kernel,
        grid_spec=grid_spec,
        out_shape=jax.ShapeDtypeStruct((p, d), F32),
        compiler_params=_params("arbitrary", "arbitrary"),
    )(tile_expert, tile_active, xs, w1, w3, w2)


def _combine_ln_kernel(p1_ref, p2_ref, x_ref, gt_ref, g_ref, b_ref, ys_hbm, y_ref, yb_ref, buf, sem, *, tm):
    i = pl.program_id(0)
    n = pl.num_programs(0)

    def issue(tile, slot):
        base = tile * tm

        def body(t, carry):
            pltpu.make_async_copy(ys_hbm.at[pl.ds(p1_ref[base + t], 1)],
                                  buf.at[slot, pl.ds(t, 1)], sem.at[slot]).start()
            pltpu.make_async_copy(ys_hbm.at[pl.ds(p2_ref[base + t], 1)],
                                  buf.at[slot, pl.ds(tm + t, 1)], sem.at[slot]).start()
            return carry

        lax.fori_loop(0, tm, body, 0)

    @pl.when(i == 0)
    def _():
        issue(0, 0)

    @pl.when(i + 1 < n)
    def _():
        issue(i + 1, (i + 1) % 2)

    slot = i % 2
    pltpu.make_async_copy(ys_hbm.at[pl.ds(0, 2 * tm)], buf.at[slot], sem.at[slot]).wait()
    g1 = gt_ref[:, 0:1]
    g2 = gt_ref[:, 1:2]
    f = g1 * buf[slot, pl.ds(0, tm), :] + g2 * buf[slot, pl.ds(tm, tm), :]
    y = _layer_norm(DN_ALPHA * x_ref[...] + f, g_ref[...], b_ref[...])
    y_ref[...] = y
    yb_ref[...] = y.astype(BF16)


def _combine_ln(x, ys, p1, p2, gates, g, b):
    t, d = x.shape
    tm = _tile(t, COMBINE_TM)
    row = lambda i, a, c: (i, 0)
    const = lambda i, a, c: (0, 0)
    grid_spec = pltpu.PrefetchScalarGridSpec(
        num_scalar_prefetch=2,
        grid=(t // tm,),
        in_specs=[pl.BlockSpec((tm, d), row), pl.BlockSpec((tm, LANES), row),
                  pl.BlockSpec((1, d), const), pl.BlockSpec((1, d), const),
                  pl.BlockSpec(memory_space=pl.ANY)],
        out_specs=[pl.BlockSpec((tm, d), row), pl.BlockSpec((tm, d), row)],
        scratch_shapes=[pltpu.VMEM((2, 2 * tm, d), F32), pltpu.SemaphoreType.DMA((2,))])
    return pl.pallas_call(
        functools.partial(_combine_ln_kernel, tm=tm),
        grid_spec=grid_spec,
        out_shape=[jax.ShapeDtypeStruct((t, d), F32), jax.ShapeDtypeStruct((t, d), BF16)],
        compiler_params=_params("arbitrary"),
    )(p1, p2, x, gates, g, b, ys)


def _moe_routing(r, tm):
    t = r.shape[0]
    e = N_EXPERTS
    sel = r[:, :e].astype(jnp.int32)
    i1 = r[:, e].astype(jnp.int32)
    i2 = r[:, e + 1].astype(jnp.int32)
    cnt = jnp.cumsum(sel, axis=0)
    rank = cnt - sel
    padded = ((cnt[-1] + tm - 1) // tm) * tm
    ends = jnp.cumsum(padded)
    pos = (ends - padded)[None, :] + rank
    lane = jnp.arange(e, dtype=jnp.int32)[None, :]
    p1 = jnp.sum(jnp.where(lane == i1[:, None], pos, 0), axis=1).astype(jnp.int32)
    p2 = jnp.sum(jnp.where(lane == i2[:, None], pos, 0), axis=1).astype(jnp.int32)
    n_rows = TOP_K * t + e * tm
    tile_start = jnp.arange(n_rows // tm, dtype=jnp.int32) * tm
    tile_expert = jnp.sum((tile_start[:, None] >= ends[None, :]).astype(jnp.int32), axis=1)
    tile_expert = jnp.minimum(tile_expert, e - 1).astype(jnp.int32)
    tile_active = (tile_start < ends[-1]).astype(jnp.int32)
    return p1, p2, tile_expert, tile_active, n_rows


def _deinterleave_perm():
    half = HEAD_DIM // 2
    return np.concatenate([np.arange(half) * 2, np.arange(half) * 2 + 1])


def _rope_tables(groups):
    pos = np.concatenate([np.tile(np.arange(seq), batch) for batch, seq in groups])
    pos = jnp.asarray(pos, jnp.int32)
    row = (pos // GRID_W).astype(F32)
    col = (pos % GRID_W).astype(F32)
    axis_dim = HEAD_DIM // 2
    inv = ROPE_THETA ** (-jnp.arange(0, axis_dim, 2, dtype=F32) / axis_dim)
    ang = jnp.concatenate([row[:, None] * inv, col[:, None] * inv], axis=-1)
    cos, sin = jnp.cos(ang), jnp.sin(ang)
    return jnp.concatenate([cos, cos], axis=-1), jnp.concatenate([-sin, sin], axis=-1)


def _group_rows(groups):
    out, row0 = [], 0
    for batch, seq in groups:
        out.append((row0, batch, seq))
        row0 += batch * seq
    return out


def _prep_w_in_even(w_in):
    perm = _deinterleave_perm()
    n_qk = N_HEADS_B + N_KV_B
    c0 = 3 * N_HEADS_A * HEAD_DIM
    pmat = np.zeros((HEAD_DIM, HEAD_DIM), np.float32)
    pmat[perm, np.arange(HEAD_DIM)] = 1.0
    w_in_b = w_in.astype(BF16)
    lead = w_in.shape[:2]
    w_qk = w_in_b[:, :, c0:c0 + n_qk * HEAD_DIM].reshape(*lead, n_qk, HEAD_DIM)
    w_qk = jnp.einsum('ldhk,kn->ldhn', w_qk, jnp.asarray(pmat, BF16), preferred_element_type=F32)
    return jnp.concatenate([w_in_b[:, :, :c0], w_qk.astype(BF16).reshape(*lead, n_qk * HEAD_DIM),
                            w_in_b[:, :, c0 + n_qk * HEAD_DIM:]], axis=2)


def _even_layer(x, xb, groups, layer, w_in_b, qk_gain, w_out_b, w1_b, w3_b, w2_b, ln_g, ln_b, rope, dil):
    w_a = N_HEADS_A * HEAD_DIM
    w_bq = N_HEADS_B * HEAD_DIM
    w_bkv = N_KV_B * HEAD_DIM
    perm = _deinterleave_perm()
    col_scale = jnp.concatenate([jnp.full((w_a,), SCALE * LOG2E, F32),
                                 jnp.ones((w_in_b.shape[2] - w_a,), F32)])[None, :]
    proj = _project(xb, w_in_b, layer, col_scale)

    n_b = N_HEADS_B + N_KV_B
    gains = jnp.concatenate([jnp.tile(qk_gain[0][perm][None], (N_HEADS_B, 1)),
                             jnp.tile(qk_gain[1][perm][None], (N_KV_B, 1))]).astype(F32)
    scales = jnp.concatenate([jnp.full((N_HEADS_B, HEAD_DIM), SCALE * LOG2E, F32),
                              jnp.ones((N_KV_B, HEAD_DIM), F32)])
    qk = _qk_prep(proj, gains.reshape(2, n_b // 2, HEAD_DIM), scales.reshape(2, n_b // 2, HEAD_DIM),
                  rope[0], rope[1], 3 * w_a)

    table, halo_blks, win_blks = dil
    oa, ob = [], []
    v_blk0 = (3 * w_a + w_bq + w_bkv) // HEAD_DIM
    for row0, batch, seq in _group_rows(groups):
        oa.append(_win_attn(proj, table, row0=row0, batch=batch, seq=seq, n_heads=N_HEADS_A,
                            q_blk0=0, k_blk0=N_HEADS_A, v_blk0=2 * N_HEADS_A,
                            halo_blks=halo_blks, win_blks=win_blks, hp=HEADS_PER_STEP_A))
        ob.append(_gqa(qk, proj, row0=row0, batch=batch, seq=seq, v_blk0=v_blk0))
    x, xb = _out_ln(x, [oa, ob], w_out_b, layer, ln_g[0][None], ln_b[0][None])
    return _ffn_ln(x, xb, w1_b, w3_b, w2_b, layer, ln_g[1][None], ln_b[1][None])


def _odd_layer(x, xb, groups, layer, w_in_b, rpb, w_out_b, w_router, b_router, w1_b, w3_b, w2_b, ln_g, ln_b):
    w_c = N_HEADS_C * HEAD_DIM
    col_scale = jnp.concatenate([jnp.full((w_c,), SCALE * LOG2E, F32), jnp.ones((2 * w_c,), F32)])[None, :]
    proj = _project(xb, w_in_b, layer, col_scale)
    table, halo_blks, win_blks = _natten_table(rpb)
    o = []
    for row0, batch, seq in _group_rows(groups):
        o.append(_win_attn(proj, table, row0=row0, batch=batch, seq=seq, n_heads=N_HEADS_C,
                           q_blk0=0, k_blk0=N_HEADS_C, v_blk0=2 * N_HEADS_C,
                           halo_blks=halo_blks, win_blks=win_blks, hp=HEADS_PER_STEP_C))

    wr = jnp.pad(w_router.astype(F32), ((0, 0), (0, ROUTER_LANES - N_EXPERTS)))
    wr_hi = wr.astype(BF16)
    wr_lo = (wr - wr_hi.astype(F32)).astype(BF16)
    br = jnp.concatenate([b_router.astype(F32), jnp.full((ROUTER_LANES - N_EXPERTS,), NEG, F32)])[None, :]
    x, r = _out_ln(x, [o], w_out_b, layer, ln_g[0][None], ln_b[0][None], router=(wr_hi, wr_lo, br))

    p1, p2, tile_expert, tile_active, n_rows = _moe_routing(r, MOE_TM)
    xs = _dispatch(x, p1, p2, n_rows)
    ys = _moe_experts(xs, tile_expert, tile_active, w1_b, w3_b, w2_b, layer)
    gates = jnp.pad(r[:, N_EXPERTS + 2:N_EXPERTS + 4], ((0, 0), (0, LANES - 2)))
    return _combine_ln(x, ys, p1, p2, gates, ln_g[1][None], ln_b[1][None])


def kernel(x_prompt, x_sample, ln_g, ln_b, w_in_even, qk_gain_b, w_out_even, ffn_w1, ffn_w3, ffn_w2,
           w_in_odd, rpb, w_out_odd, w_router, b_router, moe_w1, moe_w3, moe_w2):
    d = x_prompt.shape[-1]
    groups = [(x_prompt.shape[0], x_prompt.shape[1]), (x_sample.shape[0], x_sample.shape[1])]
    x = jnp.concatenate([x_prompt.reshape(-1, d), x_sample.reshape(-1, d)], axis=0)
    xb = x.astype(BF16)
    rope = _rope_tables(groups)
    dil = _dilated_table()
    even_w = (_prep_w_in_even(w_in_even), w_out_even.astype(BF16),
              ffn_w1.astype(BF16), ffn_w3.astype(BF16), ffn_w2.astype(BF16))
    odd_w = (w_in_odd.astype(BF16), w_out_odd.astype(BF16),
             moe_w1.astype(BF16), moe_w3.astype(BF16), moe_w2.astype(BF16))
    for i in range(ln_g.shape[0]):
        j = i // 2
        if i % 2 == 0:
            x, xb = _even_layer(x, xb, groups, j, even_w[0], qk_gain_b[j], even_w[1], *even_w[2:],
                                ln_g[i], ln_b[i], rope, dil)
        else:
            x, xb = _odd_layer(x, xb, groups, j, odd_w[0], rpb[j], odd_w[1], w_router[j], b_router[j],
                               *odd_w[2:], ln_g[i], ln_b[i])
    n_p = x_prompt.shape[0] * x_prompt.shape[1]
    return (x[:n_p].reshape(x_prompt.shape), x[n_p:].reshape(x_sample.shape))
```

```python
import functools
import math

import numpy as np
import jax
import jax.numpy as jnp
from jax import lax
from jax.experimental import pallas as pl
from jax.experimental.pallas import tpu as pltpu

HEAD_DIM = 128
GRID_W = 64
N_HEADS_A = 6
DILATED_BRANCHES = ((128, 1), (512, 4), (2048, 16))
N_HEADS_B = 10
N_KV_B = 2
N_HEADS_C = 16
NA_ROWS = 8
NA_COLS = 16
N_EXPERTS = 8
TOP_K = 2
DEPTH = 4
ROPE_THETA = 10000.0
LN_EPS = 1e-5
QK_EPS = 1e-6
NEG = -1e30
SCALE = HEAD_DIM ** -0.5
LOG2E = math.log2(math.e)
DN_ALPHA = (2 * DEPTH) ** 0.25

V7X_VMEM_BYTES = 64 * 2 ** 20
VMEM_LIMIT = V7X_VMEM_BYTES - 8 * 2 ** 20
LANES = 128
ATTN_TQ = 256
ATTN_ROWS = 128
HEADS_PER_STEP_A = 2
HEADS_PER_STEP_C = 4
ROUTER_LANES = LANES

F32 = jnp.float32
BF16 = jnp.bfloat16


def _params(*sem):
    return pltpu.CompilerParams(dimension_semantics=sem, vmem_limit_bytes=VMEM_LIMIT)


def _tile(n, pref):
    if n <= pref:
        return n
    t = (pref // LANES) * LANES
    while t >= LANES:
        if n % t == 0:
            return t
        t -= LANES
    return n


def _lane_tiles(s):
    return [s[:, t * LANES:(t + 1) * LANES] for t in range(s.shape[1] // LANES)]


def _with_ones(v):
    return jnp.concatenate([v, jnp.ones_like(v)], axis=1)


def _proj_kernel(x_ref, w_ref, cs_ref, o_ref):
    acc = jnp.dot(x_ref[...], w_ref[...], preferred_element_type=F32)
    o_ref[...] = (acc * cs_ref[...]).astype(o_ref.dtype)


def _project(xb, w_stack, layer, col_scale):
    t, k = xb.shape
    n = w_stack.shape[2]
    tm, tn = _tile(t, 1024), _tile(n, 1024)
    return pl.pallas_call(
        _proj_kernel,
        grid=(t // tm, n // tn),
        in_specs=[pl.BlockSpec((tm, k), lambda i, j: (i, 0)),
                  pl.BlockSpec((None, k, tn), lambda i, j: (layer, 0, j)),
                  pl.BlockSpec((1, tn), lambda i, j: (0, j))],
        out_specs=pl.BlockSpec((tm, tn), lambda i, j: (i, j)),
        out_shape=jax.ShapeDtypeStruct((t, n), BF16),
        compiler_params=_params("arbitrary", "arbitrary"),
    )(xb, w_stack, col_scale)


def _layer_norm(z, g, b):
    mu = jnp.mean(z, axis=-1, keepdims=True)
    zc = z - mu
    var = jnp.mean(zc * zc, axis=-1, keepdims=True)
    return zc * lax.rsqrt(var + LN_EPS) * g + b


def _qk_prep_kernel(p_ref, g_ref, sc_ref, cos_ref, sin_ref, o_ref, *, heads):
    c = cos_ref[...]
    s = sin_ref[...]
    for r in range(heads):
        x = p_ref[:, r * HEAD_DIM:(r + 1) * HEAD_DIM].astype(F32)
        ms = jnp.mean(x * x, axis=-1, keepdims=True)
        xn = x * lax.rsqrt(ms + QK_EPS) * g_ref[0, r:r + 1, :]
        y = xn * c + pltpu.roll(xn, HEAD_DIM // 2, 1) * s
        o_ref[:, r * HEAD_DIM:(r + 1) * HEAD_DIM] = (y * sc_ref[0, r:r + 1, :]).astype(o_ref.dtype)


def _qk_prep(proj, gains, scales, cos_t, sin_t, col0):
    t = proj.shape[0]
    n_heads = N_HEADS_B + N_KV_B
    half = n_heads // 2
    wblk = half * HEAD_DIM
    assert col0 % wblk == 0
    tm = _tile(t, 512)
    return pl.pallas_call(
        functools.partial(_qk_prep_kernel, heads=half),
        grid=(t // tm, 2),
        in_specs=[pl.BlockSpec((tm, wblk), lambda i, j: (i, col0 // wblk + j)),
                  pl.BlockSpec((1, half, HEAD_DIM), lambda i, j: (j, 0, 0)),
                  pl.BlockSpec((1, half, HEAD_DIM), lambda i, j: (j, 0, 0)),
                  pl.BlockSpec((tm, HEAD_DIM), lambda i, j: (i, 0)),
                  pl.BlockSpec((tm, HEAD_DIM), lambda i, j: (i, 0))],
        out_specs=pl.BlockSpec((tm, wblk), lambda i, j: (i, j)),
        out_shape=jax.ShapeDtypeStruct((t, n_heads * HEAD_DIM), BF16),
        compiler_params=_params("arbitrary", "arbitrary"),
    )(proj, gains, scales, cos_t, sin_t)


def _window_start_blk(i, halo_blks, win_blks, n_blks):
    return jnp.clip(i - halo_blks, 0, n_blks - win_blks)


def _win_attn_kernel(q_ref, k_ref, v_ref, tb_ref, o_ref, *, halo_blks, win_blks, n_blks, axis):
    i = pl.program_id(axis)
    tq = q_ref.shape[0]
    w = win_blks * tq
    start = pl.multiple_of(_window_start_blk(i, halo_blks, win_blks, n_blks) * tq, tq)
    hp = tb_ref.shape[0]
    for hd in range(hp):
        cols = slice(hd * HEAD_DIM, (hd + 1) * HEAD_DIM)
        k = k_ref[pl.ds(start, w), cols]
        v2 = _with_ones(v_ref[pl.ds(start, w), cols])
        for rb in range(tq // ATTN_ROWS):
            sl = slice(rb * ATTN_ROWS, (rb + 1) * ATTN_ROWS)
            s = lax.dot_general(q_ref[sl, cols], k, (((1,), (1,)), ((), ())), preferred_element_type=F32)
            tiles = _lane_tiles(s + tb_ref[hd, 0, sl, :])
            m = jnp.max(functools.reduce(jnp.maximum, tiles), axis=-1, keepdims=True)
            p = jnp.concatenate([jnp.exp2(t - m) for t in tiles], axis=1).astype(BF16)
            pv = jnp.dot(p, v2, preferred_element_type=F32)
            o_ref[sl, cols] = (pv[:, :HEAD_DIM] / pv[:, HEAD_DIM:]).astype(o_ref.dtype)


def _win_attn(src, table, *, row0, batch, seq, n_heads, q_blk0, k_blk0, v_blk0, halo_blks, win_blks, hp):
    tq = ATTN_TQ
    n_blks = seq // tq
    assert seq % tq == 0 and n_blks >= win_blks and row0 % seq == 0
    assert n_heads % hp == 0 and q_blk0 % hp == 0 and k_blk0 % hp == 0 and v_blk0 % hp == 0
    seq_blk0 = row0 // seq
    qrow0 = row0 // tq
    wh = hp * HEAD_DIM
    qb, kb, vb = q_blk0 // hp, k_blk0 // hp, v_blk0 // hp

    def variant(i):
        return i - _window_start_blk(i, halo_blks, win_blks, n_blks)

    kern = functools.partial(_win_attn_kernel, halo_blks=halo_blks, win_blks=win_blks,
                             n_blks=n_blks, axis=2)
    return pl.pallas_call(
        kern,
        grid=(batch, n_heads // hp, n_blks),
        in_specs=[pl.BlockSpec((tq, wh), lambda b, h, i: (qrow0 + b * n_blks + i, qb + h)),
                  pl.BlockSpec((seq, wh), lambda b, h, i: (seq_blk0 + b, kb + h), pipeline_mode=pl.Buffered(1)),
                  pl.BlockSpec((seq, wh), lambda b, h, i: (seq_blk0 + b, vb + h), pipeline_mode=pl.Buffered(1)),
                  pl.BlockSpec((hp, 1, tq, win_blks * tq), lambda b, h, i: (h, variant(i), 0, 0))],
        out_specs=pl.BlockSpec((tq, wh), lambda b, h, i: (b * n_blks + i, h)),
        out_shape=jax.ShapeDtypeStruct((batch * seq, n_heads * HEAD_DIM), BF16),
        compiler_params=_params("arbitrary", "arbitrary", "arbitrary"),
    )(src, src, src, table)


def _dilated_table():
    tq = ATTN_TQ
    halo = max(w // 2 for w, _ in DILATED_BRANCHES)
    halo_blks = halo // tq
    win_blks = 2 * halo_blks + 1
    v = jnp.arange(win_blks)[:, None, None]
    r = jnp.arange(tq)[None, :, None]
    c = jnp.arange(win_blks * tq)[None, None, :]
    delta = c - v * tq - r
    ad = jnp.abs(delta)
    cnt = jnp.zeros(delta.shape, F32)
    for window, dil in DILATED_BRANCHES:
        cnt = cnt + ((ad <= window // 2) & (delta % dil == 0)).astype(F32)
    slopes = 2.0 ** (-8.0 * jnp.arange(1, N_HEADS_A + 1, dtype=F32) / N_HEADS_A)
    bias = -slopes[:, None, None, None] * ad.astype(F32)[None] + jnp.log(jnp.maximum(cnt, 1.0))[None]
    table = jnp.where(cnt[None] > 0, bias * LOG2E, NEG)
    return table, halo_blks, win_blks


def _natten_table(rpb):
    tq = ATTN_TQ
    rows_per_tile = tq // GRID_W
    halo_blks = 1
    win_blks = 3
    assert NA_ROWS // 2 == rows_per_tile
    win_rows = win_blks * rows_per_tile
    qt = np.arange(win_blks)[:, None] * tq + np.arange(tq)[None, :]
    rq, qc = qt // GRID_W, qt % GRID_W
    kt = np.arange(win_blks * tq)
    rk, kc = kt // GRID_W, kt % GRID_W
    rs = np.clip(rq - NA_ROWS // 2, 0, win_rows - NA_ROWS)
    cs = np.clip(qc - NA_COLS // 2, 0, GRID_W - NA_COLS)
    valid = ((rk[None, None, :] >= rs[..., None]) & (rk[None, None, :] < rs[..., None] + NA_ROWS)
             & (kc[None, None, :] >= cs[..., None]) & (kc[None, None, :] < cs[..., None] + NA_COLS))
    n_ro, n_co = 2 * NA_ROWS - 1, 2 * NA_COLS - 1
    rows = np.arange(win_rows)
    ri = np.clip(rows[None, :] - rows[:, None] + NA_ROWS - 1, 0, n_ro - 1)
    cols = np.arange(GRID_W)
    ci = np.clip(cols[None, :] - cols[:, None] + NA_COLS - 1, 0, n_co - 1)
    onehot = (np.arange(n_co)[:, None, None] == ci[None]).astype(np.float32).reshape(n_co, GRID_W * GRID_W)
    t1 = rpb.astype(F32)[:, ri.reshape(-1), :]
    hi = t1.astype(BF16)
    mid = (t1 - hi.astype(F32)).astype(BF16)
    lo = (t1 - hi.astype(F32) - mid.astype(F32)).astype(BF16)
    oh = jnp.asarray(onehot, BF16)
    t2 = sum(jnp.einsum('hpb,bq->hpq', part, oh, preferred_element_type=F32) for part in (hi, mid, lo))
    t2 = t2.reshape(-1, win_rows, win_rows, GRID_W, GRID_W).transpose(0, 1, 3, 2, 4)
    bias = t2.reshape(-1, win_blks, tq, win_blks * tq)
    table = jnp.where(jnp.asarray(valid)[None], bias * LOG2E, NEG)
    return table, halo_blks, win_blks


GQA_TQ = 1024
GQA_TK = 2048
GQA_ROWS = 256


def _gqa_kernel(q_ref, k_ref, v_ref, o_ref, qs_ref, acc_ref, m_ref, *, rep, tk, rows):
    tq = q_ref.shape[0]
    seq = k_ref.shape[0]
    for r in range(rep):
        qs_ref[r * tq:(r + 1) * tq, :] = q_ref[:, r * HEAD_DIM:(r + 1) * HEAD_DIM]
    m_ref[...] = jnp.full(m_ref.shape, -jnp.inf, F32)
    acc_ref[...] = jnp.zeros(acc_ref.shape, F32)

    def body(c, carry):
        off = pl.multiple_of(c * tk, tk)
        k = k_ref[pl.ds(off, tk), :]
        v2 = _with_ones(v_ref[pl.ds(off, tk), :])
        for rb in range(rep * tq // rows):
            sl = slice(rb * rows, (rb + 1) * rows)
            s = lax.dot_general(qs_ref[sl, :], k, (((1,), (1,)), ((), ())), preferred_element_type=F32)
            tiles = _lane_tiles(s)
            m_prev = m_ref[sl, :]
            m_new = jnp.maximum(m_prev, jnp.max(functools.reduce(jnp.maximum, tiles), axis=-1, keepdims=True))
            alpha = jnp.exp2(m_prev - m_new)
            p = jnp.concatenate([jnp.exp2(t - m_new) for t in tiles], axis=1).astype(BF16)
            pv = jnp.dot(p, v2, preferred_element_type=F32)
            acc_ref[sl, :] = jnp.concatenate([alpha, alpha], axis=1) * acc_ref[sl, :] + pv
            m_ref[sl, :] = m_new
        return carry

    lax.fori_loop(0, seq // tk, body, 0)
    acc = acc_ref[...]
    o = acc[:, :HEAD_DIM] / acc[:, HEAD_DIM:]
    for r in range(rep):
        o_ref[:, r * HEAD_DIM:(r + 1) * HEAD_DIM] = o[r * tq:(r + 1) * tq].astype(o_ref.dtype)


def _gqa(qk, proj, *, row0, batch, seq, v_blk0):
    rep = N_HEADS_B // N_KV_B
    tq = GQA_TQ
    tk = _tile(seq, GQA_TK)
    n_blks = seq // tq
    assert seq % tq == 0 and row0 % seq == 0 and (rep * tq) % GQA_ROWS == 0
    seq_blk0 = row0 // seq
    qrow0 = row0 // tq
    wq = rep * HEAD_DIM
    return pl.pallas_call(
        functools.partial(_gqa_kernel, rep=rep, tk=tk, rows=GQA_ROWS),
        grid=(batch, N_KV_B, n_blks),
        in_specs=[pl.BlockSpec((tq, wq), lambda b, g, i: (qrow0 + b * n_blks + i, g)),
                  pl.BlockSpec((seq, HEAD_DIM), lambda b, g, i: (seq_blk0 + b, N_HEADS_B + g)),
                  pl.BlockSpec((seq, HEAD_DIM), lambda b, g, i: (seq_blk0 + b, v_blk0 + g))],
        out_specs=pl.BlockSpec((tq, wq), lambda b, g, i: (b * n_blks + i, g)),
        out_shape=jax.ShapeDtypeStruct((batch * seq, N_HEADS_B * HEAD_DIM), BF16),
        scratch_shapes=[pltpu.VMEM((rep * tq, HEAD_DIM), BF16),
                        pltpu.VMEM((rep * tq, 2 * HEAD_DIM), F32),
                        pltpu.VMEM((rep * tq, HEAD_DIM), F32)],
        compiler_params=_params("arbitrary", "arbitrary", "arbitrary"),
    )(qk, qk, proj)


def _out_ln_kernel(x_ref, *refs, widths, bounds, router):
    i = pl.program_id(0)
    n_groups = len(bounds) + 1
    n_act = len(widths)
    w_ref, g_ref, b_ref = refs[n_act * n_groups:n_act * n_groups + 3]
    rest = refs[n_act * n_groups + 3:]
    h = None
    row0 = 0
    for a, width in enumerate(widths):
        grp = refs[a * n_groups:(a + 1) * n_groups]
        val = grp[-1][...]
        for gi in reversed(range(n_groups - 1)):
            val = jnp.where(i < bounds[gi], grp[gi][...], val)
        part = jnp.dot(val, w_ref[row0:row0 + width, :], preferred_element_type=F32)
        h = part if h is None else h + part
        row0 += width
    y = _layer_norm(DN_ALPHA * x_ref[...] + h, g_ref[...], b_ref[...])
    if not router:
        y_ref, yb_ref = rest
        y_ref[...] = y
        yb_ref[...] = y.astype(BF16)
        return
    wr_hi_ref, wr_lo_ref, br_ref, y_ref, r_ref = rest
    y_hi = y.astype(BF16)
    y_lo = (y - y_hi.astype(F32)).astype(BF16)
    y_ref[...] = y
    logits = (jnp.dot(y_hi, wr_hi_ref[...], preferred_element_type=F32)
              + jnp.dot(y_lo, wr_hi_ref[...], preferred_element_type=F32)
              + jnp.dot(y_hi, wr_lo_ref[...], preferred_element_type=F32)) + br_ref[...]
    lane = lax.broadcasted_iota(jnp.int32, logits.shape, 1).astype(F32)
    m1 = jnp.max(logits, axis=-1, keepdims=True)
    i1 = jnp.min(jnp.where(logits == m1, lane, float(ROUTER_LANES)), axis=-1, keepdims=True)
    rest_l = jnp.where(lane == i1, -jnp.inf, logits)
    m2 = jnp.max(rest_l, axis=-1, keepdims=True)
    i2 = jnp.min(jnp.where(rest_l == m2, lane, float(ROUTER_LANES)), axis=-1, keepdims=True)
    e2 = jnp.exp(m2 - m1)
    den = 1.0 + e2
    g1 = 1.0 / den
    g2 = e2 / den
    sel = ((lane == i1) | (lane == i2)).astype(F32)
    e = N_EXPERTS
    out = jnp.where(lane < e, sel, 0.0)
    out = jnp.where(lane == e, i1, out)
    out = jnp.where(lane == e + 1, i2, out)
    out = jnp.where(lane == e + 2, g1, out)
    out = jnp.where(lane == e + 3, g2, out)
    r_ref[...] = out


def _out_ln(x, acts, w_stack, layer, g, b, router=None):
    t, d = x.shape
    group_rows = [a.shape[0] for a in acts[0]]
    tm = _tile(math.gcd(*group_rows), 512)
    starts = np.cumsum([0] + [r // tm for r in group_rows])
    row = lambda i: (i, 0)
    const = lambda i: (0, 0)
    in_specs = [pl.BlockSpec((tm, d), row)]
    args = [x]
    for per_group in acts:
        for gi, a in enumerate(per_group):
            lo, n = int(starts[gi]), int(starts[gi + 1] - starts[gi])
            in_specs.append(pl.BlockSpec((tm, a.shape[1]),
                                         lambda i, lo=lo, n=n: (jnp.clip(i - lo, 0, n - 1), 0)))
            args.append(a)
    in_specs += [pl.BlockSpec((None,) + w_stack.shape[1:], lambda i: (layer, 0, 0), pipeline_mode=pl.Buffered(1)),
                 pl.BlockSpec((1, d), const), pl.BlockSpec((1, d), const)]
    args += [w_stack, g, b]
    out_specs = [pl.BlockSpec((tm, d), row)]
    out_shape = [jax.ShapeDtypeStruct((t, d), F32)]
    widths = tuple(per_group[0].shape[1] for per_group in acts)
    bounds = tuple(int(s) for s in starts[1:-1])
    if router is None:
        out_specs.append(pl.BlockSpec((tm, d), row))
        out_shape.append(jax.ShapeDtypeStruct((t, d), BF16))
    else:
        in_specs += [pl.BlockSpec((d, ROUTER_LANES), const), pl.BlockSpec((d, ROUTER_LANES), const),
                     pl.BlockSpec((1, ROUTER_LANES), const)]
        out_specs.append(pl.BlockSpec((tm, ROUTER_LANES), row))
        out_shape.append(jax.ShapeDtypeStruct((t, ROUTER_LANES), F32))
        args += list(router)
    return pl.pallas_call(
        functools.partial(_out_ln_kernel, widths=widths, bounds=bounds, router=router is not None),
        grid=(t // tm,),
        in_specs=in_specs, out_specs=out_specs, out_shape=out_shape,
        compiler_params=_params("arbitrary"),
    )(*args)


def _swiglu_accumulate(acc_ref, xb, w1, w3, w2):
    h1 = jnp.dot(xb, w1, preferred_element_type=F32)
    h3 = jnp.dot(xb, w3, preferred_element_type=F32)
    h = (h1 * (1.0 / (1.0 + jnp.exp(-h1)))) * h3
    acc_ref[...] += jnp.dot(h.astype(BF16), w2, preferred_element_type=F32)


def _ffn_ln_kernel(x_ref, xb_ref, w1_ref, w3_ref, w2_ref, g_ref, b_ref, y_ref, yb_ref, acc_ref):
    j = pl.program_id(1)

    @pl.when(j == 0)
    def _():
        acc_ref[...] = jnp.zeros(acc_ref.shape, F32)

    _swiglu_accumulate(acc_ref, xb_ref[...], w1_ref[...], w3_ref[...], w2_ref[...])

    @pl.when(j == pl.num_programs(1) - 1)
    def _():
        y = _layer_norm(DN_ALPHA * x_ref[...] + acc_ref[...], g_ref[...], b_ref[...])
        y_ref[...] = y
        yb_ref[...] = y.astype(BF16)


def _ffn_ln(x, xb, w1, w3, w2, layer, g, b):
    t, d = x.shape
    f = w1.shape[2]
    tm, tf = _tile(t, 512), _tile(f, 512)
    return pl.pallas_call(
        _ffn_ln_kernel,
        grid=(t // tm, f // tf),
        in_specs=[pl.BlockSpec((tm, d), lambda i, j: (i, 0)),
                  pl.BlockSpec((tm, d), lambda i, j: (i, 0)),
                  pl.BlockSpec((None, d, tf), lambda i, j: (layer, 0, j)),
                  pl.BlockSpec((None, d, tf), lambda i, j: (layer, 0, j)),
                  pl.BlockSpec((None, tf, d), lambda i, j: (layer, j, 0)),
                  pl.BlockSpec((1, d), lambda i, j: (0, 0)),
                  pl.BlockSpec((1, d), lambda i, j: (0, 0))],
        out_specs=[pl.BlockSpec((tm, d), lambda i, j: (i, 0)),
                   pl.BlockSpec((tm, d), lambda i, j: (i, 0))],
        out_shape=[jax.ShapeDtypeStruct((t, d), F32), jax.ShapeDtypeStruct((t, d), BF16)],
        scratch_shapes=[pltpu.VMEM((tm, d), F32)],
        compiler_params=_params("arbitrary", "arbitrary"),
    )(x, xb, w1, w3, w2, g, b)


MOE_TM = 1024
DISPATCH_TM = 512
COMBINE_TM = 256
DMA_ISSUE_UNROLL = 8
ZERO_ROWS = 256


def _dispatch_kernel(p1_ref, p2_ref, ends_ref, x_ref, xs_hbm, zbuf, sem, zsem, *, tm, group_tm):
    i = pl.program_id(0)
    base = i * tm

    @pl.when(i == 0)
    def _():
        zbuf[...] = jnp.zeros(zbuf.shape, zbuf.dtype)
        zrows = zbuf.shape[0]

        def clear_tile(row0):
            for c in range(group_tm // zrows):
                start = pl.multiple_of(row0 + c * zrows, zrows)
                pltpu.make_async_copy(zbuf, xs_hbm.at[pl.ds(start, zrows)], zsem).start()
            for c in range(group_tm // zrows):
                pltpu.make_async_copy(zbuf, xs_hbm.at[pl.ds(0, zrows)], zsem).wait()

        for e in range(N_EXPERTS):
            end = ends_ref[e]
            prev_end = ends_ref[e - 1] if e else 0

            @pl.when(end > prev_end)
            def _():
                clear_tile(end - group_tm)

        for k in range(N_EXPERTS):
            tail = ends_ref[N_EXPERTS - 1] + k * group_tm

            @pl.when(tail < xs_hbm.shape[0])
            def _():
                clear_tile(tail)

    def issue(t, carry):
        src = x_ref.at[pl.ds(t, 1)]
        pltpu.make_async_copy(src, xs_hbm.at[pl.ds(p1_ref[base + t], 1)], sem).start()
        pltpu.make_async_copy(src, xs_hbm.at[pl.ds(p2_ref[base + t], 1)], sem).start()
        return carry

    lax.fori_loop(0, tm, issue, 0, unroll=DMA_ISSUE_UNROLL)
    for _ in range(TOP_K):
        pltpu.make_async_copy(x_ref, xs_hbm.at[pl.ds(0, tm)], sem).wait()


def _dispatch(x, p1, p2, ends, n_rows):
    t, d = x.shape
    tm = _tile(t, DISPATCH_TM)
    grid_spec = pltpu.PrefetchScalarGridSpec(
        num_scalar_prefetch=3,
        grid=(t // tm,),
        in_specs=[pl.BlockSpec((tm, d), lambda i, a, c, e: (i, 0))],
        out_specs=pl.BlockSpec(memory_space=pl.ANY),
        scratch_shapes=[pltpu.VMEM((ZERO_ROWS, d), x.dtype), pltpu.SemaphoreType.DMA(()),
                        pltpu.SemaphoreType.DMA(())])
    return pl.pallas_call(
        functools.partial(_dispatch_kernel, tm=tm, group_tm=MOE_TM),
        grid_spec=grid_spec,
        out_shape=jax.ShapeDtypeStruct((n_rows, d), x.dtype),
        compiler_params=_params("arbitrary"),
    )(p1, p2, ends, x)


def _moe_expert_kernel(te_ref, last_ref, xs_ref, w1_ref, w3_ref, w2_ref, y_ref, xb_ref):
    i = pl.program_id(0)
    j = pl.program_id(1)
    active = i <= last_ref[0]

    @pl.when(j == 0)
    def _():
        y_ref[...] = jnp.zeros(y_ref.shape, F32)

    @pl.when(active)
    def _():
        @pl.when(j == 0)
        def _():
            xb_ref[...] = xs_ref[...].astype(BF16)

        _swiglu_accumulate(y_ref, xb_ref[...], w1_ref[...], w3_ref[...], w2_ref[...])


def _moe_experts(xs, tile_expert, last_tile, w1, w3, w2, layer):
    p, d = xs.shape
    f = w1.shape[3]
    tm, tf = MOE_TM, _tile(f, 256)
    grid_spec = pltpu.PrefetchScalarGridSpec(
        num_scalar_prefetch=2,
        grid=(p // tm, f // tf),
        in_specs=[pl.BlockSpec((tm, d), lambda i, j, te, ac: (jnp.minimum(i, ac[0]), 0)),
                  pl.BlockSpec((None, None, d, tf), lambda i, j, te, ac: (layer, te[i], 0, j)),
                  pl.BlockSpec((None, None, d, tf), lambda i, j, te, ac: (layer, te[i], 0, j)),
                  pl.BlockSpec((None, None, tf, d), lambda i, j, te, ac: (layer, te[i], j, 0))],
        out_specs=pl.BlockSpec((tm, d), lambda i, j, te, ac: (i, 0)),
        scratch_shapes=[pltpu.VMEM((tm, d), BF16)])
    return pl.pallas_call(
        _moe_expert_kernel,
        grid_spec=grid_spec,
        out_shape=jax.ShapeDtypeStruct((p, d), F32),
        compiler_params=_params("arbitrary", "arbitrary"),
    )(tile_expert, last_tile, xs, w1, w3, w2)


def _combine_ln_kernel(p1_ref, p2_ref, x_ref, gt_ref, g_ref, b_ref, ys_hbm, *refs, tm, n_out, bounds):
    out_refs = refs[:n_out]
    buf, sem = refs[n_out:]
    i = pl.program_id(0)
    n = pl.num_programs(0)

    def issue(tile, slot):
        base = tile * tm

        def body(t, carry):
            pltpu.make_async_copy(ys_hbm.at[pl.ds(p1_ref[base + t], 1)],
                                  buf.at[slot, pl.ds(t, 1)], sem.at[slot]).start()
            pltpu.make_async_copy(ys_hbm.at[pl.ds(p2_ref[base + t], 1)],
                                  buf.at[slot, pl.ds(tm + t, 1)], sem.at[slot]).start()
            return carry

        lax.fori_loop(0, tm, body, 0, unroll=DMA_ISSUE_UNROLL)

    @pl.when(i == 0)
    def _():
        issue(0, 0)

    @pl.when(i + 1 < n)
    def _():
        issue(i + 1, (i + 1) % 2)

    slot = i % 2
    pltpu.make_async_copy(ys_hbm.at[pl.ds(0, 2 * tm)], buf.at[slot], sem.at[slot]).wait()
    g1 = gt_ref[:, 0:1]
    g2 = gt_ref[:, 1:2]
    f = g1 * buf[slot, pl.ds(0, tm), :] + g2 * buf[slot, pl.ds(tm, tm), :]
    y = _layer_norm(DN_ALPHA * x_ref[...] + f, g_ref[...], b_ref[...])
    if bounds is None:
        y_ref, yb_ref = out_refs
        y_ref[...] = y
        yb_ref[...] = y.astype(BF16)
    else:
        edges = (0,) + bounds + (None,)
        for gi, o_ref in enumerate(out_refs):
            lo, hi = edges[gi], edges[gi + 1]
            in_group = (i >= lo) if hi is None else ((i >= lo) & (i < hi))

            @pl.when(in_group)
            def _(o_ref=o_ref):
                o_ref[...] = y


def _combine_ln(x, ys, p1, p2, gates, g, b, final_group_rows=None):
    t, d = x.shape
    rows = [t] if final_group_rows is None else list(final_group_rows)
    tm = _tile(math.gcd(*rows) if len(rows) > 1 else t, COMBINE_TM)
    row = lambda i, a, c: (i, 0)
    const = lambda i, a, c: (0, 0)
    if final_group_rows is None:
        bounds = None
        out_specs = [pl.BlockSpec((tm, d), row), pl.BlockSpec((tm, d), row)]
        out_shape = [jax.ShapeDtypeStruct((t, d), F32), jax.ShapeDtypeStruct((t, d), BF16)]
    else:
        starts = np.cumsum([0] + [r // tm for r in rows])
        bounds = tuple(int(s) for s in starts[1:-1])
        out_specs, out_shape = [], []
        for gi, r in enumerate(rows):
            lo, n = int(starts[gi]), int(starts[gi + 1] - starts[gi])
            out_specs.append(pl.BlockSpec((tm, d), lambda i, a, c, lo=lo, n=n: (jnp.clip(i - lo, 0, n - 1), 0)))
            out_shape.append(jax.ShapeDtypeStruct((r, d), F32))
    grid_spec = pltpu.PrefetchScalarGridSpec(
        num_scalar_prefetch=2,
        grid=(t // tm,),
        in_specs=[pl.BlockSpec((tm, d), row), pl.BlockSpec((tm, LANES), row),
                  pl.BlockSpec((1, d), const), pl.BlockSpec((1, d), const),
                  pl.BlockSpec(memory_space=pl.ANY)],
        out_specs=out_specs,
        scratch_shapes=[pltpu.VMEM((2, 2 * tm, d), F32), pltpu.SemaphoreType.DMA((2,))])
    return pl.pallas_call(
        functools.partial(_combine_ln_kernel, tm=tm, n_out=len(out_specs), bounds=bounds),
        grid_spec=grid_spec,
        out_shape=out_shape,
        compiler_params=_params("arbitrary"),
    )(p1, p2, x, gates, g, b, ys)


def _moe_routing(r, tm):
    t = r.shape[0]
    e = N_EXPERTS
    sel = r[:, :e].astype(jnp.int32)
    i1 = r[:, e].astype(jnp.int32)
    i2 = r[:, e + 1].astype(jnp.int32)
    cnt = jnp.cumsum(sel, axis=0)
    rank = cnt - sel
    padded = ((cnt[-1] + tm - 1) // tm) * tm
    ends = jnp.cumsum(padded)
    pos = (ends - padded)[None, :] + rank
    lane = jnp.arange(e, dtype=jnp.int32)[None, :]
    p1 = jnp.sum(jnp.where(lane == i1[:, None], pos, 0), axis=1).astype(jnp.int32)
    p2 = jnp.sum(jnp.where(lane == i2[:, None], pos, 0), axis=1).astype(jnp.int32)
    n_rows = TOP_K * t + e * tm
    tile_start = jnp.arange(n_rows // tm, dtype=jnp.int32) * tm
    tile_expert = jnp.sum((tile_start[:, None] >= ends[None, :]).astype(jnp.int32), axis=1)
    tile_expert = jnp.minimum(tile_expert, e - 1).astype(jnp.int32)
    last_tile = (ends[-1:] // tm - 1).astype(jnp.int32)
    return p1, p2, ends.astype(jnp.int32), tile_expert, last_tile, n_rows


def _deinterleave_perm():
    half = HEAD_DIM // 2
    return np.concatenate([np.arange(half) * 2, np.arange(half) * 2 + 1])


def _rope_tables(groups):
    pos = np.concatenate([np.tile(np.arange(seq), batch) for batch, seq in groups])
    pos = jnp.asarray(pos, jnp.int32)
    row = (pos // GRID_W).astype(F32)
    col = (pos % GRID_W).astype(F32)
    axis_dim = HEAD_DIM // 2
    inv = ROPE_THETA ** (-jnp.arange(0, axis_dim, 2, dtype=F32) / axis_dim)
    ang = jnp.concatenate([row[:, None] * inv, col[:, None] * inv], axis=-1)
    cos, sin = jnp.cos(ang), jnp.sin(ang)
    return jnp.concatenate([cos, cos], axis=-1), jnp.concatenate([-sin, sin], axis=-1)


def _group_rows(groups):
    out, row0 = [], 0
    for batch, seq in groups:
        out.append((row0, batch, seq))
        row0 += batch * seq
    return out


def _prep_w_in_even(w_in):
    perm = _deinterleave_perm()
    n_qk = N_HEADS_B + N_KV_B
    c0 = 3 * N_HEADS_A * HEAD_DIM
    pmat = np.zeros((HEAD_DIM, HEAD_DIM), np.float32)
    pmat[perm, np.arange(HEAD_DIM)] = 1.0
    w_in_b = w_in.astype(BF16)
    lead = w_in.shape[:2]
    w_qk = w_in_b[:, :, c0:c0 + n_qk * HEAD_DIM].reshape(*lead, n_qk, HEAD_DIM)
    w_qk = jnp.einsum('ldhk,kn->ldhn', w_qk, jnp.asarray(pmat, BF16), preferred_element_type=F32)
    return jnp.concatenate([w_in_b[:, :, :c0], w_qk.astype(BF16).reshape(*lead, n_qk * HEAD_DIM),
                            w_in_b[:, :, c0 + n_qk * HEAD_DIM:]], axis=2)


def _even_layer(x, xb, groups, layer, w_in_b, qk_gain, w_out_b, w1_b, w3_b, w2_b, ln_g, ln_b, rope, dil):
    w_a = N_HEADS_A * HEAD_DIM
    w_bq = N_HEADS_B * HEAD_DIM
    w_bkv = N_KV_B * HEAD_DIM
    perm = _deinterleave_perm()
    col_scale = jnp.concatenate([jnp.full((w_a,), SCALE * LOG2E, F32),
                                 jnp.ones((w_in_b.shape[2] - w_a,), F32)])[None, :]
    proj = _project(xb, w_in_b, layer, col_scale)

    n_b = N_HEADS_B + N_KV_B
    gains = jnp.concatenate([jnp.tile(qk_gain[0][perm][None], (N_HEADS_B, 1)),
                             jnp.tile(qk_gain[1][perm][None], (N_KV_B, 1))]).astype(F32)
    scales = jnp.concatenate([jnp.full((N_HEADS_B, HEAD_DIM), SCALE * LOG2E, F32),
                              jnp.ones((N_KV_B, HEAD_DIM), F32)])
    qk = _qk_prep(proj, gains.reshape(2, n_b // 2, HEAD_DIM), scales.reshape(2, n_b // 2, HEAD_DIM),
                  rope[0], rope[1], 3 * w_a)

    table, halo_blks, win_blks = dil
    oa, ob = [], []
    v_blk0 = (3 * w_a + w_bq + w_bkv) // HEAD_DIM
    for row0, batch, seq in _group_rows(groups):
        oa.append(_win_attn(proj, table, row0=row0, batch=batch, seq=seq, n_heads=N_HEADS_A,
                            q_blk0=0, k_blk0=N_HEADS_A, v_blk0=2 * N_HEADS_A,
                            halo_blks=halo_blks, win_blks=win_blks, hp=HEADS_PER_STEP_A))
        ob.append(_gqa(qk, proj, row0=row0, batch=batch, seq=seq, v_blk0=v_blk0))
    x, xb = _out_ln(x, [oa, ob], w_out_b, layer, ln_g[0][None], ln_b[0][None])
    return _ffn_ln(x, xb, w1_b, w3_b, w2_b, layer, ln_g[1][None], ln_b[1][None])


def _odd_layer(x, xb, groups, layer, w_in_b, rpb, w_out_b, w_router, b_router, w1_b, w3_b, w2_b, ln_g, ln_b,
               final):
    w_c = N_HEADS_C * HEAD_DIM
    col_scale = jnp.concatenate([jnp.full((w_c,), SCALE * LOG2E, F32), jnp.ones((2 * w_c,), F32)])[None, :]
    proj = _project(xb, w_in_b, layer, col_scale)
    table, halo_blks, win_blks = _natten_table(rpb)
    o = []
    for row0, batch, seq in _group_rows(groups):
        o.append(_win_attn(proj, table, row0=row0, batch=batch, seq=seq, n_heads=N_HEADS_C,
                           q_blk0=0, k_blk0=N_HEADS_C, v_blk0=2 * N_HEADS_C,
                           halo_blks=halo_blks, win_blks=win_blks, hp=HEADS_PER_STEP_C))

    wr = jnp.pad(w_router.astype(F32), ((0, 0), (0, ROUTER_LANES - N_EXPERTS)))
    wr_hi = wr.astype(BF16)
    wr_lo = (wr - wr_hi.astype(F32)).astype(BF16)
    br = jnp.concatenate([b_router.astype(F32), jnp.full((ROUTER_LANES - N_EXPERTS,), NEG, F32)])[None, :]
    x, r = _out_ln(x, [o], w_out_b, layer, ln_g[0][None], ln_b[0][None], router=(wr_hi, wr_lo, br))

    p1, p2, ends, tile_expert, last_tile, n_rows = _moe_routing(r, MOE_TM)
    xs = _dispatch(x, p1, p2, ends, n_rows)
    ys = _moe_experts(xs, tile_expert, last_tile, w1_b, w3_b, w2_b, layer)
    gates = jnp.pad(r[:, N_EXPERTS + 2:N_EXPERTS + 4], ((0, 0), (0, LANES - 2)))
    final_rows = [batch * seq for batch, seq in groups] if final else None
    return _combine_ln(x, ys, p1, p2, gates, ln_g[1][None], ln_b[1][None], final_group_rows=final_rows)


def kernel(x_prompt, x_sample, ln_g, ln_b, w_in_even, qk_gain_b, w_out_even, ffn_w1, ffn_w3, ffn_w2,
           w_in_odd, rpb, w_out_odd, w_router, b_router, moe_w1, moe_w3, moe_w2):
    d = x_prompt.shape[-1]
    groups = [(x_prompt.shape[0], x_prompt.shape[1]), (x_sample.shape[0], x_sample.shape[1])]
    x = jnp.concatenate([x_prompt.reshape(-1, d), x_sample.reshape(-1, d)], axis=0)
    xb = x.astype(BF16)
    rope = _rope_tables(groups)
    dil = _dilated_table()
    even_w = (_prep_w_in_even(w_in_even), w_out_even.astype(BF16),
              ffn_w1.astype(BF16), ffn_w3.astype(BF16), ffn_w2.astype(BF16))
    odd_w = (w_in_odd.astype(BF16), w_out_odd.astype(BF16),
             moe_w1.astype(BF16), moe_w3.astype(BF16), moe_w2.astype(BF16))
    depth = ln_g.shape[0]
    for i in range(depth):
        j = i // 2
        if i % 2 == 0:
            x, xb = _even_layer(x, xb, groups, j, even_w[0], qk_gain_b[j], even_w[1], *even_w[2:],
                                ln_g[i], ln_b[i], rope, dil)
        else:
            x, xb = _odd_layer(x, xb, groups, j, odd_w[0], rpb[j], odd_w[1], w_router[j], b_router[j],
                               *odd_w[2:], ln_g[i], ln_b[i], final=(i == depth - 1))
    if depth % 2 == 0:
        return (x.reshape(x_prompt.shape), xb.reshape(x_sample.shape))
    n_p = x_prompt.shape[0] * x_prompt.shape[1]
    return (x[:n_p].reshape(x_prompt.shape), x[n_p:].reshape(x_sample.shape))
```

```python
import functools
import math

import numpy as np
import jax
import jax.numpy as jnp
from jax import lax
from jax.experimental import pallas as pl
from jax.experimental.pallas import tpu as pltpu

HEAD_DIM = 128
GRID_W = 64
N_HEADS_A = 6
DILATED_BRANCHES = ((128, 1), (512, 4), (2048, 16))
N_HEADS_B = 10
N_KV_B = 2
N_HEADS_C = 16
NA_ROWS = 8
NA_COLS = 16
N_EXPERTS = 8
TOP_K = 2
DEPTH = 4
ROPE_THETA = 10000.0
LN_EPS = 1e-5
QK_EPS = 1e-6
NEG = -1e30
SCALE = HEAD_DIM ** -0.5
LOG2E = math.log2(math.e)
DN_ALPHA = (2 * DEPTH) ** 0.25

V7X_VMEM_BYTES = 64 * 2 ** 20
VMEM_LIMIT = V7X_VMEM_BYTES - 8 * 2 ** 20
LANES = 128
ATTN_TQ = 256
ATTN_ROWS = 128
HEADS_PER_STEP_A = 3
HEADS_PER_STEP_C = 4
ROUTER_LANES = LANES

F32 = jnp.float32
BF16 = jnp.bfloat16


def _params(*sem):
    return pltpu.CompilerParams(dimension_semantics=sem, vmem_limit_bytes=VMEM_LIMIT)


def _tile(n, pref):
    if n <= pref:
        return n
    t = (pref // LANES) * LANES
    while t >= LANES:
        if n % t == 0:
            return t
        t -= LANES
    return n


def _lane_tiles(s):
    return [s[:, t * LANES:(t + 1) * LANES] for t in range(s.shape[1] // LANES)]


def _with_ones(v):
    return jnp.concatenate([v, jnp.ones_like(v)], axis=1)


def _proj_kernel(x_ref, w_ref, cs_ref, o_ref):
    acc = jnp.dot(x_ref[...], w_ref[...], preferred_element_type=F32)
    o_ref[...] = (acc * cs_ref[...]).astype(o_ref.dtype)


def _project(xb, w_stack, layer, col_scale):
    t, k = xb.shape
    n = w_stack.shape[2]
    tm, tn = _tile(t, 1024), _tile(n, 1024)
    return pl.pallas_call(
        _proj_kernel,
        grid=(t // tm, n // tn),
        in_specs=[pl.BlockSpec((tm, k), lambda i, j: (i, 0)),
                  pl.BlockSpec((None, k, tn), lambda i, j: (layer, 0, j)),
                  pl.BlockSpec((1, tn), lambda i, j: (0, j))],
        out_specs=pl.BlockSpec((tm, tn), lambda i, j: (i, j)),
        out_shape=jax.ShapeDtypeStruct((t, n), BF16),
        compiler_params=_params("arbitrary", "arbitrary"),
    )(xb, w_stack, col_scale)


def _layer_norm(z, g, b):
    mu = jnp.mean(z, axis=-1, keepdims=True)
    zc = z - mu
    var = jnp.mean(zc * zc, axis=-1, keepdims=True)
    return zc * lax.rsqrt(var + LN_EPS) * g + b


def _qk_prep_kernel(p_ref, g_ref, sc_ref, cos_ref, sin_ref, o_ref, *, heads):
    c = cos_ref[...]
    s = sin_ref[...]
    for r in range(heads):
        x = p_ref[:, r * HEAD_DIM:(r + 1) * HEAD_DIM].astype(F32)
        ms = jnp.mean(x * x, axis=-1, keepdims=True)
        xn = x * lax.rsqrt(ms + QK_EPS) * g_ref[0, r:r + 1, :]
        y = xn * c + pltpu.roll(xn, HEAD_DIM // 2, 1) * s
        o_ref[:, r * HEAD_DIM:(r + 1) * HEAD_DIM] = (y * sc_ref[0, r:r + 1, :]).astype(o_ref.dtype)


def _qk_prep(proj, gains, scales, cos_t, sin_t, col0):
    t = proj.shape[0]
    n_heads = N_HEADS_B + N_KV_B
    half = n_heads // 2
    wblk = half * HEAD_DIM
    assert col0 % wblk == 0
    tm = _tile(t, 512)
    return pl.pallas_call(
        functools.partial(_qk_prep_kernel, heads=half),
        grid=(t // tm, 2),
        in_specs=[pl.BlockSpec((tm, wblk), lambda i, j: (i, col0 // wblk + j)),
                  pl.BlockSpec((1, half, HEAD_DIM), lambda i, j: (j, 0, 0)),
                  pl.BlockSpec((1, half, HEAD_DIM), lambda i, j: (j, 0, 0)),
                  pl.BlockSpec((tm, HEAD_DIM), lambda i, j: (i, 0)),
                  pl.BlockSpec((tm, HEAD_DIM), lambda i, j: (i, 0))],
        out_specs=pl.BlockSpec((tm, wblk), lambda i, j: (i, j)),
        out_shape=jax.ShapeDtypeStruct((t, n_heads * HEAD_DIM), BF16),
        compiler_params=_params("arbitrary", "arbitrary"),
    )(proj, gains, scales, cos_t, sin_t)


def _window_start_blk(i, halo_blks, win_blks, n_blks):
    return jnp.clip(i - halo_blks, 0, n_blks - win_blks)


def _win_attn_kernel(q_ref, k_ref, v_ref, tb_ref, o_ref, *, halo_blks, win_blks, n_blks, axis):
    i = pl.program_id(axis)
    tq = q_ref.shape[0]
    w = win_blks * tq
    start = pl.multiple_of(_window_start_blk(i, halo_blks, win_blks, n_blks) * tq, tq)
    hp = tb_ref.shape[0]
    for hd in range(hp):
        cols = slice(hd * HEAD_DIM, (hd + 1) * HEAD_DIM)
        k = k_ref[pl.ds(start, w), cols]
        v2 = _with_ones(v_ref[pl.ds(start, w), cols])
        for rb in range(tq // ATTN_ROWS):
            sl = slice(rb * ATTN_ROWS, (rb + 1) * ATTN_ROWS)
            s = lax.dot_general(q_ref[sl, cols], k, (((1,), (1,)), ((), ())), preferred_element_type=F32)
            tiles = _lane_tiles(s + tb_ref[hd, 0, sl, :])
            m = jnp.max(functools.reduce(jnp.maximum, tiles), axis=-1, keepdims=True)
            p = jnp.concatenate([jnp.exp2(t - m) for t in tiles], axis=1).astype(BF16)
            pv = jnp.dot(p, v2, preferred_element_type=F32)
            o_ref[sl, cols] = (pv[:, :HEAD_DIM] / pv[:, HEAD_DIM:]).astype(o_ref.dtype)


def _win_attn(src, table, *, row0, batch, seq, n_heads, q_blk0, k_blk0, v_blk0, halo_blks, win_blks, hp):
    tq = ATTN_TQ
    n_blks = seq // tq
    assert seq % tq == 0 and n_blks >= win_blks and row0 % seq == 0
    assert n_heads % hp == 0 and q_blk0 % hp == 0 and k_blk0 % hp == 0 and v_blk0 % hp == 0
    seq_blk0 = row0 // seq
    qrow0 = row0 // tq
    wh = hp * HEAD_DIM
    qb, kb, vb = q_blk0 // hp, k_blk0 // hp, v_blk0 // hp

    def variant(i):
        return i - _window_start_blk(i, halo_blks, win_blks, n_blks)

    kern = functools.partial(_win_attn_kernel, halo_blks=halo_blks, win_blks=win_blks,
                             n_blks=n_blks, axis=2)
    return pl.pallas_call(
        kern,
        grid=(batch, n_heads // hp, n_blks),
        in_specs=[pl.BlockSpec((tq, wh), lambda b, h, i: (qrow0 + b * n_blks + i, qb + h)),
                  pl.BlockSpec((seq, wh), lambda b, h, i: (seq_blk0 + b, kb + h), pipeline_mode=pl.Buffered(1)),
                  pl.BlockSpec((seq, wh), lambda b, h, i: (seq_blk0 + b, vb + h), pipeline_mode=pl.Buffered(1)),
                  pl.BlockSpec((hp, 1, tq, win_blks * tq), lambda b, h, i: (h, variant(i), 0, 0))],
        out_specs=pl.BlockSpec((tq, wh), lambda b, h, i: (b * n_blks + i, h)),
        out_shape=jax.ShapeDtypeStruct((batch * seq, n_heads * HEAD_DIM), BF16),
        compiler_params=_params("arbitrary", "arbitrary", "arbitrary"),
    )(src, src, src, table)


def _dilated_table():
    tq = ATTN_TQ
    halo = max(w // 2 for w, _ in DILATED_BRANCHES)
    halo_blks = halo // tq
    win_blks = 2 * halo_blks + 1
    v = jnp.arange(win_blks)[:, None, None]
    r = jnp.arange(tq)[None, :, None]
    c = jnp.arange(win_blks * tq)[None, None, :]
    delta = c - v * tq - r
    ad = jnp.abs(delta)
    cnt = jnp.zeros(delta.shape, F32)
    for window, dil in DILATED_BRANCHES:
        cnt = cnt + ((ad <= window // 2) & (delta % dil == 0)).astype(F32)
    slopes = 2.0 ** (-8.0 * jnp.arange(1, N_HEADS_A + 1, dtype=F32) / N_HEADS_A)
    bias = -slopes[:, None, None, None] * ad.astype(F32)[None] + jnp.log(jnp.maximum(cnt, 1.0))[None]
    table = jnp.where(cnt[None] > 0, bias * LOG2E, NEG)
    return table, halo_blks, win_blks


def _natten_table(rpb):
    tq = ATTN_TQ
    rows_per_tile = tq // GRID_W
    halo_blks = 1
    win_blks = 3
    assert NA_ROWS // 2 == rows_per_tile
    win_rows = win_blks * rows_per_tile
    qt = np.arange(win_blks)[:, None] * tq + np.arange(tq)[None, :]
    rq, qc = qt // GRID_W, qt % GRID_W
    kt = np.arange(win_blks * tq)
    rk, kc = kt // GRID_W, kt % GRID_W
    rs = np.clip(rq - NA_ROWS // 2, 0, win_rows - NA_ROWS)
    cs = np.clip(qc - NA_COLS // 2, 0, GRID_W - NA_COLS)
    valid = ((rk[None, None, :] >= rs[..., None]) & (rk[None, None, :] < rs[..., None] + NA_ROWS)
             & (kc[None, None, :] >= cs[..., None]) & (kc[None, None, :] < cs[..., None] + NA_COLS))
    n_ro, n_co = 2 * NA_ROWS - 1, 2 * NA_COLS - 1
    rows = np.arange(win_rows)
    ri = np.clip(rows[None, :] - rows[:, None] + NA_ROWS - 1, 0, n_ro - 1)
    cols = np.arange(GRID_W)
    ci = np.clip(cols[None, :] - cols[:, None] + NA_COLS - 1, 0, n_co - 1)
    onehot = (np.arange(n_co)[:, None, None] == ci[None]).astype(np.float32).reshape(n_co, GRID_W * GRID_W)
    t1 = rpb.astype(F32)[:, ri.reshape(-1), :]
    hi = t1.astype(BF16)
    mid = (t1 - hi.astype(F32)).astype(BF16)
    lo = (t1 - hi.astype(F32) - mid.astype(F32)).astype(BF16)
    oh = jnp.asarray(onehot, BF16)
    t2 = sum(jnp.einsum('hpb,bq->hpq', part, oh, preferred_element_type=F32) for part in (hi, mid, lo))
    t2 = t2.reshape(-1, win_rows, win_rows, GRID_W, GRID_W).transpose(0, 1, 3, 2, 4)
    bias = t2.reshape(-1, win_blks, tq, win_blks * tq)
    table = jnp.where(jnp.asarray(valid)[None], bias * LOG2E, NEG)
    return table, halo_blks, win_blks


GQA_TQ = 1024
GQA_TK = 2048
GQA_ROWS = 256


def _gqa_kernel(q_ref, k_ref, v_ref, o_ref, qs_ref, acc_ref, m_ref, *, rep, tk, rows):
    tq = q_ref.shape[0]
    seq = k_ref.shape[0]
    for r in range(rep):
        qs_ref[r * tq:(r + 1) * tq, :] = q_ref[:, r * HEAD_DIM:(r + 1) * HEAD_DIM]
    m_ref[...] = jnp.full(m_ref.shape, -jnp.inf, F32)
    acc_ref[...] = jnp.zeros(acc_ref.shape, F32)

    def body(c, carry):
        off = pl.multiple_of(c * tk, tk)
        k = k_ref[pl.ds(off, tk), :]
        v2 = _with_ones(v_ref[pl.ds(off, tk), :])
        for rb in range(rep * tq // rows):
            sl = slice(rb * rows, (rb + 1) * rows)
            s = lax.dot_general(qs_ref[sl, :], k, (((1,), (1,)), ((), ())), preferred_element_type=F32)
            tiles = _lane_tiles(s)
            m_prev = m_ref[sl, :]
            m_new = jnp.maximum(m_prev, jnp.max(functools.reduce(jnp.maximum, tiles), axis=-1, keepdims=True))
            alpha = jnp.exp2(m_prev - m_new)
            p = jnp.concatenate([jnp.exp2(t - m_new) for t in tiles], axis=1).astype(BF16)
            pv = jnp.dot(p, v2, preferred_element_type=F32)
            acc_ref[sl, :] = jnp.concatenate([alpha, alpha], axis=1) * acc_ref[sl, :] + pv
            m_ref[sl, :] = m_new
        return carry

    lax.fori_loop(0, seq // tk, body, 0)
    acc = acc_ref[...]
    o = acc[:, :HEAD_DIM] / acc[:, HEAD_DIM:]
    for r in range(rep):
        o_ref[:, r * HEAD_DIM:(r + 1) * HEAD_DIM] = o[r * tq:(r + 1) * tq].astype(o_ref.dtype)


def _gqa(qk, proj, *, row0, batch, seq, v_blk0):
    rep = N_HEADS_B // N_KV_B
    tq = GQA_TQ
    tk = _tile(seq, GQA_TK)
    n_blks = seq // tq
    assert seq % tq == 0 and row0 % seq == 0 and (rep * tq) % GQA_ROWS == 0
    seq_blk0 = row0 // seq
    qrow0 = row0 // tq
    wq = rep * HEAD_DIM
    return pl.pallas_call(
        functools.partial(_gqa_kernel, rep=rep, tk=tk, rows=GQA_ROWS),
        grid=(batch, N_KV_B, n_blks),
        in_specs=[pl.BlockSpec((tq, wq), lambda b, g, i: (qrow0 + b * n_blks + i, g)),
                  pl.BlockSpec((seq, HEAD_DIM), lambda b, g, i: (seq_blk0 + b, N_HEADS_B + g)),
                  pl.BlockSpec((seq, HEAD_DIM), lambda b, g, i: (seq_blk0 + b, v_blk0 + g))],
        out_specs=pl.BlockSpec((tq, wq), lambda b, g, i: (b * n_blks + i, g)),
        out_shape=jax.ShapeDtypeStruct((batch * seq, N_HEADS_B * HEAD_DIM), BF16),
        scratch_shapes=[pltpu.VMEM((rep * tq, HEAD_DIM), BF16),
                        pltpu.VMEM((rep * tq, 2 * HEAD_DIM), F32),
                        pltpu.VMEM((rep * tq, HEAD_DIM), F32)],
        compiler_params=_params("arbitrary", "arbitrary", "arbitrary"),
    )(qk, qk, proj)


def _out_ln_kernel(x_ref, *refs, widths, bounds, router):
    i = pl.program_id(0)
    n_groups = len(bounds) + 1
    n_act = len(widths)
    w_ref, g_ref, b_ref = refs[n_act * n_groups:n_act * n_groups + 3]
    rest = refs[n_act * n_groups + 3:]
    h = None
    row0 = 0
    for a, width in enumerate(widths):
        grp = refs[a * n_groups:(a + 1) * n_groups]
        val = grp[-1][...]
        for gi in reversed(range(n_groups - 1)):
            val = jnp.where(i < bounds[gi], grp[gi][...], val)
        part = jnp.dot(val, w_ref[row0:row0 + width, :], preferred_element_type=F32)
        h = part if h is None else h + part
        row0 += width
    y = _layer_norm(DN_ALPHA * x_ref[...] + h, g_ref[...], b_ref[...])
    if not router:
        (y_ref,) = rest
        y_ref[...] = y
        return
    wr_hi_ref, wr_lo_ref, br_ref, y_ref, r_ref = rest
    y_hi = y.astype(BF16)
    y_lo = (y - y_hi.astype(F32)).astype(BF16)
    y_ref[...] = y
    logits = (jnp.dot(y_hi, wr_hi_ref[...], preferred_element_type=F32)
              + jnp.dot(y_lo, wr_hi_ref[...], preferred_element_type=F32)
              + jnp.dot(y_hi, wr_lo_ref[...], preferred_element_type=F32)) + br_ref[...]
    lane = lax.broadcasted_iota(jnp.int32, logits.shape, 1).astype(F32)
    m1 = jnp.max(logits, axis=-1, keepdims=True)
    i1 = jnp.min(jnp.where(logits == m1, lane, float(ROUTER_LANES)), axis=-1, keepdims=True)
    rest_l = jnp.where(lane == i1, -jnp.inf, logits)
    m2 = jnp.max(rest_l, axis=-1, keepdims=True)
    i2 = jnp.min(jnp.where(rest_l == m2, lane, float(ROUTER_LANES)), axis=-1, keepdims=True)
    e2 = jnp.exp(m2 - m1)
    den = 1.0 + e2
    g1 = 1.0 / den
    g2 = e2 / den
    sel = ((lane == i1) | (lane == i2)).astype(F32)
    e = N_EXPERTS
    out = jnp.where(lane < e, sel, 0.0)
    out = jnp.where(lane == e, i1, out)
    out = jnp.where(lane == e + 1, i2, out)
    out = jnp.where(lane == e + 2, g1, out)
    out = jnp.where(lane == e + 3, g2, out)
    r_ref[...] = out


def _out_ln(x, acts, w_stack, layer, g, b, router=None):
    t, d = x.shape
    group_rows = [a.shape[0] for a in acts[0]]
    tm = _tile(math.gcd(*group_rows), 512)
    starts = np.cumsum([0] + [r // tm for r in group_rows])
    row = lambda i: (i, 0)
    const = lambda i: (0, 0)
    in_specs = [pl.BlockSpec((tm, d), row)]
    args = [x]
    for per_group in acts:
        for gi, a in enumerate(per_group):
            lo, n = int(starts[gi]), int(starts[gi + 1] - starts[gi])
            in_specs.append(pl.BlockSpec((tm, a.shape[1]),
                                         lambda i, lo=lo, n=n: (jnp.clip(i - lo, 0, n - 1), 0)))
            args.append(a)
    in_specs += [pl.BlockSpec((None,) + w_stack.shape[1:], lambda i: (layer, 0, 0), pipeline_mode=pl.Buffered(1)),
                 pl.BlockSpec((1, d), const), pl.BlockSpec((1, d), const)]
    args += [w_stack, g, b]
    out_specs = [pl.BlockSpec((tm, d), row)]
    out_shape = [jax.ShapeDtypeStruct((t, d), F32)]
    widths = tuple(per_group[0].shape[1] for per_group in acts)
    bounds = tuple(int(s) for s in starts[1:-1])
    if router is not None:
        in_specs += [pl.BlockSpec((d, ROUTER_LANES), const), pl.BlockSpec((d, ROUTER_LANES), const),
                     pl.BlockSpec((1, ROUTER_LANES), const)]
        out_specs.append(pl.BlockSpec((tm, ROUTER_LANES), row))
        out_shape.append(jax.ShapeDtypeStruct((t, ROUTER_LANES), F32))
        args += list(router)
    return pl.pallas_call(
        functools.partial(_out_ln_kernel, widths=widths, bounds=bounds, router=router is not None),
        grid=(t // tm,),
        in_specs=in_specs, out_specs=out_specs, out_shape=out_shape,
        compiler_params=_params("arbitrary"),
    )(*args)


def _swiglu_accumulate(acc_ref, xb, w1, w3, w2):
    h1 = jnp.dot(xb, w1, preferred_element_type=F32)
    h3 = jnp.dot(xb, w3, preferred_element_type=F32)
    h = (h1 * (1.0 / (1.0 + jnp.exp(-h1)))) * h3
    acc_ref[...] += jnp.dot(h.astype(BF16), w2, preferred_element_type=F32)


def _ffn_ln_kernel(x_ref, w1_ref, w3_ref, w2_ref, g_ref, b_ref, y_ref, yb_ref, xb_ref):
    j = pl.program_id(1)

    @pl.when(j == 0)
    def _():
        xb_ref[...] = x_ref[...].astype(BF16)
        y_ref[...] = jnp.zeros(y_ref.shape, F32)

    _swiglu_accumulate(y_ref, xb_ref[...], w1_ref[...], w3_ref[...], w2_ref[...])

    @pl.when(j == pl.num_programs(1) - 1)
    def _():
        y = _layer_norm(DN_ALPHA * x_ref[...] + y_ref[...], g_ref[...], b_ref[...])
        y_ref[...] = y
        yb_ref[...] = y.astype(BF16)


FFN_TM = 1024


def _ffn_ln(x, w1, w3, w2, layer, g, b):
    t, d = x.shape
    f = w1.shape[2]
    tm, tf = _tile(t, FFN_TM), _tile(f, 512)
    return pl.pallas_call(
        _ffn_ln_kernel,
        grid=(t // tm, f // tf),
        in_specs=[pl.BlockSpec((tm, d), lambda i, j: (i, 0), pipeline_mode=pl.Buffered(1)),
                  pl.BlockSpec((None, d, tf), lambda i, j: (layer, 0, j)),
                  pl.BlockSpec((None, d, tf), lambda i, j: (layer, 0, j)),
                  pl.BlockSpec((None, tf, d), lambda i, j: (layer, j, 0)),
                  pl.BlockSpec((1, d), lambda i, j: (0, 0)),
                  pl.BlockSpec((1, d), lambda i, j: (0, 0))],
        out_specs=[pl.BlockSpec((tm, d), lambda i, j: (i, 0), pipeline_mode=pl.Buffered(1)),
                   pl.BlockSpec((tm, d), lambda i, j: (i, 0), pipeline_mode=pl.Buffered(1))],
        out_shape=[jax.ShapeDtypeStruct((t, d), F32), jax.ShapeDtypeStruct((t, d), BF16)],
        scratch_shapes=[pltpu.VMEM((tm, d), BF16)],
        compiler_params=_params("arbitrary", "arbitrary"),
    )(x, w1, w3, w2, g, b)


MOE_TM = 1024
DISPATCH_TM = 512
COMBINE_TM = 256
DMA_ISSUE_UNROLL = 8
ZERO_ROWS = 256


def _dispatch_kernel(p1_ref, p2_ref, ends_ref, x_ref, xs_hbm, zbuf, sem, zsem, *, tm, group_tm):
    i = pl.program_id(0)
    base = i * tm

    @pl.when(i == 0)
    def _():
        zbuf[...] = jnp.zeros(zbuf.shape, zbuf.dtype)
        zrows = zbuf.shape[0]

        def clear_tile(row0):
            for c in range(group_tm // zrows):
                start = pl.multiple_of(row0 + c * zrows, zrows)
                pltpu.make_async_copy(zbuf, xs_hbm.at[pl.ds(start, zrows)], zsem).start()
            for c in range(group_tm // zrows):
                pltpu.make_async_copy(zbuf, xs_hbm.at[pl.ds(0, zrows)], zsem).wait()

        for e in range(N_EXPERTS):
            end = ends_ref[e]
            prev_end = ends_ref[e - 1] if e else 0

            @pl.when(end > prev_end)
            def _():
                clear_tile(end - group_tm)

        for k in range(N_EXPERTS):
            tail = ends_ref[N_EXPERTS - 1] + k * group_tm

            @pl.when(tail < xs_hbm.shape[0])
            def _():
                clear_tile(tail)

    def issue(t, carry):
        src = x_ref.at[pl.ds(t, 1)]
        pltpu.make_async_copy(src, xs_hbm.at[pl.ds(p1_ref[base + t], 1)], sem).start()
        pltpu.make_async_copy(src, xs_hbm.at[pl.ds(p2_ref[base + t], 1)], sem).start()
        return carry

    lax.fori_loop(0, tm, issue, 0, unroll=DMA_ISSUE_UNROLL)
    for _ in range(TOP_K):
        pltpu.make_async_copy(x_ref, xs_hbm.at[pl.ds(0, tm)], sem).wait()


def _dispatch(x, p1, p2, ends, n_rows):
    t, d = x.shape
    tm = _tile(t, DISPATCH_TM)
    grid_spec = pltpu.PrefetchScalarGridSpec(
        num_scalar_prefetch=3,
        grid=(t // tm,),
        in_specs=[pl.BlockSpec((tm, d), lambda i, a, c, e: (i, 0))],
        out_specs=pl.BlockSpec(memory_space=pl.ANY),
        scratch_shapes=[pltpu.VMEM((ZERO_ROWS, d), x.dtype), pltpu.SemaphoreType.DMA(()),
                        pltpu.SemaphoreType.DMA(())])
    return pl.pallas_call(
        functools.partial(_dispatch_kernel, tm=tm, group_tm=MOE_TM),
        grid_spec=grid_spec,
        out_shape=jax.ShapeDtypeStruct((n_rows, d), x.dtype),
        compiler_params=_params("arbitrary"),
    )(p1, p2, ends, x)


def _moe_expert_kernel(te_ref, last_ref, xs_ref, w1_ref, w3_ref, w2_ref, y_ref, xb_ref):
    i = pl.program_id(0)
    j = pl.program_id(1)
    active = i <= last_ref[0]

    @pl.when(j == 0)
    def _():
        y_ref[...] = jnp.zeros(y_ref.shape, F32)

    @pl.when(active)
    def _():
        @pl.when(j == 0)
        def _():
            xb_ref[...] = xs_ref[...].astype(BF16)

        _swiglu_accumulate(y_ref, xb_ref[...], w1_ref[...], w3_ref[...], w2_ref[...])


def _moe_experts(xs, tile_expert, last_tile, w1, w3, w2, layer):
    p, d = xs.shape
    f = w1.shape[3]
    tm, tf = MOE_TM, _tile(f, 256)
    grid_spec = pltpu.PrefetchScalarGridSpec(
        num_scalar_prefetch=2,
        grid=(p // tm, f // tf),
        in_specs=[pl.BlockSpec((tm, d), lambda i, j, te, ac: (jnp.minimum(i, ac[0]), 0)),
                  pl.BlockSpec((None, None, d, tf), lambda i, j, te, ac: (layer, te[i], 0, j)),
                  pl.BlockSpec((None, None, d, tf), lambda i, j, te, ac: (layer, te[i], 0, j)),
                  pl.BlockSpec((None, None, tf, d), lambda i, j, te, ac: (layer, te[i], j, 0))],
        out_specs=pl.BlockSpec((tm, d), lambda i, j, te, ac: (i, 0)),
        scratch_shapes=[pltpu.VMEM((tm, d), BF16)])
    return pl.pallas_call(
        _moe_expert_kernel,
        grid_spec=grid_spec,
        out_shape=jax.ShapeDtypeStruct((p, d), F32),
        compiler_params=_params("arbitrary", "arbitrary"),
    )(tile_expert, last_tile, xs, w1, w3, w2)


def _combine_ln_kernel(p1_ref, p2_ref, x_ref, gt_ref, g_ref, b_ref, ys_hbm, *refs, tm, n_out, bounds):
    out_refs = refs[:n_out]
    buf, sem = refs[n_out:]
    i = pl.program_id(0)
    n = pl.num_programs(0)

    def issue(tile, slot):
        base = tile * tm

        def body(t, carry):
            pltpu.make_async_copy(ys_hbm.at[pl.ds(p1_ref[base + t], 1)],
                                  buf.at[slot, pl.ds(t, 1)], sem.at[slot]).start()
            pltpu.make_async_copy(ys_hbm.at[pl.ds(p2_ref[base + t], 1)],
                                  buf.at[slot, pl.ds(tm + t, 1)], sem.at[slot]).start()
            return carry

        lax.fori_loop(0, tm, body, 0, unroll=DMA_ISSUE_UNROLL)

    @pl.when(i == 0)
    def _():
        issue(0, 0)

    @pl.when(i + 1 < n)
    def _():
        issue(i + 1, (i + 1) % 2)

    slot = i % 2
    pltpu.make_async_copy(ys_hbm.at[pl.ds(0, 2 * tm)], buf.at[slot], sem.at[slot]).wait()
    g1 = gt_ref[:, 0:1]
    g2 = gt_ref[:, 1:2]
    f = g1 * buf[slot, pl.ds(0, tm), :] + g2 * buf[slot, pl.ds(tm, tm), :]
    y = _layer_norm(DN_ALPHA * x_ref[...] + f, g_ref[...], b_ref[...])
    if bounds is None:
        y_ref, yb_ref = out_refs
        y_ref[...] = y
        yb_ref[...] = y.astype(BF16)
    else:
        edges = (0,) + bounds + (None,)
        for gi, o_ref in enumerate(out_refs):
            lo, hi = edges[gi], edges[gi + 1]
            in_group = (i >= lo) if hi is None else ((i >= lo) & (i < hi))

            @pl.when(in_group)
            def _(o_ref=o_ref):
                o_ref[...] = y


def _combine_ln(x, ys, p1, p2, gates, g, b, final_group_rows=None):
    t, d = x.shape
    rows = [t] if final_group_rows is None else list(final_group_rows)
    tm = _tile(math.gcd(*rows) if len(rows) > 1 else t, COMBINE_TM)
    row = lambda i, a, c: (i, 0)
    const = lambda i, a, c: (0, 0)
    if final_group_rows is None:
        bounds = None
        out_specs = [pl.BlockSpec((tm, d), row), pl.BlockSpec((tm, d), row)]
        out_shape = [jax.ShapeDtypeStruct((t, d), F32), jax.ShapeDtypeStruct((t, d), BF16)]
    else:
        starts = np.cumsum([0] + [r // tm for r in rows])
        bounds = tuple(int(s) for s in starts[1:-1])
        out_specs, out_shape = [], []
        for gi, r in enumerate(rows):
            lo, n = int(starts[gi]), int(starts[gi + 1] - starts[gi])
            out_specs.append(pl.BlockSpec((tm, d), lambda i, a, c, lo=lo, n=n: (jnp.clip(i - lo, 0, n - 1), 0)))
            out_shape.append(jax.ShapeDtypeStruct((r, d), F32))
    grid_spec = pltpu.PrefetchScalarGridSpec(
        num_scalar_prefetch=2,
        grid=(t // tm,),
        in_specs=[pl.BlockSpec((tm, d), row), pl.BlockSpec((tm, LANES), row),
                  pl.BlockSpec((1, d), const), pl.BlockSpec((1, d), const),
                  pl.BlockSpec(memory_space=pl.ANY)],
        out_specs=out_specs,
        scratch_shapes=[pltpu.VMEM((2, 2 * tm, d), F32), pltpu.SemaphoreType.DMA((2,))])
    return pl.pallas_call(
        functools.partial(_combine_ln_kernel, tm=tm, n_out=len(out_specs), bounds=bounds),
        grid_spec=grid_spec,
        out_shape=out_shape,
        compiler_params=_params("arbitrary"),
    )(p1, p2, x, gates, g, b, ys)


def _moe_routing(r, tm):
    t = r.shape[0]
    e = N_EXPERTS
    sel = r[:, :e].astype(jnp.int32)
    i1 = r[:, e].astype(jnp.int32)
    i2 = r[:, e + 1].astype(jnp.int32)
    cnt = jnp.cumsum(sel, axis=0)
    rank = cnt - sel
    padded = ((cnt[-1] + tm - 1) // tm) * tm
    ends = jnp.cumsum(padded)
    pos = (ends - padded)[None, :] + rank
    lane = jnp.arange(e, dtype=jnp.int32)[None, :]
    p1 = jnp.sum(jnp.where(lane == i1[:, None], pos, 0), axis=1).astype(jnp.int32)
    p2 = jnp.sum(jnp.where(lane == i2[:, None], pos, 0), axis=1).astype(jnp.int32)
    n_rows = TOP_K * t + e * tm
    tile_start = jnp.arange(n_rows // tm, dtype=jnp.int32) * tm
    tile_expert = jnp.sum((tile_start[:, None] >= ends[None, :]).astype(jnp.int32), axis=1)
    tile_expert = jnp.minimum(tile_expert, e - 1).astype(jnp.int32)
    last_tile = (ends[-1:] // tm - 1).astype(jnp.int32)
    return p1, p2, ends.astype(jnp.int32), tile_expert, last_tile, n_rows


def _deinterleave_perm():
    half = HEAD_DIM // 2
    return np.concatenate([np.arange(half) * 2, np.arange(half) * 2 + 1])


def _rope_tables(groups):
    pos = np.concatenate([np.tile(np.arange(seq), batch) for batch, seq in groups])
    pos = jnp.asarray(pos, jnp.int32)
    row = (pos // GRID_W).astype(F32)
    col = (pos % GRID_W).astype(F32)
    axis_dim = HEAD_DIM // 2
    inv = ROPE_THETA ** (-jnp.arange(0, axis_dim, 2, dtype=F32) / axis_dim)
    ang = jnp.concatenate([row[:, None] * inv, col[:, None] * inv], axis=-1)
    cos, sin = jnp.cos(ang), jnp.sin(ang)
    return jnp.concatenate([cos, cos], axis=-1), jnp.concatenate([-sin, sin], axis=-1)


def _group_rows(groups):
    out, row0 = [], 0
    for batch, seq in groups:
        out.append((row0, batch, seq))
        row0 += batch * seq
    return out


def _prep_w_in_even(w_in):
    perm = _deinterleave_perm()
    n_qk = N_HEADS_B + N_KV_B
    c0 = 3 * N_HEADS_A * HEAD_DIM
    pmat = np.zeros((HEAD_DIM, HEAD_DIM), np.float32)
    pmat[perm, np.arange(HEAD_DIM)] = 1.0
    w_in_b = w_in.astype(BF16)
    lead = w_in.shape[:2]
    w_qk = w_in_b[:, :, c0:c0 + n_qk * HEAD_DIM].reshape(*lead, n_qk, HEAD_DIM)
    w_qk = jnp.einsum('ldhk,kn->ldhn', w_qk, jnp.asarray(pmat, BF16), preferred_element_type=F32)
    return jnp.concatenate([w_in_b[:, :, :c0], w_qk.astype(BF16).reshape(*lead, n_qk * HEAD_DIM),
                            w_in_b[:, :, c0 + n_qk * HEAD_DIM:]], axis=2)


def _even_layer(x, xb, groups, layer, w_in_b, qk_gain, w_out_b, w1_b, w3_b, w2_b, ln_g, ln_b, rope, dil):
    w_a = N_HEADS_A * HEAD_DIM
    w_bq = N_HEADS_B * HEAD_DIM
    w_bkv = N_KV_B * HEAD_DIM
    perm = _deinterleave_perm()
    col_scale = jnp.concatenate([jnp.full((w_a,), SCALE * LOG2E, F32),
                                 jnp.ones((w_in_b.shape[2] - w_a,), F32)])[None, :]
    proj = _project(xb, w_in_b, layer, col_scale)

    n_b = N_HEADS_B + N_KV_B
    gains = jnp.concatenate([jnp.tile(qk_gain[0][perm][None], (N_HEADS_B, 1)),
                             jnp.tile(qk_gain[1][perm][None], (N_KV_B, 1))]).astype(F32)
    scales = jnp.concatenate([jnp.full((N_HEADS_B, HEAD_DIM), SCALE * LOG2E, F32),
                              jnp.ones((N_KV_B, HEAD_DIM), F32)])
    qk = _qk_prep(proj, gains.reshape(2, n_b // 2, HEAD_DIM), scales.reshape(2, n_b // 2, HEAD_DIM),
                  rope[0], rope[1], 3 * w_a)

    table, halo_blks, win_blks = dil
    oa, ob = [], []
    v_blk0 = (3 * w_a + w_bq + w_bkv) // HEAD_DIM
    for row0, batch, seq in _group_rows(groups):
        oa.append(_win_attn(proj, table, row0=row0, batch=batch, seq=seq, n_heads=N_HEADS_A,
                            q_blk0=0, k_blk0=N_HEADS_A, v_blk0=2 * N_HEADS_A,
                            halo_blks=halo_blks, win_blks=win_blks, hp=HEADS_PER_STEP_A))
        ob.append(_gqa(qk, proj, row0=row0, batch=batch, seq=seq, v_blk0=v_blk0))
    (x,) = _out_ln(x, [oa, ob], w_out_b, layer, ln_g[0][None], ln_b[0][None])
    return _ffn_ln(x, w1_b, w3_b, w2_b, layer, ln_g[1][None], ln_b[1][None])


def _odd_layer(x, xb, groups, layer, w_in_b, rpb, w_out_b, w_router, b_router, w1_b, w3_b, w2_b, ln_g, ln_b,
               final):
    w_c = N_HEADS_C * HEAD_DIM
    col_scale = jnp.concatenate([jnp.full((w_c,), SCALE * LOG2E, F32), jnp.ones((2 * w_c,), F32)])[None, :]
    proj = _project(xb, w_in_b, layer, col_scale)
    table, halo_blks, win_blks = _natten_table(rpb)
    o = []
    for row0, batch, seq in _group_rows(groups):
        o.append(_win_attn(proj, table, row0=row0, batch=batch, seq=seq, n_heads=N_HEADS_C,
                           q_blk0=0, k_blk0=N_HEADS_C, v_blk0=2 * N_HEADS_C,
                           halo_blks=halo_blks, win_blks=win_blks, hp=HEADS_PER_STEP_C))

    wr = jnp.pad(w_router.astype(F32), ((0, 0), (0, ROUTER_LANES - N_EXPERTS)))
    wr_hi = wr.astype(BF16)
    wr_lo = (wr - wr_hi.astype(F32)).astype(BF16)
    br = jnp.concatenate([b_router.astype(F32), jnp.full((ROUTER_LANES - N_EXPERTS,), NEG, F32)])[None, :]
    x, r = _out_ln(x, [o], w_out_b, layer, ln_g[0][None], ln_b[0][None], router=(wr_hi, wr_lo, br))

    p1, p2, ends, tile_expert, last_tile, n_rows = _moe_routing(r, MOE_TM)
    xs = _dispatch(x, p1, p2, ends, n_rows)
    ys = _moe_experts(xs, tile_expert, last_tile, w1_b, w3_b, w2_b, layer)
    gates = jnp.pad(r[:, N_EXPERTS + 2:N_EXPERTS + 4], ((0, 0), (0, LANES - 2)))
    final_rows = [batch * seq for batch, seq in groups] if final else None
    return _combine_ln(x, ys, p1, p2, gates, ln_g[1][None], ln_b[1][None], final_group_rows=final_rows)


def kernel(x_prompt, x_sample, ln_g, ln_b, w_in_even, qk_gain_b, w_out_even, ffn_w1, ffn_w3, ffn_w2,
           w_in_odd, rpb, w_out_odd, w_router, b_router, moe_w1, moe_w3, moe_w2):
    d = x_prompt.shape[-1]
    groups = [(x_prompt.shape[0], x_prompt.shape[1]), (x_sample.shape[0], x_sample.shape[1])]
    x = jnp.concatenate([x_prompt.reshape(-1, d), x_sample.reshape(-1, d)], axis=0)
    xb = x.astype(BF16)
    rope = _rope_tables(groups)
    dil = _dilated_table()
    even_w = (_prep_w_in_even(w_in_even), w_out_even.astype(BF16),
              ffn_w1.astype(BF16), ffn_w3.astype(BF16), ffn_w2.astype(BF16))
    odd_w = (w_in_odd.astype(BF16), w_out_odd.astype(BF16),
             moe_w1.astype(BF16), moe_w3.astype(BF16), moe_w2.astype(BF16))
    depth = ln_g.shape[0]
    for i in range(depth):
        j = i // 2
        if i % 2 == 0:
            x, xb = _even_layer(x, xb, groups, j, even_w[0], qk_gain_b[j], even_w[1], *even_w[2:],
                                ln_g[i], ln_b[i], rope, dil)
        else:
            x, xb = _odd_layer(x, xb, groups, j, odd_w[0], rpb[j], odd_w[1], w_router[j], b_router[j],
                               *odd_w[2:], ln_g[i], ln_b[i], final=(i == depth - 1))
    if depth % 2 == 0:
        return (x.reshape(x_prompt.shape), xb.reshape(x_sample.shape))
    n_p = x_prompt.shape[0] * x_prompt.shape[1]
    return (x[:n_p].reshape(x_prompt.shape), x[n_p:].reshape(x_sample.shape))
```

```python
import functools
import math

import numpy as np
import jax
import jax.numpy as jnp
from jax import lax
from jax.experimental import pallas as pl
from jax.experimental.pallas import tpu as pltpu

HEAD_DIM = 128
GRID_W = 64
N_HEADS_A = 6
DILATED_BRANCHES = ((128, 1), (512, 4), (2048, 16))
N_HEADS_B = 10
N_KV_B = 2
N_HEADS_C = 16
NA_ROWS = 8
NA_COLS = 16
N_EXPERTS = 8
TOP_K = 2
DEPTH = 4
ROPE_THETA = 10000.0
LN_EPS = 1e-5
QK_EPS = 1e-6
NEG = -1e30
SCALE = HEAD_DIM ** -0.5
LOG2E = math.log2(math.e)
DN_ALPHA = (2 * DEPTH) ** 0.25

V7X_VMEM_BYTES = 64 * 2 ** 20
VMEM_LIMIT = V7X_VMEM_BYTES - 8 * 2 ** 20
LANES = 128
ATTN_TQ = 256
ATTN_ROWS = 128
HEADS_PER_STEP_A = 3
HEADS_PER_STEP_C = 4
ROUTER_LANES = LANES

F32 = jnp.float32
BF16 = jnp.bfloat16


def _params(*sem):
    return pltpu.CompilerParams(dimension_semantics=sem, vmem_limit_bytes=VMEM_LIMIT)


def _tile(n, pref):
    if n <= pref:
        return n
    t = (pref // LANES) * LANES
    while t >= LANES:
        if n % t == 0:
            return t
        t -= LANES
    return n


def _lane_tiles(s):
    return [s[:, t * LANES:(t + 1) * LANES] for t in range(s.shape[1] // LANES)]


def _with_ones(v):
    return jnp.concatenate([v, jnp.ones_like(v)], axis=1)


def _proj_kernel(x_ref, w_ref, cs_ref, o_ref):
    acc = jnp.dot(x_ref[...], w_ref[...], preferred_element_type=F32)
    o_ref[...] = (acc * cs_ref[...]).astype(o_ref.dtype)


def _project(xb, w_stack, layer, col_scale):
    t, k = xb.shape
    n = w_stack.shape[2]
    tm, tn = _tile(t, 1024), _tile(n, 1024)
    return pl.pallas_call(
        _proj_kernel,
        grid=(t // tm, n // tn),
        in_specs=[pl.BlockSpec((tm, k), lambda i, j: (i, 0)),
                  pl.BlockSpec((None, k, tn), lambda i, j: (layer, 0, j)),
                  pl.BlockSpec((1, tn), lambda i, j: (0, j))],
        out_specs=pl.BlockSpec((tm, tn), lambda i, j: (i, j)),
        out_shape=jax.ShapeDtypeStruct((t, n), BF16),
        compiler_params=_params("arbitrary", "arbitrary"),
    )(xb, w_stack, col_scale)


def _layer_norm(z, g, b):
    mu = jnp.mean(z, axis=-1, keepdims=True)
    zc = z - mu
    var = jnp.mean(zc * zc, axis=-1, keepdims=True)
    return zc * lax.rsqrt(var + LN_EPS) * g + b


def _qk_prep_kernel(p_ref, g_ref, sc_ref, cos_ref, sin_ref, o_ref, *, heads):
    c = cos_ref[...]
    s = sin_ref[...]
    for r in range(heads):
        x = p_ref[:, r * HEAD_DIM:(r + 1) * HEAD_DIM].astype(F32)
        ms = jnp.mean(x * x, axis=-1, keepdims=True)
        xn = x * lax.rsqrt(ms + QK_EPS) * g_ref[0, r:r + 1, :]
        y = xn * c + pltpu.roll(xn, HEAD_DIM // 2, 1) * s
        o_ref[:, r * HEAD_DIM:(r + 1) * HEAD_DIM] = (y * sc_ref[0, r:r + 1, :]).astype(o_ref.dtype)


def _qk_prep(proj, gains, scales, cos_t, sin_t, col0):
    t = proj.shape[0]
    n_heads = N_HEADS_B + N_KV_B
    half = n_heads // 2
    wblk = half * HEAD_DIM
    assert col0 % wblk == 0
    tm = _tile(t, 512)
    return pl.pallas_call(
        functools.partial(_qk_prep_kernel, heads=half),
        grid=(t // tm, 2),
        in_specs=[pl.BlockSpec((tm, wblk), lambda i, j: (i, col0 // wblk + j)),
                  pl.BlockSpec((1, half, HEAD_DIM), lambda i, j: (j, 0, 0)),
                  pl.BlockSpec((1, half, HEAD_DIM), lambda i, j: (j, 0, 0)),
                  pl.BlockSpec((tm, HEAD_DIM), lambda i, j: (i, 0)),
                  pl.BlockSpec((tm, HEAD_DIM), lambda i, j: (i, 0))],
        out_specs=pl.BlockSpec((tm, wblk), lambda i, j: (i, j)),
        out_shape=jax.ShapeDtypeStruct((t, n_heads * HEAD_DIM), BF16),
        compiler_params=_params("arbitrary", "arbitrary"),
    )(proj, gains, scales, cos_t, sin_t)


def _window_start_blk(i, halo_blks, win_blks, n_blks):
    return jnp.clip(i - halo_blks, 0, n_blks - win_blks)


def _win_attn_kernel(q_ref, k_ref, v_ref, tb_ref, o_ref, *, halo_blks, win_blks, n_blks, axis):
    i = pl.program_id(axis)
    tq = q_ref.shape[0]
    w = win_blks * tq
    start = pl.multiple_of(_window_start_blk(i, halo_blks, win_blks, n_blks) * tq, tq)
    hp = tb_ref.shape[0]
    for hd in range(hp):
        cols = slice(hd * HEAD_DIM, (hd + 1) * HEAD_DIM)
        k = k_ref[pl.ds(start, w), cols]
        v2 = _with_ones(v_ref[pl.ds(start, w), cols])
        for rb in range(tq // ATTN_ROWS):
            sl = slice(rb * ATTN_ROWS, (rb + 1) * ATTN_ROWS)
            s = lax.dot_general(q_ref[sl, cols], k, (((1,), (1,)), ((), ())), preferred_element_type=F32)
            tiles = _lane_tiles(s + tb_ref[hd, 0, sl, :])
            m = jnp.max(functools.reduce(jnp.maximum, tiles), axis=-1, keepdims=True)
            p = jnp.concatenate([jnp.exp2(t - m) for t in tiles], axis=1).astype(BF16)
            pv = jnp.dot(p, v2, preferred_element_type=F32)
            o_ref[sl, cols] = (pv[:, :HEAD_DIM] / pv[:, HEAD_DIM:]).astype(o_ref.dtype)


def _win_attn(src, table, *, row0, batch, seq, n_heads, q_blk0, k_blk0, v_blk0, halo_blks, win_blks, hp):
    tq = ATTN_TQ
    n_blks = seq // tq
    assert seq % tq == 0 and n_blks >= win_blks and row0 % seq == 0
    assert n_heads % hp == 0 and q_blk0 % hp == 0 and k_blk0 % hp == 0 and v_blk0 % hp == 0
    seq_blk0 = row0 // seq
    qrow0 = row0 // tq
    wh = hp * HEAD_DIM
    qb, kb, vb = q_blk0 // hp, k_blk0 // hp, v_blk0 // hp

    def variant(i):
        return i - _window_start_blk(i, halo_blks, win_blks, n_blks)

    kern = functools.partial(_win_attn_kernel, halo_blks=halo_blks, win_blks=win_blks,
                             n_blks=n_blks, axis=2)
    return pl.pallas_call(
        kern,
        grid=(batch, n_heads // hp, n_blks),
        in_specs=[pl.BlockSpec((tq, wh), lambda b, h, i: (qrow0 + b * n_blks + i, qb + h)),
                  pl.BlockSpec((seq, wh), lambda b, h, i: (seq_blk0 + b, kb + h), pipeline_mode=pl.Buffered(1)),
                  pl.BlockSpec((seq, wh), lambda b, h, i: (seq_blk0 + b, vb + h), pipeline_mode=pl.Buffered(1)),
                  pl.BlockSpec((hp, 1, tq, win_blks * tq), lambda b, h, i: (h, variant(i), 0, 0))],
        out_specs=pl.BlockSpec((tq, wh), lambda b, h, i: (b * n_blks + i, h)),
        out_shape=jax.ShapeDtypeStruct((batch * seq, n_heads * HEAD_DIM), BF16),
        compiler_params=_params("arbitrary", "arbitrary", "arbitrary"),
    )(src, src, src, table)


def _dilated_table():
    tq = ATTN_TQ
    halo = max(w // 2 for w, _ in DILATED_BRANCHES)
    halo_blks = halo // tq
    win_blks = 2 * halo_blks + 1
    v = jnp.arange(win_blks)[:, None, None]
    r = jnp.arange(tq)[None, :, None]
    c = jnp.arange(win_blks * tq)[None, None, :]
    delta = c - v * tq - r
    ad = jnp.abs(delta)
    cnt = jnp.zeros(delta.shape, F32)
    for window, dil in DILATED_BRANCHES:
        cnt = cnt + ((ad <= window // 2) & (delta % dil == 0)).astype(F32)
    slopes = 2.0 ** (-8.0 * jnp.arange(1, N_HEADS_A + 1, dtype=F32) / N_HEADS_A)
    bias = -slopes[:, None, None, None] * ad.astype(F32)[None] + jnp.log(jnp.maximum(cnt, 1.0))[None]
    table = jnp.where(cnt[None] > 0, bias * LOG2E, NEG)
    return table, halo_blks, win_blks


def _natten_table(rpb):
    tq = ATTN_TQ
    rows_per_tile = tq // GRID_W
    halo_blks = 1
    win_blks = 3
    assert NA_ROWS // 2 == rows_per_tile
    win_rows = win_blks * rows_per_tile
    qt = np.arange(win_blks)[:, None] * tq + np.arange(tq)[None, :]
    rq, qc = qt // GRID_W, qt % GRID_W
    kt = np.arange(win_blks * tq)
    rk, kc = kt // GRID_W, kt % GRID_W
    rs = np.clip(rq - NA_ROWS // 2, 0, win_rows - NA_ROWS)
    cs = np.clip(qc - NA_COLS // 2, 0, GRID_W - NA_COLS)
    valid = ((rk[None, None, :] >= rs[..., None]) & (rk[None, None, :] < rs[..., None] + NA_ROWS)
             & (kc[None, None, :] >= cs[..., None]) & (kc[None, None, :] < cs[..., None] + NA_COLS))
    n_ro, n_co = 2 * NA_ROWS - 1, 2 * NA_COLS - 1
    rows = np.arange(win_rows)
    ri = np.clip(rows[None, :] - rows[:, None] + NA_ROWS - 1, 0, n_ro - 1)
    cols = np.arange(GRID_W)
    ci = np.clip(cols[None, :] - cols[:, None] + NA_COLS - 1, 0, n_co - 1)
    onehot = (np.arange(n_co)[:, None, None] == ci[None]).astype(np.float32).reshape(n_co, GRID_W * GRID_W)
    t1 = rpb.astype(F32)[:, ri.reshape(-1), :]
    hi = t1.astype(BF16)
    mid = (t1 - hi.astype(F32)).astype(BF16)
    lo = (t1 - hi.astype(F32) - mid.astype(F32)).astype(BF16)
    oh = jnp.asarray(onehot, BF16)
    t2 = sum(jnp.einsum('hpb,bq->hpq', part, oh, preferred_element_type=F32) for part in (hi, mid, lo))
    t2 = t2.reshape(-1, win_rows, win_rows, GRID_W, GRID_W).transpose(0, 1, 3, 2, 4)
    bias = t2.reshape(-1, win_blks, tq, win_blks * tq)
    table = jnp.where(jnp.asarray(valid)[None], bias * LOG2E, NEG)
    return table, halo_blks, win_blks


GQA_TQ = 1024
GQA_TK = 2048
GQA_ROWS = 256


def _gqa_kernel(q_ref, k_ref, v_ref, o_ref, qs_ref, acc_ref, m_ref, *, rep, tk, rows):
    tq = q_ref.shape[0]
    seq = k_ref.shape[0]
    for r in range(rep):
        qs_ref[r * tq:(r + 1) * tq, :] = q_ref[:, r * HEAD_DIM:(r + 1) * HEAD_DIM]
    m_ref[...] = jnp.full(m_ref.shape, -jnp.inf, F32)
    acc_ref[...] = jnp.zeros(acc_ref.shape, F32)

    def body(c, carry):
        off = pl.multiple_of(c * tk, tk)
        k = k_ref[pl.ds(off, tk), :]
        v2 = _with_ones(v_ref[pl.ds(off, tk), :])
        for rb in range(rep * tq // rows):
            sl = slice(rb * rows, (rb + 1) * rows)
            s = lax.dot_general(qs_ref[sl, :], k, (((1,), (1,)), ((), ())), preferred_element_type=F32)
            tiles = _lane_tiles(s)
            m_prev = m_ref[sl, :]
            m_new = jnp.maximum(m_prev, jnp.max(functools.reduce(jnp.maximum, tiles), axis=-1, keepdims=True))
            alpha = jnp.exp2(m_prev - m_new)
            p = jnp.concatenate([jnp.exp2(t - m_new) for t in tiles], axis=1).astype(BF16)
            pv = jnp.dot(p, v2, preferred_element_type=F32)
            acc_ref[sl, :] = jnp.concatenate([alpha, alpha], axis=1) * acc_ref[sl, :] + pv
            m_ref[sl, :] = m_new
        return carry

    lax.fori_loop(0, seq // tk, body, 0)
    acc = acc_ref[...]
    o = acc[:, :HEAD_DIM] / acc[:, HEAD_DIM:]
    for r in range(rep):
        o_ref[:, r * HEAD_DIM:(r + 1) * HEAD_DIM] = o[r * tq:(r + 1) * tq].astype(o_ref.dtype)


def _gqa(qk, proj, *, row0, batch, seq, v_blk0):
    rep = N_HEADS_B // N_KV_B
    tq = GQA_TQ
    tk = _tile(seq, GQA_TK)
    n_blks = seq // tq
    assert seq % tq == 0 and row0 % seq == 0 and (rep * tq) % GQA_ROWS == 0
    seq_blk0 = row0 // seq
    qrow0 = row0 // tq
    wq = rep * HEAD_DIM
    return pl.pallas_call(
        functools.partial(_gqa_kernel, rep=rep, tk=tk, rows=GQA_ROWS),
        grid=(batch, N_KV_B, n_blks),
        in_specs=[pl.BlockSpec((tq, wq), lambda b, g, i: (qrow0 + b * n_blks + i, g)),
                  pl.BlockSpec((seq, HEAD_DIM), lambda b, g, i: (seq_blk0 + b, N_HEADS_B + g)),
                  pl.BlockSpec((seq, HEAD_DIM), lambda b, g, i: (seq_blk0 + b, v_blk0 + g))],
        out_specs=pl.BlockSpec((tq, wq), lambda b, g, i: (b * n_blks + i, g)),
        out_shape=jax.ShapeDtypeStruct((batch * seq, N_HEADS_B * HEAD_DIM), BF16),
        scratch_shapes=[pltpu.VMEM((rep * tq, HEAD_DIM), BF16),
                        pltpu.VMEM((rep * tq, 2 * HEAD_DIM), F32),
                        pltpu.VMEM((rep * tq, HEAD_DIM), F32)],
        compiler_params=_params("arbitrary", "arbitrary", "arbitrary"),
    )(qk, qk, proj)


def _out_ln_kernel(x_ref, *refs, widths, bounds, router):
    i = pl.program_id(0)
    n_groups = len(bounds) + 1
    n_act = len(widths)
    w_ref, g_ref, b_ref = refs[n_act * n_groups:n_act * n_groups + 3]
    rest = refs[n_act * n_groups + 3:]
    h = None
    row0 = 0
    for a, width in enumerate(widths):
        grp = refs[a * n_groups:(a + 1) * n_groups]
        val = grp[-1][...]
        for gi in reversed(range(n_groups - 1)):
            val = jnp.where(i < bounds[gi], grp[gi][...], val)
        part = jnp.dot(val, w_ref[row0:row0 + width, :], preferred_element_type=F32)
        h = part if h is None else h + part
        row0 += width
    y = _layer_norm(DN_ALPHA * x_ref[...] + h, g_ref[...], b_ref[...])
    if not router:
        (y_ref,) = rest
        y_ref[...] = y
        return
    wr_hi_ref, wr_lo_ref, br_ref, y_ref, r_ref = rest
    y_hi = y.astype(BF16)
    y_lo = (y - y_hi.astype(F32)).astype(BF16)
    y_ref[...] = y
    logits = (jnp.dot(y_hi, wr_hi_ref[...], preferred_element_type=F32)
              + jnp.dot(y_lo, wr_hi_ref[...], preferred_element_type=F32)
              + jnp.dot(y_hi, wr_lo_ref[...], preferred_element_type=F32)) + br_ref[...]
    lane = lax.broadcasted_iota(jnp.int32, logits.shape, 1).astype(F32)
    m1 = jnp.max(logits, axis=-1, keepdims=True)
    i1 = jnp.min(jnp.where(logits == m1, lane, float(ROUTER_LANES)), axis=-1, keepdims=True)
    rest_l = jnp.where(lane == i1, -jnp.inf, logits)
    m2 = jnp.max(rest_l, axis=-1, keepdims=True)
    i2 = jnp.min(jnp.where(rest_l == m2, lane, float(ROUTER_LANES)), axis=-1, keepdims=True)
    e2 = jnp.exp(m2 - m1)
    den = 1.0 + e2
    g1 = 1.0 / den
    g2 = e2 / den
    sel = ((lane == i1) | (lane == i2)).astype(F32)
    e = N_EXPERTS
    out = jnp.where(lane < e, sel, 0.0)
    out = jnp.where(lane == e, i1, out)
    out = jnp.where(lane == e + 1, i2, out)
    out = jnp.where(lane == e + 2, g1, out)
    out = jnp.where(lane == e + 3, g2, out)
    r_ref[...] = out


def _out_ln(x, acts, w_stack, layer, g, b, router=None):
    t, d = x.shape
    group_rows = [a.shape[0] for a in acts[0]]
    tm = _tile(math.gcd(*group_rows), 512)
    starts = np.cumsum([0] + [r // tm for r in group_rows])
    row = lambda i: (i, 0)
    const = lambda i: (0, 0)
    in_specs = [pl.BlockSpec((tm, d), row)]
    args = [x]
    for per_group in acts:
        for gi, a in enumerate(per_group):
            lo, n = int(starts[gi]), int(starts[gi + 1] - starts[gi])
            in_specs.append(pl.BlockSpec((tm, a.shape[1]),
                                         lambda i, lo=lo, n=n: (jnp.clip(i - lo, 0, n - 1), 0)))
            args.append(a)
    in_specs += [pl.BlockSpec((None,) + w_stack.shape[1:], lambda i: (layer, 0, 0), pipeline_mode=pl.Buffered(1)),
                 pl.BlockSpec((1, d), const), pl.BlockSpec((1, d), const)]
    args += [w_stack, g, b]
    out_specs = [pl.BlockSpec((tm, d), row)]
    out_shape = [jax.ShapeDtypeStruct((t, d), F32)]
    widths = tuple(per_group[0].shape[1] for per_group in acts)
    bounds = tuple(int(s) for s in starts[1:-1])
    if router is not None:
        in_specs += [pl.BlockSpec((d, ROUTER_LANES), const), pl.BlockSpec((d, ROUTER_LANES), const),
                     pl.BlockSpec((1, ROUTER_LANES), const)]
        out_specs.append(pl.BlockSpec((tm, ROUTER_LANES), row))
        out_shape.append(jax.ShapeDtypeStruct((t, ROUTER_LANES), F32))
        args += list(router)
    return pl.pallas_call(
        functools.partial(_out_ln_kernel, widths=widths, bounds=bounds, router=router is not None),
        grid=(t // tm,),
        in_specs=in_specs, out_specs=out_specs, out_shape=out_shape,
        compiler_params=_params("arbitrary"),
    )(*args)


def _swiglu_accumulate(acc_ref, xb, w1, w3, w2):
    h1 = jnp.dot(xb, w1, preferred_element_type=F32)
    h3 = jnp.dot(xb, w3, preferred_element_type=F32)
    h = (h1 * (1.0 / (1.0 + jnp.exp(-h1)))) * h3
    acc_ref[...] += jnp.dot(h.astype(BF16), w2, preferred_element_type=F32)


def _ffn_ln_kernel(x_ref, w1_ref, w3_ref, w2_ref, g_ref, b_ref, y_ref, yb_ref, xb_ref):
    j = pl.program_id(1)

    @pl.when(j == 0)
    def _():
        xb_ref[...] = x_ref[...].astype(BF16)
        y_ref[...] = jnp.zeros(y_ref.shape, F32)

    _swiglu_accumulate(y_ref, xb_ref[...], w1_ref[...], w3_ref[...], w2_ref[...])

    @pl.when(j == pl.num_programs(1) - 1)
    def _():
        y = _layer_norm(DN_ALPHA * x_ref[...] + y_ref[...], g_ref[...], b_ref[...])
        y_ref[...] = y
        yb_ref[...] = y.astype(BF16)


FFN_TM = 512


def _ffn_ln(x, w1, w3, w2, layer, g, b):
    t, d = x.shape
    f = w1.shape[2]
    tm, tf = _tile(t, FFN_TM), _tile(f, 512)
    return pl.pallas_call(
        _ffn_ln_kernel,
        grid=(t // tm, f // tf),
        in_specs=[pl.BlockSpec((tm, d), lambda i, j: (i, 0)),
                  pl.BlockSpec((None, d, tf), lambda i, j: (layer, 0, j)),
                  pl.BlockSpec((None, d, tf), lambda i, j: (layer, 0, j)),
                  pl.BlockSpec((None, tf, d), lambda i, j: (layer, j, 0)),
                  pl.BlockSpec((1, d), lambda i, j: (0, 0)),
                  pl.BlockSpec((1, d), lambda i, j: (0, 0))],
        out_specs=[pl.BlockSpec((tm, d), lambda i, j: (i, 0)),
                   pl.BlockSpec((tm, d), lambda i, j: (i, 0))],
        out_shape=[jax.ShapeDtypeStruct((t, d), F32), jax.ShapeDtypeStruct((t, d), BF16)],
        scratch_shapes=[pltpu.VMEM((tm, d), BF16)],
        compiler_params=_params("arbitrary", "arbitrary"),
    )(x, w1, w3, w2, g, b)


MOE_TM = 1024
DISPATCH_TM = 512
COMBINE_TM = 256
DMA_ISSUE_UNROLL = 8
ZERO_ROWS = 256


def _dispatch_kernel(p1_ref, p2_ref, ends_ref, x_ref, xs_hbm, zbuf, sem, zsem, *, tm, group_tm):
    i = pl.program_id(0)
    base = i * tm

    @pl.when(i == 0)
    def _():
        zbuf[...] = jnp.zeros(zbuf.shape, zbuf.dtype)
        zrows = zbuf.shape[0]

        def clear_tile(row0):
            for c in range(group_tm // zrows):
                start = pl.multiple_of(row0 + c * zrows, zrows)
                pltpu.make_async_copy(zbuf, xs_hbm.at[pl.ds(start, zrows)], zsem).start()
            for c in range(group_tm // zrows):
                pltpu.make_async_copy(zbuf, xs_hbm.at[pl.ds(0, zrows)], zsem).wait()

        for e in range(N_EXPERTS):
            end = ends_ref[e]
            prev_end = ends_ref[e - 1] if e else 0

            @pl.when(end > prev_end)
            def _():
                clear_tile(end - group_tm)

        for k in range(N_EXPERTS):
            tail = ends_ref[N_EXPERTS - 1] + k * group_tm

            @pl.when(tail < xs_hbm.shape[0])
            def _():
                clear_tile(tail)

    def issue(t, carry):
        src = x_ref.at[pl.ds(t, 1)]
        pltpu.make_async_copy(src, xs_hbm.at[pl.ds(p1_ref[base + t], 1)], sem).start()
        pltpu.make_async_copy(src, xs_hbm.at[pl.ds(p2_ref[base + t], 1)], sem).start()
        return carry

    lax.fori_loop(0, tm, issue, 0, unroll=DMA_ISSUE_UNROLL)
    for _ in range(TOP_K):
        pltpu.make_async_copy(x_ref, xs_hbm.at[pl.ds(0, tm)], sem).wait()


def _dispatch(x, p1, p2, ends, n_rows):
    t, d = x.shape
    tm = _tile(t, DISPATCH_TM)
    grid_spec = pltpu.PrefetchScalarGridSpec(
        num_scalar_prefetch=3,
        grid=(t // tm,),
        in_specs=[pl.BlockSpec((tm, d), lambda i, a, c, e: (i, 0))],
        out_specs=pl.BlockSpec(memory_space=pl.ANY),
        scratch_shapes=[pltpu.VMEM((ZERO_ROWS, d), x.dtype), pltpu.SemaphoreType.DMA(()),
                        pltpu.SemaphoreType.DMA(())])
    return pl.pallas_call(
        functools.partial(_dispatch_kernel, tm=tm, group_tm=MOE_TM),
        grid_spec=grid_spec,
        out_shape=jax.ShapeDtypeStruct((n_rows, d), x.dtype),
        compiler_params=_params("arbitrary"),
    )(p1, p2, ends, x)


def _moe_expert_kernel(te_ref, last_ref, xs_ref, w1_ref, w3_ref, w2_ref, y_ref, xb_ref):
    i = pl.program_id(0)
    j = pl.program_id(1)
    active = i <= last_ref[0]

    @pl.when(j == 0)
    def _():
        y_ref[...] = jnp.zeros(y_ref.shape, F32)

    @pl.when(active)
    def _():
        @pl.when(j == 0)
        def _():
            xb_ref[...] = xs_ref[...].astype(BF16)

        _swiglu_accumulate(y_ref, xb_ref[...], w1_ref[...].astype(BF16), w3_ref[...].astype(BF16),
                           w2_ref[...].astype(BF16))


def _moe_experts(xs, tile_expert, last_tile, w1, w3, w2, layer):
    p, d = xs.shape
    f = w1.shape[3]
    tm, tf = MOE_TM, _tile(f, 256)
    grid_spec = pltpu.PrefetchScalarGridSpec(
        num_scalar_prefetch=2,
        grid=(p // tm, f // tf),
        in_specs=[pl.BlockSpec((tm, d), lambda i, j, te, ac: (jnp.minimum(i, ac[0]), 0)),
                  pl.BlockSpec((None, None, d, tf), lambda i, j, te, ac: (layer, te[i], 0, j)),
                  pl.BlockSpec((None, None, d, tf), lambda i, j, te, ac: (layer, te[i], 0, j)),
                  pl.BlockSpec((None, None, tf, d), lambda i, j, te, ac: (layer, te[i], j, 0))],
        out_specs=pl.BlockSpec((tm, d), lambda i, j, te, ac: (i, 0)),
        scratch_shapes=[pltpu.VMEM((tm, d), BF16)])
    return pl.pallas_call(
        _moe_expert_kernel,
        grid_spec=grid_spec,
        out_shape=jax.ShapeDtypeStruct((p, d), F32),
        compiler_params=_params("arbitrary", "arbitrary"),
    )(tile_expert, last_tile, xs, w1, w3, w2)


def _combine_ln_kernel(p1_ref, p2_ref, x_ref, gt_ref, g_ref, b_ref, ys_hbm, *refs, tm, n_out, bounds):
    out_refs = refs[:n_out]
    buf, sem = refs[n_out:]
    i = pl.program_id(0)
    n = pl.num_programs(0)

    def issue(tile, slot):
        base = tile * tm

        def body(t, carry):
            pltpu.make_async_copy(ys_hbm.at[pl.ds(p1_ref[base + t], 1)],
                                  buf.at[slot, pl.ds(t, 1)], sem.at[slot]).start()
            pltpu.make_async_copy(ys_hbm.at[pl.ds(p2_ref[base + t], 1)],
                                  buf.at[slot, pl.ds(tm + t, 1)], sem.at[slot]).start()
            return carry

        lax.fori_loop(0, tm, body, 0, unroll=DMA_ISSUE_UNROLL)

    @pl.when(i == 0)
    def _():
        issue(0, 0)

    @pl.when(i + 1 < n)
    def _():
        issue(i + 1, (i + 1) % 2)

    slot = i % 2
    pltpu.make_async_copy(ys_hbm.at[pl.ds(0, 2 * tm)], buf.at[slot], sem.at[slot]).wait()
    g1 = gt_ref[:, 0:1]
    g2 = gt_ref[:, 1:2]
    f = g1 * buf[slot, pl.ds(0, tm), :] + g2 * buf[slot, pl.ds(tm, tm), :]
    y = _layer_norm(DN_ALPHA * x_ref[...] + f, g_ref[...], b_ref[...])
    if bounds is None:
        y_ref, yb_ref = out_refs
        y_ref[...] = y
        yb_ref[...] = y.astype(BF16)
    else:
        edges = (0,) + bounds + (None,)
        for gi, o_ref in enumerate(out_refs):
            lo, hi = edges[gi], edges[gi + 1]
            in_group = (i >= lo) if hi is None else ((i >= lo) & (i < hi))

            @pl.when(in_group)
            def _(o_ref=o_ref):
                o_ref[...] = y


def _combine_ln(x, ys, p1, p2, gates, g, b, final_group_rows=None):
    t, d = x.shape
    rows = [t] if final_group_rows is None else list(final_group_rows)
    tm = _tile(math.gcd(*rows) if len(rows) > 1 else t, COMBINE_TM)
    row = lambda i, a, c: (i, 0)
    const = lambda i, a, c: (0, 0)
    if final_group_rows is None:
        bounds = None
        out_specs = [pl.BlockSpec((tm, d), row), pl.BlockSpec((tm, d), row)]
        out_shape = [jax.ShapeDtypeStruct((t, d), F32), jax.ShapeDtypeStruct((t, d), BF16)]
    else:
        starts = np.cumsum([0] + [r // tm for r in rows])
        bounds = tuple(int(s) for s in starts[1:-1])
        out_specs, out_shape = [], []
        for gi, r in enumerate(rows):
            lo, n = int(starts[gi]), int(starts[gi + 1] - starts[gi])
            out_specs.append(pl.BlockSpec((tm, d), lambda i, a, c, lo=lo, n=n: (jnp.clip(i - lo, 0, n - 1), 0)))
            out_shape.append(jax.ShapeDtypeStruct((r, d), F32))
    grid_spec = pltpu.PrefetchScalarGridSpec(
        num_scalar_prefetch=2,
        grid=(t // tm,),
        in_specs=[pl.BlockSpec((tm, d), row), pl.BlockSpec((tm, LANES), row),
                  pl.BlockSpec((1, d), const), pl.BlockSpec((1, d), const),
                  pl.BlockSpec(memory_space=pl.ANY)],
        out_specs=out_specs,
        scratch_shapes=[pltpu.VMEM((2, 2 * tm, d), F32), pltpu.SemaphoreType.DMA((2,))])
    return pl.pallas_call(
        functools.partial(_combine_ln_kernel, tm=tm, n_out=len(out_specs), bounds=bounds),
        grid_spec=grid_spec,
        out_shape=out_shape,
        compiler_params=_params("arbitrary"),
    )(p1, p2, x, gates, g, b, ys)


def _moe_routing(r, tm):
    t = r.shape[0]
    e = N_EXPERTS
    sel = r[:, :e].astype(jnp.int32)
    i1 = r[:, e].astype(jnp.int32)
    i2 = r[:, e + 1].astype(jnp.int32)
    cnt = jnp.cumsum(sel, axis=0)
    rank = cnt - sel
    padded = ((cnt[-1] + tm - 1) // tm) * tm
    ends = jnp.cumsum(padded)
    pos = (ends - padded)[None, :] + rank
    lane = jnp.arange(e, dtype=jnp.int32)[None, :]
    p1 = jnp.sum(jnp.where(lane == i1[:, None], pos, 0), axis=1).astype(jnp.int32)
    p2 = jnp.sum(jnp.where(lane == i2[:, None], pos, 0), axis=1).astype(jnp.int32)
    n_rows = TOP_K * t + e * tm
    tile_start = jnp.arange(n_rows // tm, dtype=jnp.int32) * tm
    tile_expert = jnp.sum((tile_start[:, None] >= ends[None, :]).astype(jnp.int32), axis=1)
    tile_expert = jnp.minimum(tile_expert, e - 1).astype(jnp.int32)
    last_tile = (ends[-1:] // tm - 1).astype(jnp.int32)
    return p1, p2, ends.astype(jnp.int32), tile_expert, last_tile, n_rows


def _deinterleave_perm():
    half = HEAD_DIM // 2
    return np.concatenate([np.arange(half) * 2, np.arange(half) * 2 + 1])


def _rope_tables(groups):
    pos = np.concatenate([np.tile(np.arange(seq), batch) for batch, seq in groups])
    pos = jnp.asarray(pos, jnp.int32)
    row = (pos // GRID_W).astype(F32)
    col = (pos % GRID_W).astype(F32)
    axis_dim = HEAD_DIM // 2
    inv = ROPE_THETA ** (-jnp.arange(0, axis_dim, 2, dtype=F32) / axis_dim)
    ang = jnp.concatenate([row[:, None] * inv, col[:, None] * inv], axis=-1)
    cos, sin = jnp.cos(ang), jnp.sin(ang)
    return jnp.concatenate([cos, cos], axis=-1), jnp.concatenate([-sin, sin], axis=-1)


def _group_rows(groups):
    out, row0 = [], 0
    for batch, seq in groups:
        out.append((row0, batch, seq))
        row0 += batch * seq
    return out


def _prep_w_in_even(w_in):
    perm = _deinterleave_perm()
    n_qk = N_HEADS_B + N_KV_B
    c0 = 3 * N_HEADS_A * HEAD_DIM
    pmat = np.zeros((HEAD_DIM, HEAD_DIM), np.float32)
    pmat[perm, np.arange(HEAD_DIM)] = 1.0
    w_in_b = w_in.astype(BF16)
    lead = w_in.shape[:2]
    w_qk = w_in_b[:, :, c0:c0 + n_qk * HEAD_DIM].reshape(*lead, n_qk, HEAD_DIM)
    w_qk = jnp.einsum('ldhk,kn->ldhn', w_qk, jnp.asarray(pmat, BF16), preferred_element_type=F32)
    return jnp.concatenate([w_in_b[:, :, :c0], w_qk.astype(BF16).reshape(*lead, n_qk * HEAD_DIM),
                            w_in_b[:, :, c0 + n_qk * HEAD_DIM:]], axis=2)


def _even_layer(x, xb, groups, layer, w_in_b, qk_gain, w_out_b, w1_b, w3_b, w2_b, ln_g, ln_b, rope, dil):
    w_a = N_HEADS_A * HEAD_DIM
    w_bq = N_HEADS_B * HEAD_DIM
    w_bkv = N_KV_B * HEAD_DIM
    perm = _deinterleave_perm()
    col_scale = jnp.concatenate([jnp.full((w_a,), SCALE * LOG2E, F32),
                                 jnp.ones((w_in_b.shape[2] - w_a,), F32)])[None, :]
    proj = _project(xb, w_in_b, layer, col_scale)

    n_b = N_HEADS_B + N_KV_B
    gains = jnp.concatenate([jnp.tile(qk_gain[0][perm][None], (N_HEADS_B, 1)),
                             jnp.tile(qk_gain[1][perm][None], (N_KV_B, 1))]).astype(F32)
    scales = jnp.concatenate([jnp.full((N_HEADS_B, HEAD_DIM), SCALE * LOG2E, F32),
                              jnp.ones((N_KV_B, HEAD_DIM), F32)])
    qk = _qk_prep(proj, gains.reshape(2, n_b // 2, HEAD_DIM), scales.reshape(2, n_b // 2, HEAD_DIM),
                  rope[0], rope[1], 3 * w_a)

    table, halo_blks, win_blks = dil
    oa, ob = [], []
    v_blk0 = (3 * w_a + w_bq + w_bkv) // HEAD_DIM
    for row0, batch, seq in _group_rows(groups):
        oa.append(_win_attn(proj, table, row0=row0, batch=batch, seq=seq, n_heads=N_HEADS_A,
                            q_blk0=0, k_blk0=N_HEADS_A, v_blk0=2 * N_HEADS_A,
                            halo_blks=halo_blks, win_blks=win_blks, hp=HEADS_PER_STEP_A))
        ob.append(_gqa(qk, proj, row0=row0, batch=batch, seq=seq, v_blk0=v_blk0))
    (x,) = _out_ln(x, [oa, ob], w_out_b, layer, ln_g[0][None], ln_b[0][None])
    return _ffn_ln(x, w1_b, w3_b, w2_b, layer, ln_g[1][None], ln_b[1][None])


def _odd_layer(x, xb, groups, layer, w_in_b, rpb, w_out_b, w_router, b_router, w1_b, w3_b, w2_b, ln_g, ln_b,
               final):
    w_c = N_HEADS_C * HEAD_DIM
    col_scale = jnp.concatenate([jnp.full((w_c,), SCALE * LOG2E, F32), jnp.ones((2 * w_c,), F32)])[None, :]
    proj = _project(xb, w_in_b, layer, col_scale)
    table, halo_blks, win_blks = _natten_table(rpb)
    o = []
    for row0, batch, seq in _group_rows(groups):
        o.append(_win_attn(proj, table, row0=row0, batch=batch, seq=seq, n_heads=N_HEADS_C,
                           q_blk0=0, k_blk0=N_HEADS_C, v_blk0=2 * N_HEADS_C,
                           halo_blks=halo_blks, win_blks=win_blks, hp=HEADS_PER_STEP_C))

    wr = jnp.pad(w_router.astype(F32), ((0, 0), (0, ROUTER_LANES - N_EXPERTS)))
    wr_hi = wr.astype(BF16)
    wr_lo = (wr - wr_hi.astype(F32)).astype(BF16)
    br = jnp.concatenate([b_router.astype(F32), jnp.full((ROUTER_LANES - N_EXPERTS,), NEG, F32)])[None, :]
    x, r = _out_ln(x, [o], w_out_b, layer, ln_g[0][None], ln_b[0][None], router=(wr_hi, wr_lo, br))

    p1, p2, ends, tile_expert, last_tile, n_rows = _moe_routing(r, MOE_TM)
    xs = _dispatch(x, p1, p2, ends, n_rows)
    ys = _moe_experts(xs, tile_expert, last_tile, w1_b, w3_b, w2_b, layer)
    gates = jnp.pad(r[:, N_EXPERTS + 2:N_EXPERTS + 4], ((0, 0), (0, LANES - 2)))
    final_rows = [batch * seq for batch, seq in groups] if final else None
    return _combine_ln(x, ys, p1, p2, gates, ln_g[1][None], ln_b[1][None], final_group_rows=final_rows)


def kernel(x_prompt, x_sample, ln_g, ln_b, w_in_even, qk_gain_b, w_out_even, ffn_w1, ffn_w3, ffn_w2,
           w_in_odd, rpb, w_out_odd, w_router, b_router, moe_w1, moe_w3, moe_w2):
    d = x_prompt.shape[-1]
    groups = [(x_prompt.shape[0], x_prompt.shape[1]), (x_sample.shape[0], x_sample.shape[1])]
    x = jnp.concatenate([x_prompt.reshape(-1, d), x_sample.reshape(-1, d)], axis=0)
    xb = x.astype(BF16)
    rope = _rope_tables(groups)
    dil = _dilated_table()
    even_w = (_prep_w_in_even(w_in_even), w_out_even.astype(BF16),
              ffn_w1.astype(BF16), ffn_w3.astype(BF16), ffn_w2.astype(BF16))
    odd_w = (w_in_odd.astype(BF16), w_out_odd.astype(BF16), moe_w1, moe_w3, moe_w2)
    depth = ln_g.shape[0]
    for i in range(depth):
        j = i // 2
        if i % 2 == 0:
            x, xb = _even_layer(x, xb, groups, j, even_w[0], qk_gain_b[j], even_w[1], *even_w[2:],
                                ln_g[i], ln_b[i], rope, dil)
        else:
            x, xb = _odd_layer(x, xb, groups, j, odd_w[0], rpb[j], odd_w[1], w_router[j], b_router[j],
                               *odd_w[2:], ln_g[i], ln_b[i], final=(i == depth - 1))
    if depth % 2 == 0:
        return (x.reshape(x_prompt.shape), xb.reshape(x_sample.shape))
    n_p = x_prompt.shape[0] * x_prompt.shape[1]
    return (x[:n_p].reshape(x_prompt.shape), x[n_p:].reshape(x_sample.shape))
```

```python
import functools
import math

import numpy as np
import jax
import jax.numpy as jnp
from jax import lax
from jax.experimental import pallas as pl
from jax.experimental.pallas import tpu as pltpu

HEAD_DIM = 128
GRID_W = 64
N_HEADS_A = 6
DILATED_BRANCHES = ((128, 1), (512, 4), (2048, 16))
N_HEADS_B = 10
N_KV_B = 2
N_HEADS_C = 16
NA_ROWS = 8
NA_COLS = 16
N_EXPERTS = 8
TOP_K = 2
DEPTH = 4
ROPE_THETA = 10000.0
LN_EPS = 1e-5
QK_EPS = 1e-6
NEG = -1e30
SCALE = HEAD_DIM ** -0.5
LOG2E = math.log2(math.e)
DN_ALPHA = (2 * DEPTH) ** 0.25

V7X_VMEM_BYTES = 64 * 2 ** 20
VMEM_LIMIT = V7X_VMEM_BYTES - 8 * 2 ** 20
LANES = 128
ATTN_TQ = 256
ATTN_ROWS = 128
HEADS_PER_STEP_A = 3
HEADS_PER_STEP_C = 4
ROUTER_LANES = LANES

F32 = jnp.float32
BF16 = jnp.bfloat16


def _params(*sem):
    return pltpu.CompilerParams(dimension_semantics=sem, vmem_limit_bytes=VMEM_LIMIT)


def _tile(n, pref):
    if n <= pref:
        return n
    t = (pref // LANES) * LANES
    while t >= LANES:
        if n % t == 0:
            return t
        t -= LANES
    return n


def _lane_tiles(s):
    return [s[:, t * LANES:(t + 1) * LANES] for t in range(s.shape[1] // LANES)]


def _with_ones(v):
    return jnp.concatenate([v, jnp.ones_like(v)], axis=1)


def _proj_kernel(x_ref, w_ref, cs_ref, o_ref):
    acc = jnp.dot(x_ref[...], w_ref[...], preferred_element_type=F32)
    o_ref[...] = (acc * cs_ref[...]).astype(o_ref.dtype)


def _project(xb, w_stack, layer, col_scale):
    t, k = xb.shape
    n = w_stack.shape[2]
    tm, tn = _tile(t, 1024), _tile(n, 2048)
    return pl.pallas_call(
        _proj_kernel,
        grid=(t // tm, n // tn),
        in_specs=[pl.BlockSpec((tm, k), lambda i, j: (i, 0)),
                  pl.BlockSpec((None, k, tn), lambda i, j: (layer, 0, j)),
                  pl.BlockSpec((1, tn), lambda i, j: (0, j))],
        out_specs=pl.BlockSpec((tm, tn), lambda i, j: (i, j)),
        out_shape=jax.ShapeDtypeStruct((t, n), BF16),
        compiler_params=_params("arbitrary", "arbitrary"),
    )(xb, w_stack, col_scale)


def _layer_norm(z, g, b):
    mu = jnp.mean(z, axis=-1, keepdims=True)
    zc = z - mu
    var = jnp.mean(zc * zc, axis=-1, keepdims=True)
    return zc * lax.rsqrt(var + LN_EPS) * g + b


def _qk_prep_kernel(p_ref, g_ref, sc_ref, cos_ref, sin_ref, o_ref, *, heads):
    c = cos_ref[...]
    s = sin_ref[...]
    for r in range(heads):
        x = p_ref[:, r * HEAD_DIM:(r + 1) * HEAD_DIM].astype(F32)
        ms = jnp.mean(x * x, axis=-1, keepdims=True)
        xn = x * lax.rsqrt(ms + QK_EPS) * g_ref[0, r:r + 1, :]
        y = xn * c + pltpu.roll(xn, HEAD_DIM // 2, 1) * s
        o_ref[:, r * HEAD_DIM:(r + 1) * HEAD_DIM] = (y * sc_ref[0, r:r + 1, :]).astype(o_ref.dtype)


def _qk_prep(proj, gains, scales, cos_t, sin_t, col0):
    t = proj.shape[0]
    n_heads = N_HEADS_B + N_KV_B
    half = n_heads // 2
    wblk = half * HEAD_DIM
    assert col0 % wblk == 0
    tm = _tile(t, 512)
    return pl.pallas_call(
        functools.partial(_qk_prep_kernel, heads=half),
        grid=(t // tm, 2),
        in_specs=[pl.BlockSpec((tm, wblk), lambda i, j: (i, col0 // wblk + j)),
                  pl.BlockSpec((1, half, HEAD_DIM), lambda i, j: (j, 0, 0)),
                  pl.BlockSpec((1, half, HEAD_DIM), lambda i, j: (j, 0, 0)),
                  pl.BlockSpec((tm, HEAD_DIM), lambda i, j: (i, 0)),
                  pl.BlockSpec((tm, HEAD_DIM), lambda i, j: (i, 0))],
        out_specs=pl.BlockSpec((tm, wblk), lambda i, j: (i, j)),
        out_shape=jax.ShapeDtypeStruct((t, n_heads * HEAD_DIM), BF16),
        compiler_params=_params("arbitrary", "arbitrary"),
    )(proj, gains, scales, cos_t, sin_t)


def _window_start_blk(i, halo_blks, win_blks, n_blks):
    return jnp.clip(i - halo_blks, 0, n_blks - win_blks)


def _win_attn_kernel(q_ref, k_ref, v_ref, tb_ref, o_ref, *, halo_blks, win_blks, n_blks, axis):
    i = pl.program_id(axis)
    tq = q_ref.shape[0]
    w = win_blks * tq
    start = pl.multiple_of(_window_start_blk(i, halo_blks, win_blks, n_blks) * tq, tq)
    hp = tb_ref.shape[0]
    for hd in range(hp):
        cols = slice(hd * HEAD_DIM, (hd + 1) * HEAD_DIM)
        k = k_ref[pl.ds(start, w), cols]
        v2 = _with_ones(v_ref[pl.ds(start, w), cols])
        for rb in range(tq // ATTN_ROWS):
            sl = slice(rb * ATTN_ROWS, (rb + 1) * ATTN_ROWS)
            s = lax.dot_general(q_ref[sl, cols], k, (((1,), (1,)), ((), ())), preferred_element_type=F32)
            tiles = _lane_tiles(s + tb_ref[hd, 0, sl, :])
            m = jnp.max(functools.reduce(jnp.maximum, tiles), axis=-1, keepdims=True)
            p = jnp.concatenate([jnp.exp2(t - m) for t in tiles], axis=1).astype(BF16)
            pv = jnp.dot(p, v2, preferred_element_type=F32)
            o_ref[sl, cols] = (pv[:, :HEAD_DIM] / pv[:, HEAD_DIM:]).astype(o_ref.dtype)


def _win_attn(src, table, *, row0, batch, seq, n_heads, q_blk0, k_blk0, v_blk0, halo_blks, win_blks, hp):
    tq = ATTN_TQ
    n_blks = seq // tq
    assert seq % tq == 0 and n_blks >= win_blks and row0 % seq == 0
    assert n_heads % hp == 0 and q_blk0 % hp == 0 and k_blk0 % hp == 0 and v_blk0 % hp == 0
    seq_blk0 = row0 // seq
    qrow0 = row0 // tq
    wh = hp * HEAD_DIM
    qb, kb, vb = q_blk0 // hp, k_blk0 // hp, v_blk0 // hp

    def variant(i):
        return i - _window_start_blk(i, halo_blks, win_blks, n_blks)

    kern = functools.partial(_win_attn_kernel, halo_blks=halo_blks, win_blks=win_blks,
                             n_blks=n_blks, axis=2)
    return pl.pallas_call(
        kern,
        grid=(batch, n_heads // hp, n_blks),
        in_specs=[pl.BlockSpec((tq, wh), lambda b, h, i: (qrow0 + b * n_blks + i, qb + h)),
                  pl.BlockSpec((seq, wh), lambda b, h, i: (seq_blk0 + b, kb + h), pipeline_mode=pl.Buffered(1)),
                  pl.BlockSpec((seq, wh), lambda b, h, i: (seq_blk0 + b, vb + h), pipeline_mode=pl.Buffered(1)),
                  pl.BlockSpec((hp, 1, tq, win_blks * tq), lambda b, h, i: (h, variant(i), 0, 0))],
        out_specs=pl.BlockSpec((tq, wh), lambda b, h, i: (b * n_blks + i, h)),
        out_shape=jax.ShapeDtypeStruct((batch * seq, n_heads * HEAD_DIM), BF16),
        compiler_params=_params("arbitrary", "arbitrary", "arbitrary"),
    )(src, src, src, table)


def _dilated_table():
    tq = ATTN_TQ
    halo = max(w // 2 for w, _ in DILATED_BRANCHES)
    halo_blks = halo // tq
    win_blks = 2 * halo_blks + 1
    v = jnp.arange(win_blks)[:, None, None]
    r = jnp.arange(tq)[None, :, None]
    c = jnp.arange(win_blks * tq)[None, None, :]
    delta = c - v * tq - r
    ad = jnp.abs(delta)
    cnt = jnp.zeros(delta.shape, F32)
    for window, dil in DILATED_BRANCHES:
        cnt = cnt + ((ad <= window // 2) & (delta % dil == 0)).astype(F32)
    slopes = 2.0 ** (-8.0 * jnp.arange(1, N_HEADS_A + 1, dtype=F32) / N_HEADS_A)
    bias = -slopes[:, None, None, None] * ad.astype(F32)[None] + jnp.log(jnp.maximum(cnt, 1.0))[None]
    table = jnp.where(cnt[None] > 0, bias * LOG2E, NEG)
    return table, halo_blks, win_blks


def _natten_table(rpb):
    tq = ATTN_TQ
    rows_per_tile = tq // GRID_W
    halo_blks = 1
    win_blks = 3
    assert NA_ROWS // 2 == rows_per_tile
    win_rows = win_blks * rows_per_tile
    qt = np.arange(win_blks)[:, None] * tq + np.arange(tq)[None, :]
    rq, qc = qt // GRID_W, qt % GRID_W
    kt = np.arange(win_blks * tq)
    rk, kc = kt // GRID_W, kt % GRID_W
    rs = np.clip(rq - NA_ROWS // 2, 0, win_rows - NA_ROWS)
    cs = np.clip(qc - NA_COLS // 2, 0, GRID_W - NA_COLS)
    valid = ((rk[None, None, :] >= rs[..., None]) & (rk[None, None, :] < rs[..., None] + NA_ROWS)
             & (kc[None, None, :] >= cs[..., None]) & (kc[None, None, :] < cs[..., None] + NA_COLS))
    n_ro, n_co = 2 * NA_ROWS - 1, 2 * NA_COLS - 1
    rows = np.arange(win_rows)
    ri = np.clip(rows[None, :] - rows[:, None] + NA_ROWS - 1, 0, n_ro - 1)
    cols = np.arange(GRID_W)
    ci = np.clip(cols[None, :] - cols[:, None] + NA_COLS - 1, 0, n_co - 1)
    onehot = (np.arange(n_co)[:, None, None] == ci[None]).astype(np.float32).reshape(n_co, GRID_W * GRID_W)
    t1 = rpb.astype(F32)[:, ri.reshape(-1), :]
    hi = t1.astype(BF16)
    mid = (t1 - hi.astype(F32)).astype(BF16)
    lo = (t1 - hi.astype(F32) - mid.astype(F32)).astype(BF16)
    oh = jnp.asarray(onehot, BF16)
    t2 = sum(jnp.einsum('hpb,bq->hpq', part, oh, preferred_element_type=F32) for part in (hi, mid, lo))
    t2 = t2.reshape(-1, win_rows, win_rows, GRID_W, GRID_W).transpose(0, 1, 3, 2, 4)
    bias = t2.reshape(-1, win_blks, tq, win_blks * tq)
    table = jnp.where(jnp.asarray(valid)[None], bias * LOG2E, NEG)
    return table, halo_blks, win_blks


GQA_TQ = 1024
GQA_TK = 2048
GQA_ROWS = 256


def _gqa_kernel(q_ref, k_ref, v_ref, o_ref, qs_ref, acc_ref, m_ref, *, rep, tk, rows):
    tq = q_ref.shape[0]
    seq = k_ref.shape[0]
    for r in range(rep):
        qs_ref[r * tq:(r + 1) * tq, :] = q_ref[:, r * HEAD_DIM:(r + 1) * HEAD_DIM]
    m_ref[...] = jnp.full(m_ref.shape, -jnp.inf, F32)
    acc_ref[...] = jnp.zeros(acc_ref.shape, F32)

    def body(c, carry):
        off = pl.multiple_of(c * tk, tk)
        k = k_ref[pl.ds(off, tk), :]
        v2 = _with_ones(v_ref[pl.ds(off, tk), :])
        for rb in range(rep * tq // rows):
            sl = slice(rb * rows, (rb + 1) * rows)
            s = lax.dot_general(qs_ref[sl, :], k, (((1,), (1,)), ((), ())), preferred_element_type=F32)
            tiles = _lane_tiles(s)
            m_prev = m_ref[sl, :]
            m_new = jnp.maximum(m_prev, jnp.max(functools.reduce(jnp.maximum, tiles), axis=-1, keepdims=True))
            alpha = jnp.exp2(m_prev - m_new)
            p = jnp.concatenate([jnp.exp2(t - m_new) for t in tiles], axis=1).astype(BF16)
            pv = jnp.dot(p, v2, preferred_element_type=F32)
            acc_ref[sl, :] = jnp.concatenate([alpha, alpha], axis=1) * acc_ref[sl, :] + pv
            m_ref[sl, :] = m_new
        return carry

    lax.fori_loop(0, seq // tk, body, 0)
    acc = acc_ref[...]
    o = acc[:, :HEAD_DIM] / acc[:, HEAD_DIM:]
    for r in range(rep):
        o_ref[:, r * HEAD_DIM:(r + 1) * HEAD_DIM] = o[r * tq:(r + 1) * tq].astype(o_ref.dtype)


def _gqa(qk, proj, *, row0, batch, seq, v_blk0):
    rep = N_HEADS_B // N_KV_B
    tq = GQA_TQ
    tk = _tile(seq, GQA_TK)
    n_blks = seq // tq
    assert seq % tq == 0 and row0 % seq == 0 and (rep * tq) % GQA_ROWS == 0
    seq_blk0 = row0 // seq
    qrow0 = row0 // tq
    wq = rep * HEAD_DIM
    return pl.pallas_call(
        functools.partial(_gqa_kernel, rep=rep, tk=tk, rows=GQA_ROWS),
        grid=(batch, N_KV_B, n_blks),
        in_specs=[pl.BlockSpec((tq, wq), lambda b, g, i: (qrow0 + b * n_blks + i, g)),
                  pl.BlockSpec((seq, HEAD_DIM), lambda b, g, i: (seq_blk0 + b, N_HEADS_B + g)),
                  pl.BlockSpec((seq, HEAD_DIM), lambda b, g, i: (seq_blk0 + b, v_blk0 + g))],
        out_specs=pl.BlockSpec((tq, wq), lambda b, g, i: (b * n_blks + i, g)),
        out_shape=jax.ShapeDtypeStruct((batch * seq, N_HEADS_B * HEAD_DIM), BF16),
        scratch_shapes=[pltpu.VMEM((rep * tq, HEAD_DIM), BF16),
                        pltpu.VMEM((rep * tq, 2 * HEAD_DIM), F32),
                        pltpu.VMEM((rep * tq, HEAD_DIM), F32)],
        compiler_params=_params("arbitrary", "arbitrary", "arbitrary"),
    )(qk, qk, proj)


def _out_ln_kernel(x_ref, *refs, widths, bounds, router):
    i = pl.program_id(0)
    n_groups = len(bounds) + 1
    n_act = len(widths)
    w_ref, g_ref, b_ref = refs[n_act * n_groups:n_act * n_groups + 3]
    rest = refs[n_act * n_groups + 3:]
    h = None
    row0 = 0
    for a, width in enumerate(widths):
        grp = refs[a * n_groups:(a + 1) * n_groups]
        val = grp[-1][...]
        for gi in reversed(range(n_groups - 1)):
            val = jnp.where(i < bounds[gi], grp[gi][...], val)
        part = jnp.dot(val, w_ref[row0:row0 + width, :], preferred_element_type=F32)
        h = part if h is None else h + part
        row0 += width
    y = _layer_norm(DN_ALPHA * x_ref[...] + h, g_ref[...], b_ref[...])
    if not router:
        (y_ref,) = rest
        y_ref[...] = y
        return
    wr_hi_ref, wr_lo_ref, br_ref, y_ref, r_ref = rest
    y_hi = y.astype(BF16)
    y_lo = (y - y_hi.astype(F32)).astype(BF16)
    y_ref[...] = y
    logits = (jnp.dot(y_hi, wr_hi_ref[...], preferred_element_type=F32)
              + jnp.dot(y_lo, wr_hi_ref[...], preferred_element_type=F32)
              + jnp.dot(y_hi, wr_lo_ref[...], preferred_element_type=F32)) + br_ref[...]
    lane = lax.broadcasted_iota(jnp.int32, logits.shape, 1).astype(F32)
    m1 = jnp.max(logits, axis=-1, keepdims=True)
    i1 = jnp.min(jnp.where(logits == m1, lane, float(ROUTER_LANES)), axis=-1, keepdims=True)
    rest_l = jnp.where(lane == i1, -jnp.inf, logits)
    m2 = jnp.max(rest_l, axis=-1, keepdims=True)
    i2 = jnp.min(jnp.where(rest_l == m2, lane, float(ROUTER_LANES)), axis=-1, keepdims=True)
    e2 = jnp.exp(m2 - m1)
    den = 1.0 + e2
    g1 = 1.0 / den
    g2 = e2 / den
    sel = ((lane == i1) | (lane == i2)).astype(F32)
    e = N_EXPERTS
    out = jnp.where(lane < e, sel, 0.0)
    out = jnp.where(lane == e, i1, out)
    out = jnp.where(lane == e + 1, i2, out)
    out = jnp.where(lane == e + 2, g1, out)
    out = jnp.where(lane == e + 3, g2, out)
    r_ref[...] = out


def _out_ln(x, acts, w_stack, layer, g, b, router=None):
    t, d = x.shape
    group_rows = [a.shape[0] for a in acts[0]]
    tm = _tile(math.gcd(*group_rows), 512)
    starts = np.cumsum([0] + [r // tm for r in group_rows])
    row = lambda i: (i, 0)
    const = lambda i: (0, 0)
    in_specs = [pl.BlockSpec((tm, d), row)]
    args = [x]
    for per_group in acts:
        for gi, a in enumerate(per_group):
            lo, n = int(starts[gi]), int(starts[gi + 1] - starts[gi])
            in_specs.append(pl.BlockSpec((tm, a.shape[1]),
                                         lambda i, lo=lo, n=n: (jnp.clip(i - lo, 0, n - 1), 0)))
            args.append(a)
    in_specs += [pl.BlockSpec((None,) + w_stack.shape[1:], lambda i: (layer, 0, 0), pipeline_mode=pl.Buffered(1)),
                 pl.BlockSpec((1, d), const), pl.BlockSpec((1, d), const)]
    args += [w_stack, g, b]
    out_specs = [pl.BlockSpec((tm, d), row)]
    out_shape = [jax.ShapeDtypeStruct((t, d), F32)]
    widths = tuple(per_group[0].shape[1] for per_group in acts)
    bounds = tuple(int(s) for s in starts[1:-1])
    if router is not None:
        in_specs += [pl.BlockSpec((d, ROUTER_LANES), const), pl.BlockSpec((d, ROUTER_LANES), const),
                     pl.BlockSpec((1, ROUTER_LANES), const)]
        out_specs.append(pl.BlockSpec((tm, ROUTER_LANES), row))
        out_shape.append(jax.ShapeDtypeStruct((t, ROUTER_LANES), F32))
        args += list(router)
    return pl.pallas_call(
        functools.partial(_out_ln_kernel, widths=widths, bounds=bounds, router=router is not None),
        grid=(t // tm,),
        in_specs=in_specs, out_specs=out_specs, out_shape=out_shape,
        compiler_params=_params("arbitrary"),
    )(*args)


def _swiglu_accumulate(acc_ref, xb, w1, w3, w2):
    h1 = jnp.dot(xb, w1, preferred_element_type=F32)
    h3 = jnp.dot(xb, w3, preferred_element_type=F32)
    h = (h1 * (1.0 / (1.0 + jnp.exp(-h1)))) * h3
    acc_ref[...] += jnp.dot(h.astype(BF16), w2, preferred_element_type=F32)


def _ffn_ln_kernel(x_ref, w1_ref, w3_ref, w2_ref, g_ref, b_ref, y_ref, yb_ref, xb_ref):
    j = pl.program_id(1)

    @pl.when(j == 0)
    def _():
        xb_ref[...] = x_ref[...].astype(BF16)
        y_ref[...] = jnp.zeros(y_ref.shape, F32)

    _swiglu_accumulate(y_ref, xb_ref[...], w1_ref[...], w3_ref[...], w2_ref[...])

    @pl.when(j == pl.num_programs(1) - 1)
    def _():
        y = _layer_norm(DN_ALPHA * x_ref[...] + y_ref[...], g_ref[...], b_ref[...])
        y_ref[...] = y
        yb_ref[...] = y.astype(BF16)


FFN_TM = 512


def _ffn_ln(x, w1, w3, w2, layer, g, b):
    t, d = x.shape
    f = w1.shape[2]
    tm, tf = _tile(t, FFN_TM), _tile(f, 512)
    return pl.pallas_call(
        _ffn_ln_kernel,
        grid=(t // tm, f // tf),
        in_specs=[pl.BlockSpec((tm, d), lambda i, j: (i, 0)),
                  pl.BlockSpec((None, d, tf), lambda i, j: (layer, 0, j)),
                  pl.BlockSpec((None, d, tf), lambda i, j: (layer, 0, j)),
                  pl.BlockSpec((None, tf, d), lambda i, j: (layer, j, 0)),
                  pl.BlockSpec((1, d), lambda i, j: (0, 0)),
                  pl.BlockSpec((1, d), lambda i, j: (0, 0))],
        out_specs=[pl.BlockSpec((tm, d), lambda i, j: (i, 0)),
                   pl.BlockSpec((tm, d), lambda i, j: (i, 0))],
        out_shape=[jax.ShapeDtypeStruct((t, d), F32), jax.ShapeDtypeStruct((t, d), BF16)],
        scratch_shapes=[pltpu.VMEM((tm, d), BF16)],
        compiler_params=_params("arbitrary", "arbitrary"),
    )(x, w1, w3, w2, g, b)


MOE_TM = 1024
DISPATCH_TM = 512
COMBINE_TM = 256
DMA_ISSUE_UNROLL = 8
ZERO_ROWS = 256


def _dispatch_kernel(p1_ref, p2_ref, ends_ref, x_ref, xs_hbm, zbuf, sem, zsem, *, tm, group_tm):
    i = pl.program_id(0)
    base = i * tm

    @pl.when(i == 0)
    def _():
        zbuf[...] = jnp.zeros(zbuf.shape, zbuf.dtype)
        zrows = zbuf.shape[0]

        def clear_tile(row0):
            for c in range(group_tm // zrows):
                start = pl.multiple_of(row0 + c * zrows, zrows)
                pltpu.make_async_copy(zbuf, xs_hbm.at[pl.ds(start, zrows)], zsem).start()
            for c in range(group_tm // zrows):
                pltpu.make_async_copy(zbuf, xs_hbm.at[pl.ds(0, zrows)], zsem).wait()

        for e in range(N_EXPERTS):
            end = ends_ref[e]
            prev_end = ends_ref[e - 1] if e else 0

            @pl.when(end > prev_end)
            def _():
                clear_tile(end - group_tm)

        for k in range(N_EXPERTS):
            tail = ends_ref[N_EXPERTS - 1] + k * group_tm

            @pl.when(tail < xs_hbm.shape[0])
            def _():
                clear_tile(tail)

    def issue(t, carry):
        src = x_ref.at[pl.ds(t, 1)]
        pltpu.make_async_copy(src, xs_hbm.at[pl.ds(p1_ref[base + t], 1)], sem).start()
        pltpu.make_async_copy(src, xs_hbm.at[pl.ds(p2_ref[base + t], 1)], sem).start()
        return carry

    lax.fori_loop(0, tm, issue, 0, unroll=DMA_ISSUE_UNROLL)
    for _ in range(TOP_K):
        pltpu.make_async_copy(x_ref, xs_hbm.at[pl.ds(0, tm)], sem).wait()


def _dispatch(x, p1, p2, ends, n_rows):
    t, d = x.shape
    tm = _tile(t, DISPATCH_TM)
    grid_spec = pltpu.PrefetchScalarGridSpec(
        num_scalar_prefetch=3,
        grid=(t // tm,),
        in_specs=[pl.BlockSpec((tm, d), lambda i, a, c, e: (i, 0))],
        out_specs=pl.BlockSpec(memory_space=pl.ANY),
        scratch_shapes=[pltpu.VMEM((ZERO_ROWS, d), x.dtype), pltpu.SemaphoreType.DMA(()),
                        pltpu.SemaphoreType.DMA(())])
    return pl.pallas_call(
        functools.partial(_dispatch_kernel, tm=tm, group_tm=MOE_TM),
        grid_spec=grid_spec,
        out_shape=jax.ShapeDtypeStruct((n_rows, d), x.dtype),
        compiler_params=_params("arbitrary"),
    )(p1, p2, ends, x)


def _moe_expert_kernel(te_ref, last_ref, xs_ref, w1_ref, w3_ref, w2_ref, y_ref, xb_ref):
    i = pl.program_id(0)
    j = pl.program_id(1)
    active = i <= last_ref[0]

    @pl.when(j == 0)
    def _():
        y_ref[...] = jnp.zeros(y_ref.shape, F32)

    @pl.when(active)
    def _():
        @pl.when(j == 0)
        def _():
            xb_ref[...] = xs_ref[...].astype(BF16)

        _swiglu_accumulate(y_ref, xb_ref[...], w1_ref[...].astype(BF16), w3_ref[...].astype(BF16),
                           w2_ref[...].astype(BF16))


def _moe_experts(xs, tile_expert, last_tile, w1, w3, w2, layer):
    p, d = xs.shape
    f = w1.shape[3]
    tm, tf = MOE_TM, _tile(f, 256)
    grid_spec = pltpu.PrefetchScalarGridSpec(
        num_scalar_prefetch=2,
        grid=(p // tm, f // tf),
        in_specs=[pl.BlockSpec((tm, d), lambda i, j, te, ac: (jnp.minimum(i, ac[0]), 0)),
                  pl.BlockSpec((None, None, d, tf), lambda i, j, te, ac: (layer, te[i], 0, j)),
                  pl.BlockSpec((None, None, d, tf), lambda i, j, te, ac: (layer, te[i], 0, j)),
                  pl.BlockSpec((None, None, tf, d), lambda i, j, te, ac: (layer, te[i], j, 0))],
        out_specs=pl.BlockSpec((tm, d), lambda i, j, te, ac: (i, 0)),
        scratch_shapes=[pltpu.VMEM((tm, d), BF16)])
    return pl.pallas_call(
        _moe_expert_kernel,
        grid_spec=grid_spec,
        out_shape=jax.ShapeDtypeStruct((p, d), F32),
        compiler_params=_params("arbitrary", "arbitrary"),
    )(tile_expert, last_tile, xs, w1, w3, w2)


def _combine_ln_kernel(p1_ref, p2_ref, x_ref, gt_ref, g_ref, b_ref, ys_hbm, *refs, tm, n_out, bounds):
    out_refs = refs[:n_out]
    buf, sem = refs[n_out:]
    i = pl.program_id(0)
    n = pl.num_programs(0)

    def issue(tile, slot):
        base = tile * tm

        def body(t, carry):
            pltpu.make_async_copy(ys_hbm.at[pl.ds(p1_ref[base + t], 1)],
                                  buf.at[slot, pl.ds(t, 1)], sem.at[slot]).start()
            pltpu.make_async_copy(ys_hbm.at[pl.ds(p2_ref[base + t], 1)],
                                  buf.at[slot, pl.ds(tm + t, 1)], sem.at[slot]).start()
            return carry

        lax.fori_loop(0, tm, body, 0, unroll=DMA_ISSUE_UNROLL)

    @pl.when(i == 0)
    def _():
        issue(0, 0)

    @pl.when(i + 1 < n)
    def _():
        issue(i + 1, (i + 1) % 2)

    slot = i % 2
    pltpu.make_async_copy(ys_hbm.at[pl.ds(0, 2 * tm)], buf.at[slot], sem.at[slot]).wait()
    g1 = gt_ref[:, 0:1]
    g2 = gt_ref[:, 1:2]
    f = g1 * buf[slot, pl.ds(0, tm), :] + g2 * buf[slot, pl.ds(tm, tm), :]
    y = _layer_norm(DN_ALPHA * x_ref[...] + f, g_ref[...], b_ref[...])
    if bounds is None:
        y_ref, yb_ref = out_refs
        y_ref[...] = y
        yb_ref[...] = y.astype(BF16)
    else:
        edges = (0,) + bounds + (None,)
        for gi, o_ref in enumerate(out_refs):
            lo, hi = edges[gi], edges[gi + 1]
            in_group = (i >= lo) if hi is None else ((i >= lo) & (i < hi))

            @pl.when(in_group)
            def _(o_ref=o_ref):
                o_ref[...] = y


def _combine_ln(x, ys, p1, p2, gates, g, b, final_group_rows=None):
    t, d = x.shape
    rows = [t] if final_group_rows is None else list(final_group_rows)
    tm = _tile(math.gcd(*rows) if len(rows) > 1 else t, COMBINE_TM)
    row = lambda i, a, c: (i, 0)
    const = lambda i, a, c: (0, 0)
    if final_group_rows is None:
        bounds = None
        out_specs = [pl.BlockSpec((tm, d), row), pl.BlockSpec((tm, d), row)]
        out_shape = [jax.ShapeDtypeStruct((t, d), F32), jax.ShapeDtypeStruct((t, d), BF16)]
    else:
        starts = np.cumsum([0] + [r // tm for r in rows])
        bounds = tuple(int(s) for s in starts[1:-1])
        out_specs, out_shape = [], []
        for gi, r in enumerate(rows):
            lo, n = int(starts[gi]), int(starts[gi + 1] - starts[gi])
            out_specs.append(pl.BlockSpec((tm, d), lambda i, a, c, lo=lo, n=n: (jnp.clip(i - lo, 0, n - 1), 0)))
            out_shape.append(jax.ShapeDtypeStruct((r, d), F32))
    grid_spec = pltpu.PrefetchScalarGridSpec(
        num_scalar_prefetch=2,
        grid=(t // tm,),
        in_specs=[pl.BlockSpec((tm, d), row), pl.BlockSpec((tm, LANES), row),
                  pl.BlockSpec((1, d), const), pl.BlockSpec((1, d), const),
                  pl.BlockSpec(memory_space=pl.ANY)],
        out_specs=out_specs,
        scratch_shapes=[pltpu.VMEM((2, 2 * tm, d), F32), pltpu.SemaphoreType.DMA((2,))])
    return pl.pallas_call(
        functools.partial(_combine_ln_kernel, tm=tm, n_out=len(out_specs), bounds=bounds),
        grid_spec=grid_spec,
        out_shape=out_shape,
        compiler_params=_params("arbitrary"),
    )(p1, p2, x, gates, g, b, ys)


def _moe_routing(r, tm):
    t = r.shape[0]
    e = N_EXPERTS
    sel = r[:, :e].astype(jnp.int32)
    i1 = r[:, e].astype(jnp.int32)
    i2 = r[:, e + 1].astype(jnp.int32)
    cnt = jnp.cumsum(sel, axis=0)
    rank = cnt - sel
    padded = ((cnt[-1] + tm - 1) // tm) * tm
    ends = jnp.cumsum(padded)
    pos = (ends - padded)[None, :] + rank
    lane = jnp.arange(e, dtype=jnp.int32)[None, :]
    p1 = jnp.sum(jnp.where(lane == i1[:, None], pos, 0), axis=1).astype(jnp.int32)
    p2 = jnp.sum(jnp.where(lane == i2[:, None], pos, 0), axis=1).astype(jnp.int32)
    n_rows = TOP_K * t + e * tm
    tile_start = jnp.arange(n_rows // tm, dtype=jnp.int32) * tm
    tile_expert = jnp.sum((tile_start[:, None] >= ends[None, :]).astype(jnp.int32), axis=1)
    tile_expert = jnp.minimum(tile_expert, e - 1).astype(jnp.int32)
    last_tile = (ends[-1:] // tm - 1).astype(jnp.int32)
    return p1, p2, ends.astype(jnp.int32), tile_expert, last_tile, n_rows


def _deinterleave_perm():
    half = HEAD_DIM // 2
    return np.concatenate([np.arange(half) * 2, np.arange(half) * 2 + 1])


def _rope_tables(groups):
    pos = np.concatenate([np.tile(np.arange(seq), batch) for batch, seq in groups])
    pos = jnp.asarray(pos, jnp.int32)
    row = (pos // GRID_W).astype(F32)
    col = (pos % GRID_W).astype(F32)
    axis_dim = HEAD_DIM // 2
    inv = ROPE_THETA ** (-jnp.arange(0, axis_dim, 2, dtype=F32) / axis_dim)
    ang = jnp.concatenate([row[:, None] * inv, col[:, None] * inv], axis=-1)
    cos, sin = jnp.cos(ang), jnp.sin(ang)
    return jnp.concatenate([cos, cos], axis=-1), jnp.concatenate([-sin, sin], axis=-1)


def _group_rows(groups):
    out, row0 = [], 0
    for batch, seq in groups:
        out.append((row0, batch, seq))
        row0 += batch * seq
    return out


def _prep_w_in_even(w_in):
    perm = _deinterleave_perm()
    n_qk = N_HEADS_B + N_KV_B
    c0 = 3 * N_HEADS_A * HEAD_DIM
    pmat = np.zeros((HEAD_DIM, HEAD_DIM), np.float32)
    pmat[perm, np.arange(HEAD_DIM)] = 1.0
    w_in_b = w_in.astype(BF16)
    lead = w_in.shape[:2]
    w_qk = w_in_b[:, :, c0:c0 + n_qk * HEAD_DIM].reshape(*lead, n_qk, HEAD_DIM)
    w_qk = jnp.einsum('ldhk,kn->ldhn', w_qk, jnp.asarray(pmat, BF16), preferred_element_type=F32)
    return jnp.concatenate([w_in_b[:, :, :c0], w_qk.astype(BF16).reshape(*lead, n_qk * HEAD_DIM),
                            w_in_b[:, :, c0 + n_qk * HEAD_DIM:]], axis=2)


def _even_layer(x, xb, groups, layer, w_in_b, qk_gain, w_out_b, w1_b, w3_b, w2_b, ln_g, ln_b, rope, dil):
    w_a = N_HEADS_A * HEAD_DIM
    w_bq = N_HEADS_B * HEAD_DIM
    w_bkv = N_KV_B * HEAD_DIM
    perm = _deinterleave_perm()
    col_scale = jnp.concatenate([jnp.full((w_a,), SCALE * LOG2E, F32),
                                 jnp.ones((w_in_b.shape[2] - w_a,), F32)])[None, :]
    proj = _project(xb, w_in_b, layer, col_scale)

    n_b = N_HEADS_B + N_KV_B
    gains = jnp.concatenate([jnp.tile(qk_gain[0][perm][None], (N_HEADS_B, 1)),
                             jnp.tile(qk_gain[1][perm][None], (N_KV_B, 1))]).astype(F32)
    scales = jnp.concatenate([jnp.full((N_HEADS_B, HEAD_DIM), SCALE * LOG2E, F32),
                              jnp.ones((N_KV_B, HEAD_DIM), F32)])
    qk = _qk_prep(proj, gains.reshape(2, n_b // 2, HEAD_DIM), scales.reshape(2, n_b // 2, HEAD_DIM),
                  rope[0], rope[1], 3 * w_a)

    table, halo_blks, win_blks = dil
    oa, ob = [], []
    v_blk0 = (3 * w_a + w_bq + w_bkv) // HEAD_DIM
    for row0, batch, seq in _group_rows(groups):
        oa.append(_win_attn(proj, table, row0=row0, batch=batch, seq=seq, n_heads=N_HEADS_A,
                            q_blk0=0, k_blk0=N_HEADS_A, v_blk0=2 * N_HEADS_A,
                            halo_blks=halo_blks, win_blks=win_blks, hp=HEADS_PER_STEP_A))
        ob.append(_gqa(qk, proj, row0=row0, batch=batch, seq=seq, v_blk0=v_blk0))
    (x,) = _out_ln(x, [oa, ob], w_out_b, layer, ln_g[0][None], ln_b[0][None])
    return _ffn_ln(x, w1_b, w3_b, w2_b, layer, ln_g[1][None], ln_b[1][None])


def _odd_layer(x, xb, groups, layer, w_in_b, rpb, w_out_b, w_router, b_router, w1_b, w3_b, w2_b, ln_g, ln_b,
               final):
    w_c = N_HEADS_C * HEAD_DIM
    col_scale = jnp.concatenate([jnp.full((w_c,), SCALE * LOG2E, F32), jnp.ones((2 * w_c,), F32)])[None, :]
    proj = _project(xb, w_in_b, layer, col_scale)
    table, halo_blks, win_blks = _natten_table(rpb)
    o = []
    for row0, batch, seq in _group_rows(groups):
        o.append(_win_attn(proj, table, row0=row0, batch=batch, seq=seq, n_heads=N_HEADS_C,
                           q_blk0=0, k_blk0=N_HEADS_C, v_blk0=2 * N_HEADS_C,
                           halo_blks=halo_blks, win_blks=win_blks, hp=HEADS_PER_STEP_C))

    wr = jnp.pad(w_router.astype(F32), ((0, 0), (0, ROUTER_LANES - N_EXPERTS)))
    wr_hi = wr.astype(BF16)
    wr_lo = (wr - wr_hi.astype(F32)).astype(BF16)
    br = jnp.concatenate([b_router.astype(F32), jnp.full((ROUTER_LANES - N_EXPERTS,), NEG, F32)])[None, :]
    x, r = _out_ln(x, [o], w_out_b, layer, ln_g[0][None], ln_b[0][None], router=(wr_hi, wr_lo, br))

    p1, p2, ends, tile_expert, last_tile, n_rows = _moe_routing(r, MOE_TM)
    xs = _dispatch(x, p1, p2, ends, n_rows)
    ys = _moe_experts(xs, tile_expert, last_tile, w1_b, w3_b, w2_b, layer)
    gates = jnp.pad(r[:, N_EXPERTS + 2:N_EXPERTS + 4], ((0, 0), (0, LANES - 2)))
    final_rows = [batch * seq for batch, seq in groups] if final else None
    return _combine_ln(x, ys, p1, p2, gates, ln_g[1][None], ln_b[1][None], final_group_rows=final_rows)


def kernel(x_prompt, x_sample, ln_g, ln_b, w_in_even, qk_gain_b, w_out_even, ffn_w1, ffn_w3, ffn_w2,
           w_in_odd, rpb, w_out_odd, w_router, b_router, moe_w1, moe_w3, moe_w2):
    d = x_prompt.shape[-1]
    groups = [(x_prompt.shape[0], x_prompt.shape[1]), (x_sample.shape[0], x_sample.shape[1])]
    x = jnp.concatenate([x_prompt.reshape(-1, d), x_sample.reshape(-1, d)], axis=0)
    xb = x.astype(BF16)
    rope = _rope_tables(groups)
    dil = _dilated_table()
    even_w = (_prep_w_in_even(w_in_even), w_out_even.astype(BF16),
              ffn_w1.astype(BF16), ffn_w3.astype(BF16), ffn_w2.astype(BF16))
    odd_w = (w_in_odd.astype(BF16), w_out_odd.astype(BF16), moe_w1, moe_w3, moe_w2)
    depth = ln_g.shape[0]
    for i in range(depth):
        j = i // 2
        if i % 2 == 0:
            x, xb = _even_layer(x, xb, groups, j, even_w[0], qk_gain_b[j], even_w[1], *even_w[2:],
                                ln_g[i], ln_b[i], rope, dil)
        else:
            x, xb = _odd_layer(x, xb, groups, j, odd_w[0], rpb[j], odd_w[1], w_router[j], b_router[j],
                               *odd_w[2:], ln_g[i], ln_b[i], final=(i == depth - 1))
    if depth % 2 == 0:
        return (x.reshape(x_prompt.shape), xb.reshape(x_sample.shape))
    n_p = x_prompt.shape[0] * x_prompt.shape[1]
    return (x[:n_p].reshape(x_prompt.shape), x[n_p:].reshape(x_sample.shape))
```

```python
import functools
import math

import numpy as np
import jax
import jax.numpy as jnp
from jax import lax
from jax.experimental import pallas as pl
from jax.experimental.pallas import tpu as pltpu

HEAD_DIM = 128
GRID_W = 64
N_HEADS_A = 6
DILATED_BRANCHES = ((128, 1), (512, 4), (2048, 16))
N_HEADS_B = 10
N_KV_B = 2
N_HEADS_C = 16
NA_ROWS = 8
NA_COLS = 16
N_EXPERTS = 8
TOP_K = 2
DEPTH = 4
ROPE_THETA = 10000.0
LN_EPS = 1e-5
QK_EPS = 1e-6
NEG = -1e30
SCALE = HEAD_DIM ** -0.5
LOG2E = math.log2(math.e)
DN_ALPHA = (2 * DEPTH) ** 0.25

V7X_VMEM_BYTES = 64 * 2 ** 20
VMEM_LIMIT = V7X_VMEM_BYTES - 8 * 2 ** 20
LANES = 128
ATTN_TQ = 256
ATTN_ROWS = 128
HEADS_PER_STEP_A = 3
HEADS_PER_STEP_C = 4
ROUTER_LANES = LANES

F32 = jnp.float32
BF16 = jnp.bfloat16


def _params(*sem):
    return pltpu.CompilerParams(dimension_semantics=sem, vmem_limit_bytes=VMEM_LIMIT)


def _tile(n, pref):
    if n <= pref:
        return n
    t = (pref // LANES) * LANES
    while t >= LANES:
        if n % t == 0:
            return t
        t -= LANES
    return n


def _lane_tiles(s):
    return [s[:, t * LANES:(t + 1) * LANES] for t in range(s.shape[1] // LANES)]


def _with_ones(v):
    return jnp.concatenate([v, jnp.ones_like(v)], axis=1)


def _proj_kernel(x_ref, w_ref, cs_ref, o_ref):
    acc = jnp.dot(x_ref[...], w_ref[...], preferred_element_type=F32)
    o_ref[...] = (acc * cs_ref[...]).astype(o_ref.dtype)


def _project(xb, w_stack, layer, col_scale):
    t, k = xb.shape
    n = w_stack.shape[2]
    tm, tn = _tile(t, 1024), _tile(n, 2048)
    return pl.pallas_call(
        _proj_kernel,
        grid=(t // tm, n // tn),
        in_specs=[pl.BlockSpec((tm, k), lambda i, j: (i, 0)),
                  pl.BlockSpec((None, k, tn), lambda i, j: (layer, 0, j)),
                  pl.BlockSpec((1, tn), lambda i, j: (0, j))],
        out_specs=pl.BlockSpec((tm, tn), lambda i, j: (i, j)),
        out_shape=jax.ShapeDtypeStruct((t, n), BF16),
        compiler_params=_params("arbitrary", "arbitrary"),
    )(xb, w_stack, col_scale)


def _layer_norm(z, g, b):
    mu = jnp.mean(z, axis=-1, keepdims=True)
    zc = z - mu
    var = jnp.mean(zc * zc, axis=-1, keepdims=True)
    return zc * lax.rsqrt(var + LN_EPS) * g + b


def _qk_prep_kernel(p_ref, g_ref, sc_ref, cos_ref, sin_ref, o_ref, *, heads):
    c = cos_ref[...]
    s = sin_ref[...]
    for r in range(heads):
        x = p_ref[:, r * HEAD_DIM:(r + 1) * HEAD_DIM].astype(F32)
        ms = jnp.mean(x * x, axis=-1, keepdims=True)
        xn = x * lax.rsqrt(ms + QK_EPS) * g_ref[0, r:r + 1, :]
        y = xn * c + pltpu.roll(xn, HEAD_DIM // 2, 1) * s
        o_ref[:, r * HEAD_DIM:(r + 1) * HEAD_DIM] = (y * sc_ref[0, r:r + 1, :]).astype(o_ref.dtype)


def _qk_prep(proj, gains, scales, cos_t, sin_t, col0):
    t = proj.shape[0]
    n_heads = N_HEADS_B + N_KV_B
    half = n_heads // 2
    wblk = half * HEAD_DIM
    assert col0 % wblk == 0
    tm = _tile(t, 512)
    return pl.pallas_call(
        functools.partial(_qk_prep_kernel, heads=half),
        grid=(t // tm, 2),
        in_specs=[pl.BlockSpec((tm, wblk), lambda i, j: (i, col0 // wblk + j)),
                  pl.BlockSpec((1, half, HEAD_DIM), lambda i, j: (j, 0, 0)),
                  pl.BlockSpec((1, half, HEAD_DIM), lambda i, j: (j, 0, 0)),
                  pl.BlockSpec((tm, HEAD_DIM), lambda i, j: (i, 0)),
                  pl.BlockSpec((tm, HEAD_DIM), lambda i, j: (i, 0))],
        out_specs=pl.BlockSpec((tm, wblk), lambda i, j: (i, j)),
        out_shape=jax.ShapeDtypeStruct((t, n_heads * HEAD_DIM), BF16),
        compiler_params=_params("arbitrary", "arbitrary"),
    )(proj, gains, scales, cos_t, sin_t)


def _window_start_blk(i, halo_blks, win_blks, n_blks):
    return jnp.clip(i - halo_blks, 0, n_blks - win_blks)


def _win_attn_kernel(q_ref, k_ref, v_ref, tb_ref, o_ref, *, halo_blks, win_blks, n_blks, axis):
    i = pl.program_id(axis)
    tq = q_ref.shape[0]
    w = win_blks * tq
    start = pl.multiple_of(_window_start_blk(i, halo_blks, win_blks, n_blks) * tq, tq)
    hp = tb_ref.shape[0]
    for hd in range(hp):
        cols = slice(hd * HEAD_DIM, (hd + 1) * HEAD_DIM)
        k = k_ref[pl.ds(start, w), cols]
        v2 = _with_ones(v_ref[pl.ds(start, w), cols])
        for rb in range(tq // ATTN_ROWS):
            sl = slice(rb * ATTN_ROWS, (rb + 1) * ATTN_ROWS)
            s = lax.dot_general(q_ref[sl, cols], k, (((1,), (1,)), ((), ())), preferred_element_type=F32)
            tiles = _lane_tiles(s + tb_ref[hd, 0, sl, :])
            m = jnp.max(functools.reduce(jnp.maximum, tiles), axis=-1, keepdims=True)
            p = jnp.concatenate([jnp.exp2(t - m) for t in tiles], axis=1).astype(BF16)
            pv = jnp.dot(p, v2, preferred_element_type=F32)
            o_ref[sl, cols] = (pv[:, :HEAD_DIM] / pv[:, HEAD_DIM:]).astype(o_ref.dtype)


def _win_attn(src, table, *, row0, batch, seq, n_heads, q_blk0, k_blk0, v_blk0, halo_blks, win_blks, hp):
    tq = ATTN_TQ
    n_blks = seq // tq
    assert seq % tq == 0 and n_blks >= win_blks and row0 % seq == 0
    assert n_heads % hp == 0 and q_blk0 % hp == 0 and k_blk0 % hp == 0 and v_blk0 % hp == 0
    seq_blk0 = row0 // seq
    qrow0 = row0 // tq
    wh = hp * HEAD_DIM
    qb, kb, vb = q_blk0 // hp, k_blk0 // hp, v_blk0 // hp

    def variant(i):
        return i - _window_start_blk(i, halo_blks, win_blks, n_blks)

    kern = functools.partial(_win_attn_kernel, halo_blks=halo_blks, win_blks=win_blks,
                             n_blks=n_blks, axis=2)
    return pl.pallas_call(
        kern,
        grid=(batch, n_heads // hp, n_blks),
        in_specs=[pl.BlockSpec((tq, wh), lambda b, h, i: (qrow0 + b * n_blks + i, qb + h)),
                  pl.BlockSpec((seq, wh), lambda b, h, i: (seq_blk0 + b, kb + h), pipeline_mode=pl.Buffered(1)),
                  pl.BlockSpec((seq, wh), lambda b, h, i: (seq_blk0 + b, vb + h), pipeline_mode=pl.Buffered(1)),
                  pl.BlockSpec((hp, 1, tq, win_blks * tq), lambda b, h, i: (h, variant(i), 0, 0))],
        out_specs=pl.BlockSpec((tq, wh), lambda b, h, i: (b * n_blks + i, h)),
        out_shape=jax.ShapeDtypeStruct((batch * seq, n_heads * HEAD_DIM), BF16),
        compiler_params=_params("arbitrary", "arbitrary", "arbitrary"),
    )(src, src, src, table)


def _dilated_table():
    tq = ATTN_TQ
    halo = max(w // 2 for w, _ in DILATED_BRANCHES)
    halo_blks = halo // tq
    win_blks = 2 * halo_blks + 1
    v = jnp.arange(win_blks)[:, None, None]
    r = jnp.arange(tq)[None, :, None]
    c = jnp.arange(win_blks * tq)[None, None, :]
    delta = c - v * tq - r
    ad = jnp.abs(delta)
    cnt = jnp.zeros(delta.shape, F32)
    for window, dil in DILATED_BRANCHES:
        cnt = cnt + ((ad <= window // 2) & (delta % dil == 0)).astype(F32)
    slopes = 2.0 ** (-8.0 * jnp.arange(1, N_HEADS_A + 1, dtype=F32) / N_HEADS_A)
    bias = -slopes[:, None, None, None] * ad.astype(F32)[None] + jnp.log(jnp.maximum(cnt, 1.0))[None]
    table = jnp.where(cnt[None] > 0, bias * LOG2E, NEG)
    return table, halo_blks, win_blks


def _natten_table(rpb):
    tq = ATTN_TQ
    rows_per_tile = tq // GRID_W
    halo_blks = 1
    win_blks = 3
    assert NA_ROWS // 2 == rows_per_tile
    win_rows = win_blks * rows_per_tile
    qt = np.arange(win_blks)[:, None] * tq + np.arange(tq)[None, :]
    rq, qc = qt // GRID_W, qt % GRID_W
    kt = np.arange(win_blks * tq)
    rk, kc = kt // GRID_W, kt % GRID_W
    rs = np.clip(rq - NA_ROWS // 2, 0, win_rows - NA_ROWS)
    cs = np.clip(qc - NA_COLS // 2, 0, GRID_W - NA_COLS)
    valid = ((rk[None, None, :] >= rs[..., None]) & (rk[None, None, :] < rs[..., None] + NA_ROWS)
             & (kc[None, None, :] >= cs[..., None]) & (kc[None, None, :] < cs[..., None] + NA_COLS))
    n_ro, n_co = 2 * NA_ROWS - 1, 2 * NA_COLS - 1
    cols = np.arange(GRID_W)
    ci = np.clip(cols[None, :] - cols[:, None] + NA_COLS - 1, 0, n_co - 1)
    onehot = (np.arange(n_co)[:, None, None] == ci[None]).astype(np.float32).reshape(n_co, GRID_W * GRID_W)
    t1 = rpb.astype(F32)
    hi = t1.astype(BF16)
    mid = (t1 - hi.astype(F32)).astype(BF16)
    lo = (t1 - hi.astype(F32) - mid.astype(F32)).astype(BF16)
    oh = jnp.asarray(onehot, BF16)
    blocks = sum(jnp.einsum('hab,bq->haq', part, oh, preferred_element_type=F32) for part in (hi, mid, lo))
    blocks = blocks.reshape(-1, n_ro, GRID_W, GRID_W).transpose(0, 2, 1, 3)
    lo_pad = win_rows - NA_ROWS
    blocks = jnp.pad(blocks, ((0, 0), (0, 0), (lo_pad, lo_pad), (0, 0)))
    per_rq = [blocks[:, :, lo_pad + NA_ROWS - 1 - r:lo_pad + NA_ROWS - 1 - r + win_rows, :] for r in range(win_rows)]
    bias = jnp.stack(per_rq, axis=1)
    bias = bias.reshape(-1, win_blks, tq, win_blks * tq)
    table = jnp.where(jnp.asarray(valid)[None], bias * LOG2E, NEG)
    return table, halo_blks, win_blks


GQA_TQ = 1024
GQA_TK = 2048
GQA_ROWS = 256


def _gqa_kernel(q_ref, k_ref, v_ref, o_ref, qs_ref, acc_ref, m_ref, *, rep, tk, rows):
    tq = q_ref.shape[0]
    seq = k_ref.shape[0]
    for r in range(rep):
        qs_ref[r * tq:(r + 1) * tq, :] = q_ref[:, r * HEAD_DIM:(r + 1) * HEAD_DIM]
    m_ref[...] = jnp.full(m_ref.shape, -jnp.inf, F32)
    acc_ref[...] = jnp.zeros(acc_ref.shape, F32)

    def body(c, carry):
        off = pl.multiple_of(c * tk, tk)
        k = k_ref[pl.ds(off, tk), :]
        v2 = _with_ones(v_ref[pl.ds(off, tk), :])
        for rb in range(rep * tq // rows):
            sl = slice(rb * rows, (rb + 1) * rows)
            s = lax.dot_general(qs_ref[sl, :], k, (((1,), (1,)), ((), ())), preferred_element_type=F32)
            tiles = _lane_tiles(s)
            m_prev = m_ref[sl, :]
            m_new = jnp.maximum(m_prev, jnp.max(functools.reduce(jnp.maximum, tiles), axis=-1, keepdims=True))
            alpha = jnp.exp2(m_prev - m_new)
            p = jnp.concatenate([jnp.exp2(t - m_new) for t in tiles], axis=1).astype(BF16)
            pv = jnp.dot(p, v2, preferred_element_type=F32)
            acc_ref[sl, :] = jnp.concatenate([alpha, alpha], axis=1) * acc_ref[sl, :] + pv
            m_ref[sl, :] = m_new
        return carry

    lax.fori_loop(0, seq // tk, body, 0)
    acc = acc_ref[...]
    o = acc[:, :HEAD_DIM] / acc[:, HEAD_DIM:]
    for r in range(rep):
        o_ref[:, r * HEAD_DIM:(r + 1) * HEAD_DIM] = o[r * tq:(r + 1) * tq].astype(o_ref.dtype)


def _gqa(qk, proj, *, row0, batch, seq, v_blk0):
    rep = N_HEADS_B // N_KV_B
    tq = GQA_TQ
    tk = _tile(seq, GQA_TK)
    n_blks = seq // tq
    assert seq % tq == 0 and row0 % seq == 0 and (rep * tq) % GQA_ROWS == 0
    seq_blk0 = row0 // seq
    qrow0 = row0 // tq
    wq = rep * HEAD_DIM
    return pl.pallas_call(
        functools.partial(_gqa_kernel, rep=rep, tk=tk, rows=GQA_ROWS),
        grid=(batch, N_KV_B, n_blks),
        in_specs=[pl.BlockSpec((tq, wq), lambda b, g, i: (qrow0 + b * n_blks + i, g)),
                  pl.BlockSpec((seq, HEAD_DIM), lambda b, g, i: (seq_blk0 + b, N_HEADS_B + g)),
                  pl.BlockSpec((seq, HEAD_DIM), lambda b, g, i: (seq_blk0 + b, v_blk0 + g))],
        out_specs=pl.BlockSpec((tq, wq), lambda b, g, i: (b * n_blks + i, g)),
        out_shape=jax.ShapeDtypeStruct((batch * seq, N_HEADS_B * HEAD_DIM), BF16),
        scratch_shapes=[pltpu.VMEM((rep * tq, HEAD_DIM), BF16),
                        pltpu.VMEM((rep * tq, 2 * HEAD_DIM), F32),
                        pltpu.VMEM((rep * tq, HEAD_DIM), F32)],
        compiler_params=_params("arbitrary", "arbitrary", "arbitrary"),
    )(qk, qk, proj)


def _out_ln_kernel(x_ref, *refs, widths, bounds, router):
    i = pl.program_id(0)
    n_groups = len(bounds) + 1
    n_act = len(widths)
    w_ref, g_ref, b_ref = refs[n_act * n_groups:n_act * n_groups + 3]
    rest = refs[n_act * n_groups + 3:]
    h = None
    row0 = 0
    for a, width in enumerate(widths):
        grp = refs[a * n_groups:(a + 1) * n_groups]
        val = grp[-1][...]
        for gi in reversed(range(n_groups - 1)):
            val = jnp.where(i < bounds[gi], grp[gi][...], val)
        part = jnp.dot(val, w_ref[row0:row0 + width, :], preferred_element_type=F32)
        h = part if h is None else h + part
        row0 += width
    y = _layer_norm(DN_ALPHA * x_ref[...] + h, g_ref[...], b_ref[...])
    if not router:
        (y_ref,) = rest
        y_ref[...] = y
        return
    wr_hi_ref, wr_lo_ref, br_ref, y_ref, r_ref = rest
    y_hi = y.astype(BF16)
    y_lo = (y - y_hi.astype(F32)).astype(BF16)
    y_ref[...] = y
    logits = (jnp.dot(y_hi, wr_hi_ref[...], preferred_element_type=F32)
              + jnp.dot(y_lo, wr_hi_ref[...], preferred_element_type=F32)
              + jnp.dot(y_hi, wr_lo_ref[...], preferred_element_type=F32)) + br_ref[...]
    lane = lax.broadcasted_iota(jnp.int32, logits.shape, 1).astype(F32)
    m1 = jnp.max(logits, axis=-1, keepdims=True)
    i1 = jnp.min(jnp.where(logits == m1, lane, float(ROUTER_LANES)), axis=-1, keepdims=True)
    rest_l = jnp.where(lane == i1, -jnp.inf, logits)
    m2 = jnp.max(rest_l, axis=-1, keepdims=True)
    i2 = jnp.min(jnp.where(rest_l == m2, lane, float(ROUTER_LANES)), axis=-1, keepdims=True)
    e2 = jnp.exp(m2 - m1)
    den = 1.0 + e2
    g1 = 1.0 / den
    g2 = e2 / den
    sel = ((lane == i1) | (lane == i2)).astype(F32)
    e = N_EXPERTS
    out = jnp.where(lane < e, sel, 0.0)
    out = jnp.where(lane == e, i1, out)
    out = jnp.where(lane == e + 1, i2, out)
    out = jnp.where(lane == e + 2, g1, out)
    out = jnp.where(lane == e + 3, g2, out)
    r_ref[...] = out


def _out_ln(x, acts, w_stack, layer, g, b, router=None):
    t, d = x.shape
    group_rows = [a.shape[0] for a in acts[0]]
    tm = _tile(math.gcd(*group_rows), 512)
    starts = np.cumsum([0] + [r // tm for r in group_rows])
    row = lambda i: (i, 0)
    const = lambda i: (0, 0)
    in_specs = [pl.BlockSpec((tm, d), row)]
    args = [x]
    for per_group in acts:
        for gi, a in enumerate(per_group):
            lo, n = int(starts[gi]), int(starts[gi + 1] - starts[gi])
            in_specs.append(pl.BlockSpec((tm, a.shape[1]),
                                         lambda i, lo=lo, n=n: (jnp.clip(i - lo, 0, n - 1), 0)))
            args.append(a)
    in_specs += [pl.BlockSpec((None,) + w_stack.shape[1:], lambda i: (layer, 0, 0), pipeline_mode=pl.Buffered(1)),
                 pl.BlockSpec((1, d), const), pl.BlockSpec((1, d), const)]
    args += [w_stack, g, b]
    out_specs = [pl.BlockSpec((tm, d), row)]
    out_shape = [jax.ShapeDtypeStruct((t, d), F32)]
    widths = tuple(per_group[0].shape[1] for per_group in acts)
    bounds = tuple(int(s) for s in starts[1:-1])
    if router is not None:
        in_specs += [pl.BlockSpec((d, ROUTER_LANES), const), pl.BlockSpec((d, ROUTER_LANES), const),
                     pl.BlockSpec((1, ROUTER_LANES), const)]
        out_specs.append(pl.BlockSpec((tm, ROUTER_LANES), row))
        out_shape.append(jax.ShapeDtypeStruct((t, ROUTER_LANES), F32))
        args += list(router)
    return pl.pallas_call(
        functools.partial(_out_ln_kernel, widths=widths, bounds=bounds, router=router is not None),
        grid=(t // tm,),
        in_specs=in_specs, out_specs=out_specs, out_shape=out_shape,
        compiler_params=_params("arbitrary"),
    )(*args)


def _swiglu_accumulate(acc_ref, xb, w1, w3, w2):
    h1 = jnp.dot(xb, w1, preferred_element_type=F32)
    h3 = jnp.dot(xb, w3, preferred_element_type=F32)
    h = (h1 * (1.0 / (1.0 + jnp.exp(-h1)))) * h3
    acc_ref[...] += jnp.dot(h.astype(BF16), w2, preferred_element_type=F32)


def _ffn_ln_kernel(x_ref, w1_ref, w3_ref, w2_ref, g_ref, b_ref, y_ref, yb_ref, xb_ref):
    j = pl.program_id(1)

    @pl.when(j == 0)
    def _():
        xb_ref[...] = x_ref[...].astype(BF16)
        y_ref[...] = jnp.zeros(y_ref.shape, F32)

    _swiglu_accumulate(y_ref, xb_ref[...], w1_ref[...], w3_ref[...], w2_ref[...])

    @pl.when(j == pl.num_programs(1) - 1)
    def _():
        y = _layer_norm(DN_ALPHA * x_ref[...] + y_ref[...], g_ref[...], b_ref[...])
        y_ref[...] = y
        yb_ref[...] = y.astype(BF16)


FFN_TM = 512


def _ffn_ln(x, w1, w3, w2, layer, g, b):
    t, d = x.shape
    f = w1.shape[2]
    tm, tf = _tile(t, FFN_TM), _tile(f, 512)
    return pl.pallas_call(
        _ffn_ln_kernel,
        grid=(t // tm, f // tf),
        in_specs=[pl.BlockSpec((tm, d), lambda i, j: (i, 0)),
                  pl.BlockSpec((None, d, tf), lambda i, j: (layer, 0, j)),
                  pl.BlockSpec((None, d, tf), lambda i, j: (layer, 0, j)),
                  pl.BlockSpec((None, tf, d), lambda i, j: (layer, j, 0)),
                  pl.BlockSpec((1, d), lambda i, j: (0, 0)),
                  pl.BlockSpec((1, d), lambda i, j: (0, 0))],
        out_specs=[pl.BlockSpec((tm, d), lambda i, j: (i, 0)),
                   pl.BlockSpec((tm, d), lambda i, j: (i, 0))],
        out_shape=[jax.ShapeDtypeStruct((t, d), F32), jax.ShapeDtypeStruct((t, d), BF16)],
        scratch_shapes=[pltpu.VMEM((tm, d), BF16)],
        compiler_params=_params("arbitrary", "arbitrary"),
    )(x, w1, w3, w2, g, b)


MOE_TM = 1024
DISPATCH_TM = 512
COMBINE_TM = 256
DMA_ISSUE_UNROLL = 8
ZERO_ROWS = 256


def _dispatch_kernel(p1_ref, p2_ref, ends_ref, x_ref, xs_hbm, zbuf, sem, zsem, *, tm, group_tm):
    i = pl.program_id(0)
    base = i * tm

    @pl.when(i == 0)
    def _():
        zbuf[...] = jnp.zeros(zbuf.shape, zbuf.dtype)
        zrows = zbuf.shape[0]

        def clear_tile(row0):
            for c in range(group_tm // zrows):
                start = pl.multiple_of(row0 + c * zrows, zrows)
                pltpu.make_async_copy(zbuf, xs_hbm.at[pl.ds(start, zrows)], zsem).start()
            for c in range(group_tm // zrows):
                pltpu.make_async_copy(zbuf, xs_hbm.at[pl.ds(0, zrows)], zsem).wait()

        for e in range(N_EXPERTS):
            end = ends_ref[e]
            prev_end = ends_ref[e - 1] if e else 0

            @pl.when(end > prev_end)
            def _():
                clear_tile(end - group_tm)

        for k in range(N_EXPERTS):
            tail = ends_ref[N_EXPERTS - 1] + k * group_tm

            @pl.when(tail < xs_hbm.shape[0])
            def _():
                clear_tile(tail)

    def issue(t, carry):
        src = x_ref.at[pl.ds(t, 1)]
        pltpu.make_async_copy(src, xs_hbm.at[pl.ds(p1_ref[base + t], 1)], sem).start()
        pltpu.make_async_copy(src, xs_hbm.at[pl.ds(p2_ref[base + t], 1)], sem).start()
        return carry

    lax.fori_loop(0, tm, issue, 0, unroll=DMA_ISSUE_UNROLL)
    for _ in range(TOP_K):
        pltpu.make_async_copy(x_ref, xs_hbm.at[pl.ds(0, tm)], sem).wait()


def _dispatch(x, p1, p2, ends, n_rows):
    t, d = x.shape
    tm = _tile(t, DISPATCH_TM)
    grid_spec = pltpu.PrefetchScalarGridSpec(
        num_scalar_prefetch=3,
        grid=(t // tm,),
        in_specs=[pl.BlockSpec((tm, d), lambda i, a, c, e: (i, 0))],
        out_specs=pl.BlockSpec(memory_space=pl.ANY),
        scratch_shapes=[pltpu.VMEM((ZERO_ROWS, d), x.dtype), pltpu.SemaphoreType.DMA(()),
                        pltpu.SemaphoreType.DMA(())])
    return pl.pallas_call(
        functools.partial(_dispatch_kernel, tm=tm, group_tm=MOE_TM),
        grid_spec=grid_spec,
        out_shape=jax.ShapeDtypeStruct((n_rows, d), x.dtype),
        compiler_params=_params("arbitrary"),
    )(p1, p2, ends, x)


def _moe_expert_kernel(te_ref, last_ref, xs_ref, w1_ref, w3_ref, w2_ref, y_ref, xb_ref):
    i = pl.program_id(0)
    j = pl.program_id(1)
    active = i <= last_ref[0]

    @pl.when(j == 0)
    def _():
        y_ref[...] = jnp.zeros(y_ref.shape, F32)

    @pl.when(active)
    def _():
        @pl.when(j == 0)
        def _():
            xb_ref[...] = xs_ref[...].astype(BF16)

        _swiglu_accumulate(y_ref, xb_ref[...], w1_ref[...].astype(BF16), w3_ref[...].astype(BF16),
                           w2_ref[...].astype(BF16))


def _moe_experts(xs, tile_expert, last_tile, w1, w3, w2, layer):
    p, d = xs.shape
    f = w1.shape[3]
    tm, tf = MOE_TM, _tile(f, 256)
    grid_spec = pltpu.PrefetchScalarGridSpec(
        num_scalar_prefetch=2,
        grid=(p // tm, f // tf),
        in_specs=[pl.BlockSpec((tm, d), lambda i, j, te, ac: (jnp.minimum(i, ac[0]), 0)),
                  pl.BlockSpec((None, None, d, tf), lambda i, j, te, ac: (layer, te[i], 0, j)),
                  pl.BlockSpec((None, None, d, tf), lambda i, j, te, ac: (layer, te[i], 0, j)),
                  pl.BlockSpec((None, None, tf, d), lambda i, j, te, ac: (layer, te[i], j, 0))],
        out_specs=pl.BlockSpec((tm, d), lambda i, j, te, ac: (i, 0)),
        scratch_shapes=[pltpu.VMEM((tm, d), BF16)])
    return pl.pallas_call(
        _moe_expert_kernel,
        grid_spec=grid_spec,
        out_shape=jax.ShapeDtypeStruct((p, d), F32),
        compiler_params=_params("arbitrary", "arbitrary"),
    )(tile_expert, last_tile, xs, w1, w3, w2)


def _combine_ln_kernel(p1_ref, p2_ref, x_ref, gt_ref, g_ref, b_ref, ys_hbm, *refs, tm, n_out, bounds):
    out_refs = refs[:n_out]
    buf, sem = refs[n_out:]
    i = pl.program_id(0)
    n = pl.num_programs(0)

    def issue(tile, slot):
        base = tile * tm

        def body(t, carry):
            pltpu.make_async_copy(ys_hbm.at[pl.ds(p1_ref[base + t], 1)],
                                  buf.at[slot, pl.ds(t, 1)], sem.at[slot]).start()
            pltpu.make_async_copy(ys_hbm.at[pl.ds(p2_ref[base + t], 1)],
                                  buf.at[slot, pl.ds(tm + t, 1)], sem.at[slot]).start()
            return carry

        lax.fori_loop(0, tm, body, 0, unroll=DMA_ISSUE_UNROLL)

    @pl.when(i == 0)
    def _():
        issue(0, 0)

    @pl.when(i + 1 < n)
    def _():
        issue(i + 1, (i + 1) % 2)

    slot = i % 2
    pltpu.make_async_copy(ys_hbm.at[pl.ds(0, 2 * tm)], buf.at[slot], sem.at[slot]).wait()
    g1 = gt_ref[:, 0:1]
    g2 = gt_ref[:, 1:2]
    f = g1 * buf[slot, pl.ds(0, tm), :] + g2 * buf[slot, pl.ds(tm, tm), :]
    y = _layer_norm(DN_ALPHA * x_ref[...] + f, g_ref[...], b_ref[...])
    if bounds is None:
        y_ref, yb_ref = out_refs
        y_ref[...] = y
        yb_ref[...] = y.astype(BF16)
    else:
        edges = (0,) + bounds + (None,)
        for gi, o_ref in enumerate(out_refs):
            lo, hi = edges[gi], edges[gi + 1]
            in_group = (i >= lo) if hi is None else ((i >= lo) & (i < hi))

            @pl.when(in_group)
            def _(o_ref=o_ref):
                o_ref[...] = y


def _combine_ln(x, ys, p1, p2, gates, g, b, final_group_rows=None):
    t, d = x.shape
    rows = [t] if final_group_rows is None else list(final_group_rows)
    tm = _tile(math.gcd(*rows) if len(rows) > 1 else t, COMBINE_TM)
    row = lambda i, a, c: (i, 0)
    const = lambda i, a, c: (0, 0)
    if final_group_rows is None:
        bounds = None
        out_specs = [pl.BlockSpec((tm, d), row), pl.BlockSpec((tm, d), row)]
        out_shape = [jax.ShapeDtypeStruct((t, d), F32), jax.ShapeDtypeStruct((t, d), BF16)]
    else:
        starts = np.cumsum([0] + [r // tm for r in rows])
        bounds = tuple(int(s) for s in starts[1:-1])
        out_specs, out_shape = [], []
        for gi, r in enumerate(rows):
            lo, n = int(starts[gi]), int(starts[gi + 1] - starts[gi])
            out_specs.append(pl.BlockSpec((tm, d), lambda i, a, c, lo=lo, n=n: (jnp.clip(i - lo, 0, n - 1), 0)))
            out_shape.append(jax.ShapeDtypeStruct((r, d), F32))
    grid_spec = pltpu.PrefetchScalarGridSpec(
        num_scalar_prefetch=2,
        grid=(t // tm,),
        in_specs=[pl.BlockSpec((tm, d), row), pl.BlockSpec((tm, LANES), row),
                  pl.BlockSpec((1, d), const), pl.BlockSpec((1, d), const),
                  pl.BlockSpec(memory_space=pl.ANY)],
        out_specs=out_specs,
        scratch_shapes=[pltpu.VMEM((2, 2 * tm, d), F32), pltpu.SemaphoreType.DMA((2,))])
    return pl.pallas_call(
        functools.partial(_combine_ln_kernel, tm=tm, n_out=len(out_specs), bounds=bounds),
        grid_spec=grid_spec,
        out_shape=out_shape,
        compiler_params=_params("arbitrary"),
    )(p1, p2, x, gates, g, b, ys)


def _moe_routing(r, tm):
    t = r.shape[0]
    e = N_EXPERTS
    sel = r[:, :e].astype(jnp.int32)
    i1 = r[:, e].astype(jnp.int32)
    i2 = r[:, e + 1].astype(jnp.int32)
    cnt = jnp.cumsum(sel, axis=0)
    rank = cnt - sel
    padded = ((cnt[-1] + tm - 1) // tm) * tm
    ends = jnp.cumsum(padded)
    pos = (ends - padded)[None, :] + rank
    lane = jnp.arange(e, dtype=jnp.int32)[None, :]
    p1 = jnp.sum(jnp.where(lane == i1[:, None], pos, 0), axis=1).astype(jnp.int32)
    p2 = jnp.sum(jnp.where(lane == i2[:, None], pos, 0), axis=1).astype(jnp.int32)
    n_rows = TOP_K * t + e * tm
    tile_start = jnp.arange(n_rows // tm, dtype=jnp.int32) * tm
    tile_expert = jnp.sum((tile_start[:, None] >= ends[None, :]).astype(jnp.int32), axis=1)
    tile_expert = jnp.minimum(tile_expert, e - 1).astype(jnp.int32)
    last_tile = (ends[-1:] // tm - 1).astype(jnp.int32)
    return p1, p2, ends.astype(jnp.int32), tile_expert, last_tile, n_rows


def _deinterleave_perm():
    half = HEAD_DIM // 2
    return np.concatenate([np.arange(half) * 2, np.arange(half) * 2 + 1])


def _rope_tables(groups):
    pos = np.concatenate([np.tile(np.arange(seq), batch) for batch, seq in groups])
    pos = jnp.asarray(pos, jnp.int32)
    row = (pos // GRID_W).astype(F32)
    col = (pos % GRID_W).astype(F32)
    axis_dim = HEAD_DIM // 2
    inv = ROPE_THETA ** (-jnp.arange(0, axis_dim, 2, dtype=F32) / axis_dim)
    ang = jnp.concatenate([row[:, None] * inv, col[:, None] * inv], axis=-1)
    cos, sin = jnp.cos(ang), jnp.sin(ang)
    return jnp.concatenate([cos, cos], axis=-1), jnp.concatenate([-sin, sin], axis=-1)


def _group_rows(groups):
    out, row0 = [], 0
    for batch, seq in groups:
        out.append((row0, batch, seq))
        row0 += batch * seq
    return out


def _prep_w_in_even(w_in):
    perm = _deinterleave_perm()
    n_qk = N_HEADS_B + N_KV_B
    c0 = 3 * N_HEADS_A * HEAD_DIM
    pmat = np.zeros((HEAD_DIM, HEAD_DIM), np.float32)
    pmat[perm, np.arange(HEAD_DIM)] = 1.0
    w_in_b = w_in.astype(BF16)
    lead = w_in.shape[:2]
    w_qk = w_in_b[:, :, c0:c0 + n_qk * HEAD_DIM].reshape(*lead, n_qk, HEAD_DIM)
    w_qk = jnp.einsum('ldhk,kn->ldhn', w_qk, jnp.asarray(pmat, BF16), preferred_element_type=F32)
    return jnp.concatenate([w_in_b[:, :, :c0], w_qk.astype(BF16).reshape(*lead, n_qk * HEAD_DIM),
                            w_in_b[:, :, c0 + n_qk * HEAD_DIM:]], axis=2)


def _even_layer(x, xb, groups, layer, w_in_b, qk_gain, w_out_b, w1_b, w3_b, w2_b, ln_g, ln_b, rope, dil):
    w_a = N_HEADS_A * HEAD_DIM
    w_bq = N_HEADS_B * HEAD_DIM
    w_bkv = N_KV_B * HEAD_DIM
    perm = _deinterleave_perm()
    col_scale = jnp.concatenate([jnp.full((w_a,), SCALE * LOG2E, F32),
                                 jnp.ones((w_in_b.shape[2] - w_a,), F32)])[None, :]
    proj = _project(xb, w_in_b, layer, col_scale)

    n_b = N_HEADS_B + N_KV_B
    gains = jnp.concatenate([jnp.tile(qk_gain[0][perm][None], (N_HEADS_B, 1)),
                             jnp.tile(qk_gain[1][perm][None], (N_KV_B, 1))]).astype(F32)
    scales = jnp.concatenate([jnp.full((N_HEADS_B, HEAD_DIM), SCALE * LOG2E, F32),
                              jnp.ones((N_KV_B, HEAD_DIM), F32)])
    qk = _qk_prep(proj, gains.reshape(2, n_b // 2, HEAD_DIM), scales.reshape(2, n_b // 2, HEAD_DIM),
                  rope[0], rope[1], 3 * w_a)

    table, halo_blks, win_blks = dil
    oa, ob = [], []
    v_blk0 = (3 * w_a + w_bq + w_bkv) // HEAD_DIM
    for row0, batch, seq in _group_rows(groups):
        oa.append(_win_attn(proj, table, row0=row0, batch=batch, seq=seq, n_heads=N_HEADS_A,
                            q_blk0=0, k_blk0=N_HEADS_A, v_blk0=2 * N_HEADS_A,
                            halo_blks=halo_blks, win_blks=win_blks, hp=HEADS_PER_STEP_A))
        ob.append(_gqa(qk, proj, row0=row0, batch=batch, seq=seq, v_blk0=v_blk0))
    (x,) = _out_ln(x, [oa, ob], w_out_b, layer, ln_g[0][None], ln_b[0][None])
    return _ffn_ln(x, w1_b, w3_b, w2_b, layer, ln_g[1][None], ln_b[1][None])


def _odd_layer(x, xb, groups, layer, w_in_b, rpb, w_out_b, w_router, b_router, w1_b, w3_b, w2_b, ln_g, ln_b,
               final):
    w_c = N_HEADS_C * HEAD_DIM
    col_scale = jnp.concatenate([jnp.full((w_c,), SCALE * LOG2E, F32), jnp.ones((2 * w_c,), F32)])[None, :]
    proj = _project(xb, w_in_b, layer, col_scale)
    table, halo_blks, win_blks = _natten_table(rpb)
    o = []
    for row0, batch, seq in _group_rows(groups):
        o.append(_win_attn(proj, table, row0=row0, batch=batch, seq=seq, n_heads=N_HEADS_C,
                           q_blk0=0, k_blk0=N_HEADS_C, v_blk0=2 * N_HEADS_C,
                           halo_blks=halo_blks, win_blks=win_blks, hp=HEADS_PER_STEP_C))

    wr = jnp.pad(w_router.astype(F32), ((0, 0), (0, ROUTER_LANES - N_EXPERTS)))
    wr_hi = wr.astype(BF16)
    wr_lo = (wr - wr_hi.astype(F32)).astype(BF16)
    br = jnp.concatenate([b_router.astype(F32), jnp.full((ROUTER_LANES - N_EXPERTS,), NEG, F32)])[None, :]
    x, r = _out_ln(x, [o], w_out_b, layer, ln_g[0][None], ln_b[0][None], router=(wr_hi, wr_lo, br))

    p1, p2, ends, tile_expert, last_tile, n_rows = _moe_routing(r, MOE_TM)
    xs = _dispatch(x, p1, p2, ends, n_rows)
    ys = _moe_experts(xs, tile_expert, last_tile, w1_b, w3_b, w2_b, layer)
    gates = jnp.pad(r[:, N_EXPERTS + 2:N_EXPERTS + 4], ((0, 0), (0, LANES - 2)))
    final_rows = [batch * seq for batch, seq in groups] if final else None
    return _combine_ln(x, ys, p1, p2, gates, ln_g[1][None], ln_b[1][None], final_group_rows=final_rows)


def kernel(x_prompt, x_sample, ln_g, ln_b, w_in_even, qk_gain_b, w_out_even, ffn_w1, ffn_w3, ffn_w2,
           w_in_odd, rpb, w_out_odd, w_router, b_router, moe_w1, moe_w3, moe_w2):
    d = x_prompt.shape[-1]
    groups = [(x_prompt.shape[0], x_prompt.shape[1]), (x_sample.shape[0], x_sample.shape[1])]
    x = jnp.concatenate([x_prompt.reshape(-1, d), x_sample.reshape(-1, d)], axis=0)
    xb = x.astype(BF16)
    rope = _rope_tables(groups)
    dil = _dilated_table()
    even_w = (_prep_w_in_even(w_in_even), w_out_even.astype(BF16),
              ffn_w1.astype(BF16), ffn_w3.astype(BF16), ffn_w2.astype(BF16))
    odd_w = (w_in_odd.astype(BF16), w_out_odd.astype(BF16), moe_w1, moe_w3, moe_w2)
    depth = ln_g.shape[0]
    for i in range(depth):
        j = i // 2
        if i % 2 == 0:
            x, xb = _even_layer(x, xb, groups, j, even_w[0], qk_gain_b[j], even_w[1], *even_w[2:],
                                ln_g[i], ln_b[i], rope, dil)
        else:
            x, xb = _odd_layer(x, xb, groups, j, odd_w[0], rpb[j], odd_w[1], w_router[j], b_router[j],
                               *odd_w[2:], ln_g[i], ln_b[i], final=(i == depth - 1))
    if depth % 2 == 0:
        return (x.reshape(x_prompt.shape), xb.reshape(x_sample.shape))
    n_p = x_prompt.shape[0] * x_prompt.shape[1]
    return (x[:n_p].reshape(x_prompt.shape), x[n_p:].reshape(x_sample.shape))
```

```python
import functools
import math

import numpy as np
import jax
import jax.numpy as jnp
from jax import lax
from jax.experimental import pallas as pl
from jax.experimental.pallas import tpu as pltpu

HEAD_DIM = 128
GRID_W = 64
N_HEADS_A = 6
DILATED_BRANCHES = ((128, 1), (512, 4), (2048, 16))
N_HEADS_B = 10
N_KV_B = 2
N_HEADS_C = 16
NA_ROWS = 8
NA_COLS = 16
N_EXPERTS = 8
TOP_K = 2
DEPTH = 4
ROPE_THETA = 10000.0
LN_EPS = 1e-5
QK_EPS = 1e-6
NEG = -1e30
SCALE = HEAD_DIM ** -0.5
LOG2E = math.log2(math.e)
DN_ALPHA = (2 * DEPTH) ** 0.25

V7X_VMEM_BYTES = 64 * 2 ** 20
VMEM_LIMIT = V7X_VMEM_BYTES - 8 * 2 ** 20
LANES = 128
ATTN_TQ = 256
ATTN_ROWS = 128
HEADS_PER_STEP_A = 3
HEADS_PER_STEP_C = 4
ROUTER_LANES = LANES

F32 = jnp.float32
BF16 = jnp.bfloat16


def _params(*sem):
    return pltpu.CompilerParams(dimension_semantics=sem, vmem_limit_bytes=VMEM_LIMIT)


def _tile(n, pref):
    if n <= pref:
        return n
    t = (pref // LANES) * LANES
    while t >= LANES:
        if n % t == 0:
            return t
        t -= LANES
    return n


def _lane_tiles(s):
    return [s[:, t * LANES:(t + 1) * LANES] for t in range(s.shape[1] // LANES)]


def _with_ones(v):
    return jnp.concatenate([v, jnp.ones_like(v)], axis=1)


def _proj_kernel(x_ref, w_ref, cs_ref, o_ref):
    acc = jnp.dot(x_ref[...], w_ref[...], preferred_element_type=F32)
    o_ref[...] = (acc * cs_ref[...]).astype(o_ref.dtype)


def _project(xb, w_stack, layer, col_scale):
    t, k = xb.shape
    n = w_stack.shape[2]
    tm, tn = _tile(t, 1024), _tile(n, 2048)
    return pl.pallas_call(
        _proj_kernel,
        grid=(t // tm, n // tn),
        in_specs=[pl.BlockSpec((tm, k), lambda i, j: (i, 0)),
                  pl.BlockSpec((None, k, tn), lambda i, j: (layer, 0, j)),
                  pl.BlockSpec((1, tn), lambda i, j: (0, j))],
        out_specs=pl.BlockSpec((tm, tn), lambda i, j: (i, j)),
        out_shape=jax.ShapeDtypeStruct((t, n), BF16),
        compiler_params=_params("arbitrary", "arbitrary"),
    )(xb, w_stack, col_scale)


def _layer_norm(z, g, b):
    mu = jnp.mean(z, axis=-1, keepdims=True)
    zc = z - mu
    var = jnp.mean(zc * zc, axis=-1, keepdims=True)
    return zc * lax.rsqrt(var + LN_EPS) * g + b


def _qk_prep_kernel(p_ref, g_ref, sc_ref, cos_ref, sin_ref, o_ref, *, heads):
    c = cos_ref[...]
    s = sin_ref[...]
    for r in range(heads):
        x = p_ref[:, r * HEAD_DIM:(r + 1) * HEAD_DIM].astype(F32)
        ms = jnp.mean(x * x, axis=-1, keepdims=True)
        xn = x * lax.rsqrt(ms + QK_EPS) * g_ref[0, r:r + 1, :]
        y = xn * c + pltpu.roll(xn, HEAD_DIM // 2, 1) * s
        o_ref[:, r * HEAD_DIM:(r + 1) * HEAD_DIM] = (y * sc_ref[0, r:r + 1, :]).astype(o_ref.dtype)


def _qk_prep(proj, gains, scales, cos_t, sin_t, col0):
    t = proj.shape[0]
    n_heads = N_HEADS_B + N_KV_B
    half = n_heads // 2
    wblk = half * HEAD_DIM
    assert col0 % wblk == 0
    tm = _tile(t, 512)
    return pl.pallas_call(
        functools.partial(_qk_prep_kernel, heads=half),
        grid=(t // tm, 2),
        in_specs=[pl.BlockSpec((tm, wblk), lambda i, j: (i, col0 // wblk + j)),
                  pl.BlockSpec((1, half, HEAD_DIM), lambda i, j: (j, 0, 0)),
                  pl.BlockSpec((1, half, HEAD_DIM), lambda i, j: (j, 0, 0)),
                  pl.BlockSpec((tm, HEAD_DIM), lambda i, j: (i, 0)),
                  pl.BlockSpec((tm, HEAD_DIM), lambda i, j: (i, 0))],
        out_specs=pl.BlockSpec((tm, wblk), lambda i, j: (i, j)),
        out_shape=jax.ShapeDtypeStruct((t, n_heads * HEAD_DIM), BF16),
        compiler_params=_params("arbitrary", "arbitrary"),
    )(proj, gains, scales, cos_t, sin_t)


def _window_start_blk(i, halo_blks, win_blks, n_blks):
    return jnp.clip(i - halo_blks, 0, n_blks - win_blks)


def _win_attn_kernel(q_ref, k_ref, v_ref, tb_ref, o_ref, *, halo_blks, win_blks, n_blks, axis):
    i = pl.program_id(axis)
    tq = q_ref.shape[0]
    w = win_blks * tq
    start = pl.multiple_of(_window_start_blk(i, halo_blks, win_blks, n_blks) * tq, tq)
    hp = tb_ref.shape[0]
    for hd in range(hp):
        cols = slice(hd * HEAD_DIM, (hd + 1) * HEAD_DIM)
        k = k_ref[pl.ds(start, w), cols]
        v2 = _with_ones(v_ref[pl.ds(start, w), cols])
        for rb in range(tq // ATTN_ROWS):
            sl = slice(rb * ATTN_ROWS, (rb + 1) * ATTN_ROWS)
            s = lax.dot_general(q_ref[sl, cols], k, (((1,), (1,)), ((), ())), preferred_element_type=F32)
            tiles = _lane_tiles(s + tb_ref[hd, 0, sl, :])
            m = jnp.max(functools.reduce(jnp.maximum, tiles), axis=-1, keepdims=True)
            p = jnp.concatenate([jnp.exp2(t - m) for t in tiles], axis=1).astype(BF16)
            pv = jnp.dot(p, v2, preferred_element_type=F32)
            o_ref[sl, cols] = (pv[:, :HEAD_DIM] / pv[:, HEAD_DIM:]).astype(o_ref.dtype)


def _win_attn(src, table, *, row0, batch, seq, n_heads, q_blk0, k_blk0, v_blk0, halo_blks, win_blks, hp):
    tq = ATTN_TQ
    n_blks = seq // tq
    assert seq % tq == 0 and n_blks >= win_blks and row0 % seq == 0
    assert n_heads % hp == 0 and q_blk0 % hp == 0 and k_blk0 % hp == 0 and v_blk0 % hp == 0
    seq_blk0 = row0 // seq
    qrow0 = row0 // tq
    wh = hp * HEAD_DIM
    qb, kb, vb = q_blk0 // hp, k_blk0 // hp, v_blk0 // hp

    def variant(i):
        return i - _window_start_blk(i, halo_blks, win_blks, n_blks)

    kern = functools.partial(_win_attn_kernel, halo_blks=halo_blks, win_blks=win_blks,
                             n_blks=n_blks, axis=2)
    return pl.pallas_call(
        kern,
        grid=(batch, n_heads // hp, n_blks),
        in_specs=[pl.BlockSpec((tq, wh), lambda b, h, i: (qrow0 + b * n_blks + i, qb + h)),
                  pl.BlockSpec((seq, wh), lambda b, h, i: (seq_blk0 + b, kb + h), pipeline_mode=pl.Buffered(1)),
                  pl.BlockSpec((seq, wh), lambda b, h, i: (seq_blk0 + b, vb + h), pipeline_mode=pl.Buffered(1)),
                  pl.BlockSpec((hp, 1, tq, win_blks * tq), lambda b, h, i: (h, variant(i), 0, 0))],
        out_specs=pl.BlockSpec((tq, wh), lambda b, h, i: (b * n_blks + i, h)),
        out_shape=jax.ShapeDtypeStruct((batch * seq, n_heads * HEAD_DIM), BF16),
        compiler_params=_params("arbitrary", "arbitrary", "arbitrary"),
    )(src, src, src, table)


def _dilated_table():
    tq = ATTN_TQ
    halo = max(w // 2 for w, _ in DILATED_BRANCHES)
    halo_blks = halo // tq
    win_blks = 2 * halo_blks + 1
    v = jnp.arange(win_blks)[:, None, None]
    r = jnp.arange(tq)[None, :, None]
    c = jnp.arange(win_blks * tq)[None, None, :]
    delta = c - v * tq - r
    ad = jnp.abs(delta)
    cnt = jnp.zeros(delta.shape, F32)
    for window, dil in DILATED_BRANCHES:
        cnt = cnt + ((ad <= window // 2) & (delta % dil == 0)).astype(F32)
    slopes = 2.0 ** (-8.0 * jnp.arange(1, N_HEADS_A + 1, dtype=F32) / N_HEADS_A)
    bias = -slopes[:, None, None, None] * ad.astype(F32)[None] + jnp.log(jnp.maximum(cnt, 1.0))[None]
    table = jnp.where(cnt[None] > 0, bias * LOG2E, NEG)
    return table, halo_blks, win_blks


def _natten_table(rpb):
    tq = ATTN_TQ
    rows_per_tile = tq // GRID_W
    halo_blks = 1
    win_blks = 3
    assert NA_ROWS // 2 == rows_per_tile
    win_rows = win_blks * rows_per_tile
    qt = np.arange(win_blks)[:, None] * tq + np.arange(tq)[None, :]
    rq, qc = qt // GRID_W, qt % GRID_W
    kt = np.arange(win_blks * tq)
    rk, kc = kt // GRID_W, kt % GRID_W
    rs = np.clip(rq - NA_ROWS // 2, 0, win_rows - NA_ROWS)
    cs = np.clip(qc - NA_COLS // 2, 0, GRID_W - NA_COLS)
    valid = ((rk[None, None, :] >= rs[..., None]) & (rk[None, None, :] < rs[..., None] + NA_ROWS)
             & (kc[None, None, :] >= cs[..., None]) & (kc[None, None, :] < cs[..., None] + NA_COLS))
    n_ro, n_co = 2 * NA_ROWS - 1, 2 * NA_COLS - 1
    cols = np.arange(GRID_W)
    ci = np.clip(cols[None, :] - cols[:, None] + NA_COLS - 1, 0, n_co - 1)
    onehot = (np.arange(n_co)[:, None, None] == ci[None]).astype(np.float32).reshape(n_co, GRID_W * GRID_W)
    t1 = rpb.astype(F32)
    hi = t1.astype(BF16)
    mid = (t1 - hi.astype(F32)).astype(BF16)
    lo = (t1 - hi.astype(F32) - mid.astype(F32)).astype(BF16)
    oh = jnp.asarray(onehot, BF16)
    blocks = sum(jnp.einsum('hab,bq->haq', part, oh, preferred_element_type=F32) for part in (hi, mid, lo))
    blocks = blocks.reshape(-1, n_ro, GRID_W, GRID_W).transpose(0, 2, 1, 3)
    lo_pad = win_rows - NA_ROWS
    blocks = jnp.pad(blocks, ((0, 0), (0, 0), (lo_pad, lo_pad), (0, 0)))
    flat = blocks.reshape(blocks.shape[0], GRID_W, (n_ro + 2 * lo_pad) * GRID_W)
    first = [lo_pad + NA_ROWS - 1 - r for r in range(win_rows)]
    per_rq = [flat[:, :, a0 * GRID_W:(a0 + win_rows) * GRID_W] for a0 in first]
    bias = jnp.stack(per_rq, axis=1)
    bias = bias.reshape(-1, win_blks, tq, win_blks * tq)
    table = jnp.where(jnp.asarray(valid)[None], bias * LOG2E, NEG)
    return table, halo_blks, win_blks


GQA_TQ = 1024
GQA_TK = 2048
GQA_ROWS = 256


def _gqa_kernel(q_ref, k_ref, v_ref, o_ref, qs_ref, acc_ref, m_ref, *, rep, tk, rows):
    tq = q_ref.shape[0]
    seq = k_ref.shape[0]
    for r in range(rep):
        qs_ref[r * tq:(r + 1) * tq, :] = q_ref[:, r * HEAD_DIM:(r + 1) * HEAD_DIM]
    m_ref[...] = jnp.full(m_ref.shape, -jnp.inf, F32)
    acc_ref[...] = jnp.zeros(acc_ref.shape, F32)

    def body(c, carry):
        off = pl.multiple_of(c * tk, tk)
        k = k_ref[pl.ds(off, tk), :]
        v2 = _with_ones(v_ref[pl.ds(off, tk), :])
        for rb in range(rep * tq // rows):
            sl = slice(rb * rows, (rb + 1) * rows)
            s = lax.dot_general(qs_ref[sl, :], k, (((1,), (1,)), ((), ())), preferred_element_type=F32)
            tiles = _lane_tiles(s)
            m_prev = m_ref[sl, :]
            m_new = jnp.maximum(m_prev, jnp.max(functools.reduce(jnp.maximum, tiles), axis=-1, keepdims=True))
            alpha = jnp.exp2(m_prev - m_new)
            p = jnp.concatenate([jnp.exp2(t - m_new) for t in tiles], axis=1).astype(BF16)
            pv = jnp.dot(p, v2, preferred_element_type=F32)
            acc_ref[sl, :] = jnp.concatenate([alpha, alpha], axis=1) * acc_ref[sl, :] + pv
            m_ref[sl, :] = m_new
        return carry

    lax.fori_loop(0, seq // tk, body, 0)
    acc = acc_ref[...]
    o = acc[:, :HEAD_DIM] / acc[:, HEAD_DIM:]
    for r in range(rep):
        o_ref[:, r * HEAD_DIM:(r + 1) * HEAD_DIM] = o[r * tq:(r + 1) * tq].astype(o_ref.dtype)


def _gqa(qk, proj, *, row0, batch, seq, v_blk0):
    rep = N_HEADS_B // N_KV_B
    tq = GQA_TQ
    tk = _tile(seq, GQA_TK)
    n_blks = seq // tq
    assert seq % tq == 0 and row0 % seq == 0 and (rep * tq) % GQA_ROWS == 0
    seq_blk0 = row0 // seq
    qrow0 = row0 // tq
    wq = rep * HEAD_DIM
    return pl.pallas_call(
        functools.partial(_gqa_kernel, rep=rep, tk=tk, rows=GQA_ROWS),
        grid=(batch, N_KV_B, n_blks),
        in_specs=[pl.BlockSpec((tq, wq), lambda b, g, i: (qrow0 + b * n_blks + i, g)),
                  pl.BlockSpec((seq, HEAD_DIM), lambda b, g, i: (seq_blk0 + b, N_HEADS_B + g)),
                  pl.BlockSpec((seq, HEAD_DIM), lambda b, g, i: (seq_blk0 + b, v_blk0 + g))],
        out_specs=pl.BlockSpec((tq, wq), lambda b, g, i: (b * n_blks + i, g)),
        out_shape=jax.ShapeDtypeStruct((batch * seq, N_HEADS_B * HEAD_DIM), BF16),
        scratch_shapes=[pltpu.VMEM((rep * tq, HEAD_DIM), BF16),
                        pltpu.VMEM((rep * tq, 2 * HEAD_DIM), F32),
                        pltpu.VMEM((rep * tq, HEAD_DIM), F32)],
        compiler_params=_params("arbitrary", "arbitrary", "arbitrary"),
    )(qk, qk, proj)


def _out_ln_kernel(x_ref, *refs, widths, bounds, router):
    i = pl.program_id(0)
    n_groups = len(bounds) + 1
    n_act = len(widths)
    w_ref, g_ref, b_ref = refs[n_act * n_groups:n_act * n_groups + 3]
    rest = refs[n_act * n_groups + 3:]
    h = None
    row0 = 0
    for a, width in enumerate(widths):
        grp = refs[a * n_groups:(a + 1) * n_groups]
        val = grp[-1][...]
        for gi in reversed(range(n_groups - 1)):
            val = jnp.where(i < bounds[gi], grp[gi][...], val)
        part = jnp.dot(val, w_ref[row0:row0 + width, :], preferred_element_type=F32)
        h = part if h is None else h + part
        row0 += width
    y = _layer_norm(DN_ALPHA * x_ref[...] + h, g_ref[...], b_ref[...])
    if not router:
        (y_ref,) = rest
        y_ref[...] = y
        return
    wr_hi_ref, wr_lo_ref, br_ref, y_ref, r_ref = rest
    y_hi = y.astype(BF16)
    y_lo = (y - y_hi.astype(F32)).astype(BF16)
    y_ref[...] = y
    logits = (jnp.dot(y_hi, wr_hi_ref[...], preferred_element_type=F32)
              + jnp.dot(y_lo, wr_hi_ref[...], preferred_element_type=F32)
              + jnp.dot(y_hi, wr_lo_ref[...], preferred_element_type=F32)) + br_ref[...]
    lane = lax.broadcasted_iota(jnp.int32, logits.shape, 1).astype(F32)
    m1 = jnp.max(logits, axis=-1, keepdims=True)
    i1 = jnp.min(jnp.where(logits == m1, lane, float(ROUTER_LANES)), axis=-1, keepdims=True)
    rest_l = jnp.where(lane == i1, -jnp.inf, logits)
    m2 = jnp.max(rest_l, axis=-1, keepdims=True)
    i2 = jnp.min(jnp.where(rest_l == m2, lane, float(ROUTER_LANES)), axis=-1, keepdims=True)
    e2 = jnp.exp(m2 - m1)
    den = 1.0 + e2
    g1 = 1.0 / den
    g2 = e2 / den
    sel = ((lane == i1) | (lane == i2)).astype(F32)
    e = N_EXPERTS
    out = jnp.where(lane < e, sel, 0.0)
    out = jnp.where(lane == e, i1, out)
    out = jnp.where(lane == e + 1, i2, out)
    out = jnp.where(lane == e + 2, g1, out)
    out = jnp.where(lane == e + 3, g2, out)
    r_ref[...] = out


def _out_ln(x, acts, w_stack, layer, g, b, router=None):
    t, d = x.shape
    group_rows = [a.shape[0] for a in acts[0]]
    tm = _tile(math.gcd(*group_rows), 512)
    starts = np.cumsum([0] + [r // tm for r in group_rows])
    row = lambda i: (i, 0)
    const = lambda i: (0, 0)
    in_specs = [pl.BlockSpec((tm, d), row)]
    args = [x]
    for per_group in acts:
        for gi, a in enumerate(per_group):
            lo, n = int(starts[gi]), int(starts[gi + 1] - starts[gi])
            in_specs.append(pl.BlockSpec((tm, a.shape[1]),
                                         lambda i, lo=lo, n=n: (jnp.clip(i - lo, 0, n - 1), 0)))
            args.append(a)
    in_specs += [pl.BlockSpec((None,) + w_stack.shape[1:], lambda i: (layer, 0, 0), pipeline_mode=pl.Buffered(1)),
                 pl.BlockSpec((1, d), const), pl.BlockSpec((1, d), const)]
    args += [w_stack, g, b]
    out_specs = [pl.BlockSpec((tm, d), row)]
    out_shape = [jax.ShapeDtypeStruct((t, d), F32)]
    widths = tuple(per_group[0].shape[1] for per_group in acts)
    bounds = tuple(int(s) for s in starts[1:-1])
    if router is not None:
        in_specs += [pl.BlockSpec((d, ROUTER_LANES), const), pl.BlockSpec((d, ROUTER_LANES), const),
                     pl.BlockSpec((1, ROUTER_LANES), const)]
        out_specs.append(pl.BlockSpec((tm, ROUTER_LANES), row))
        out_shape.append(jax.ShapeDtypeStruct((t, ROUTER_LANES), F32))
        args += list(router)
    return pl.pallas_call(
        functools.partial(_out_ln_kernel, widths=widths, bounds=bounds, router=router is not None),
        grid=(t // tm,),
        in_specs=in_specs, out_specs=out_specs, out_shape=out_shape,
        compiler_params=_params("arbitrary"),
    )(*args)


def _swiglu_accumulate(acc_ref, xb, w1, w3, w2):
    h1 = jnp.dot(xb, w1, preferred_element_type=F32)
    h3 = jnp.dot(xb, w3, preferred_element_type=F32)
    h = (h1 * (1.0 / (1.0 + jnp.exp(-h1)))) * h3
    acc_ref[...] += jnp.dot(h.astype(BF16), w2, preferred_element_type=F32)


def _ffn_ln_kernel(x_ref, w1_ref, w3_ref, w2_ref, g_ref, b_ref, y_ref, yb_ref, xb_ref):
    j = pl.program_id(1)

    @pl.when(j == 0)
    def _():
        xb_ref[...] = x_ref[...].astype(BF16)
        y_ref[...] = jnp.zeros(y_ref.shape, F32)

    _swiglu_accumulate(y_ref, xb_ref[...], w1_ref[...], w3_ref[...], w2_ref[...])

    @pl.when(j == pl.num_programs(1) - 1)
    def _():
        y = _layer_norm(DN_ALPHA * x_ref[...] + y_ref[...], g_ref[...], b_ref[...])
        y_ref[...] = y
        yb_ref[...] = y.astype(BF16)


FFN_TM = 512


def _ffn_ln(x, w1, w3, w2, layer, g, b):
    t, d = x.shape
    f = w1.shape[2]
    tm, tf = _tile(t, FFN_TM), _tile(f, 512)
    return pl.pallas_call(
        _ffn_ln_kernel,
        grid=(t // tm, f // tf),
        in_specs=[pl.BlockSpec((tm, d), lambda i, j: (i, 0)),
                  pl.BlockSpec((None, d, tf), lambda i, j: (layer, 0, j)),
                  pl.BlockSpec((None, d, tf), lambda i, j: (layer, 0, j)),
                  pl.BlockSpec((None, tf, d), lambda i, j: (layer, j, 0)),
                  pl.BlockSpec((1, d), lambda i, j: (0, 0)),
                  pl.BlockSpec((1, d), lambda i, j: (0, 0))],
        out_specs=[pl.BlockSpec((tm, d), lambda i, j: (i, 0)),
                   pl.BlockSpec((tm, d), lambda i, j: (i, 0))],
        out_shape=[jax.ShapeDtypeStruct((t, d), F32), jax.ShapeDtypeStruct((t, d), BF16)],
        scratch_shapes=[pltpu.VMEM((tm, d), BF16)],
        compiler_params=_params("arbitrary", "arbitrary"),
    )(x, w1, w3, w2, g, b)


MOE_TM = 1024
DISPATCH_TM = 512
COMBINE_TM = 256
DMA_ISSUE_UNROLL = 8
ZERO_ROWS = 256


def _dispatch_kernel(p1_ref, p2_ref, ends_ref, x_ref, xs_hbm, zbuf, sem, zsem, *, tm, group_tm):
    i = pl.program_id(0)
    base = i * tm

    @pl.when(i == 0)
    def _():
        zbuf[...] = jnp.zeros(zbuf.shape, zbuf.dtype)
        zrows = zbuf.shape[0]

        def clear_tile(row0):
            for c in range(group_tm // zrows):
                start = pl.multiple_of(row0 + c * zrows, zrows)
                pltpu.make_async_copy(zbuf, xs_hbm.at[pl.ds(start, zrows)], zsem).start()
            for c in range(group_tm // zrows):
                pltpu.make_async_copy(zbuf, xs_hbm.at[pl.ds(0, zrows)], zsem).wait()

        for e in range(N_EXPERTS):
            end = ends_ref[e]
            prev_end = ends_ref[e - 1] if e else 0

            @pl.when(end > prev_end)
            def _():
                clear_tile(end - group_tm)

        for k in range(N_EXPERTS):
            tail = ends_ref[N_EXPERTS - 1] + k * group_tm

            @pl.when(tail < xs_hbm.shape[0])
            def _():
                clear_tile(tail)

    def issue(t, carry):
        src = x_ref.at[pl.ds(t, 1)]
        pltpu.make_async_copy(src, xs_hbm.at[pl.ds(p1_ref[base + t], 1)], sem).start()
        pltpu.make_async_copy(src, xs_hbm.at[pl.ds(p2_ref[base + t], 1)], sem).start()
        return carry

    lax.fori_loop(0, tm, issue, 0, unroll=DMA_ISSUE_UNROLL)
    for _ in range(TOP_K):
        pltpu.make_async_copy(x_ref, xs_hbm.at[pl.ds(0, tm)], sem).wait()


def _dispatch(x, p1, p2, ends, n_rows):
    t, d = x.shape
    tm = _tile(t, DISPATCH_TM)
    grid_spec = pltpu.PrefetchScalarGridSpec(
        num_scalar_prefetch=3,
        grid=(t // tm,),
        in_specs=[pl.BlockSpec((tm, d), lambda i, a, c, e: (i, 0))],
        out_specs=pl.BlockSpec(memory_space=pl.ANY),
        scratch_shapes=[pltpu.VMEM((ZERO_ROWS, d), x.dtype), pltpu.SemaphoreType.DMA(()),
                        pltpu.SemaphoreType.DMA(())])
    return pl.pallas_call(
        functools.partial(_dispatch_kernel, tm=tm, group_tm=MOE_TM),
        grid_spec=grid_spec,
        out_shape=jax.ShapeDtypeStruct((n_rows, d), x.dtype),
        compiler_params=_params("arbitrary"),
    )(p1, p2, ends, x)


def _moe_expert_kernel(te_ref, last_ref, xs_ref, w1_ref, w3_ref, w2_ref, y_ref, xb_ref):
    i = pl.program_id(0)
    j = pl.program_id(1)
    active = i <= last_ref[0]

    @pl.when(j == 0)
    def _():
        y_ref[...] = jnp.zeros(y_ref.shape, F32)

    @pl.when(active)
    def _():
        @pl.when(j == 0)
        def _():
            xb_ref[...] = xs_ref[...].astype(BF16)

        _swiglu_accumulate(y_ref, xb_ref[...], w1_ref[...].astype(BF16), w3_ref[...].astype(BF16),
                           w2_ref[...].astype(BF16))


def _moe_experts(xs, tile_expert, last_tile, w1, w3, w2, layer):
    p, d = xs.shape
    f = w1.shape[3]
    tm, tf = MOE_TM, _tile(f, 256)
    grid_spec = pltpu.PrefetchScalarGridSpec(
        num_scalar_prefetch=2,
        grid=(p // tm, f // tf),
        in_specs=[pl.BlockSpec((tm, d), lambda i, j, te, ac: (jnp.minimum(i, ac[0]), 0)),
                  pl.BlockSpec((None, None, d, tf), lambda i, j, te, ac: (layer, te[i], 0, j)),
                  pl.BlockSpec((None, None, d, tf), lambda i, j, te, ac: (layer, te[i], 0, j)),
                  pl.BlockSpec((None, None, tf, d), lambda i, j, te, ac: (layer, te[i], j, 0))],
        out_specs=pl.BlockSpec((tm, d), lambda i, j, te, ac: (i, 0)),
        scratch_shapes=[pltpu.VMEM((tm, d), BF16)])
    return pl.pallas_call(
        _moe_expert_kernel,
        grid_spec=grid_spec,
        out_shape=jax.ShapeDtypeStruct((p, d), F32),
        compiler_params=_params("arbitrary", "arbitrary"),
    )(tile_expert, last_tile, xs, w1, w3, w2)


def _combine_ln_kernel(p1_ref, p2_ref, x_ref, gt_ref, g_ref, b_ref, ys_hbm, *refs, tm, n_out, bounds):
    out_refs = refs[:n_out]
    buf, sem = refs[n_out:]
    i = pl.program_id(0)
    n = pl.num_programs(0)

    def issue(tile, slot):
        base = tile * tm

        def body(t, carry):
            pltpu.make_async_copy(ys_hbm.at[pl.ds(p1_ref[base + t], 1)],
                                  buf.at[slot, pl.ds(t, 1)], sem.at[slot]).start()
            pltpu.make_async_copy(ys_hbm.at[pl.ds(p2_ref[base + t], 1)],
                                  buf.at[slot, pl.ds(tm + t, 1)], sem.at[slot]).start()
            return carry

        lax.fori_loop(0, tm, body, 0, unroll=DMA_ISSUE_UNROLL)

    @pl.when(i == 0)
    def _():
        issue(0, 0)

    @pl.when(i + 1 < n)
    def _():
        issue(i + 1, (i + 1) % 2)

    slot = i % 2
    pltpu.make_async_copy(ys_hbm.at[pl.ds(0, 2 * tm)], buf.at[slot], sem.at[slot]).wait()
    g1 = gt_ref[:, 0:1]
    g2 = gt_ref[:, 1:2]
    f = g1 * buf[slot, pl.ds(0, tm), :] + g2 * buf[slot, pl.ds(tm, tm), :]
    y = _layer_norm(DN_ALPHA * x_ref[...] + f, g_ref[...], b_ref[...])
    if bounds is None:
        y_ref, yb_ref = out_refs
        y_ref[...] = y
        yb_ref[...] = y.astype(BF16)
    else:
        edges = (0,) + bounds + (None,)
        for gi, o_ref in enumerate(out_refs):
            lo, hi = edges[gi], edges[gi + 1]
            in_group = (i >= lo) if hi is None else ((i >= lo) & (i < hi))

            @pl.when(in_group)
            def _(o_ref=o_ref):
                o_ref[...] = y


def _combine_ln(x, ys, p1, p2, gates, g, b, final_group_rows=None):
    t, d = x.shape
    rows = [t] if final_group_rows is None else list(final_group_rows)
    tm = _tile(math.gcd(*rows) if len(rows) > 1 else t, COMBINE_TM)
    row = lambda i, a, c: (i, 0)
    const = lambda i, a, c: (0, 0)
    if final_group_rows is None:
        bounds = None
        out_specs = [pl.BlockSpec((tm, d), row), pl.BlockSpec((tm, d), row)]
        out_shape = [jax.ShapeDtypeStruct((t, d), F32), jax.ShapeDtypeStruct((t, d), BF16)]
    else:
        starts = np.cumsum([0] + [r // tm for r in rows])
        bounds = tuple(int(s) for s in starts[1:-1])
        out_specs, out_shape = [], []
        for gi, r in enumerate(rows):
            lo, n = int(starts[gi]), int(starts[gi + 1] - starts[gi])
            out_specs.append(pl.BlockSpec((tm, d), lambda i, a, c, lo=lo, n=n: (jnp.clip(i - lo, 0, n - 1), 0)))
            out_shape.append(jax.ShapeDtypeStruct((r, d), F32))
    grid_spec = pltpu.PrefetchScalarGridSpec(
        num_scalar_prefetch=2,
        grid=(t // tm,),
        in_specs=[pl.BlockSpec((tm, d), row), pl.BlockSpec((tm, LANES), row),
                  pl.BlockSpec((1, d), const), pl.BlockSpec((1, d), const),
                  pl.BlockSpec(memory_space=pl.ANY)],
        out_specs=out_specs,
        scratch_shapes=[pltpu.VMEM((2, 2 * tm, d), F32), pltpu.SemaphoreType.DMA((2,))])
    return pl.pallas_call(
        functools.partial(_combine_ln_kernel, tm=tm, n_out=len(out_specs), bounds=bounds),
        grid_spec=grid_spec,
        out_shape=out_shape,
        compiler_params=_params("arbitrary"),
    )(p1, p2, x, gates, g, b, ys)


def _moe_routing(r, tm):
    t = r.shape[0]
    e = N_EXPERTS
    sel = r[:, :e].astype(jnp.int32)
    i1 = r[:, e].astype(jnp.int32)
    i2 = r[:, e + 1].astype(jnp.int32)
    cnt = jnp.cumsum(sel, axis=0)
    rank = cnt - sel
    padded = ((cnt[-1] + tm - 1) // tm) * tm
    ends = jnp.cumsum(padded)
    pos = (ends - padded)[None, :] + rank
    lane = jnp.arange(e, dtype=jnp.int32)[None, :]
    p1 = jnp.sum(jnp.where(lane == i1[:, None], pos, 0), axis=1).astype(jnp.int32)
    p2 = jnp.sum(jnp.where(lane == i2[:, None], pos, 0), axis=1).astype(jnp.int32)
    n_rows = TOP_K * t + e * tm
    tile_start = jnp.arange(n_rows // tm, dtype=jnp.int32) * tm
    tile_expert = jnp.sum((tile_start[:, None] >= ends[None, :]).astype(jnp.int32), axis=1)
    tile_expert = jnp.minimum(tile_expert, e - 1).astype(jnp.int32)
    last_tile = (ends[-1:] // tm - 1).astype(jnp.int32)
    return p1, p2, ends.astype(jnp.int32), tile_expert, last_tile, n_rows


def _deinterleave_perm():
    half = HEAD_DIM // 2
    return np.concatenate([np.arange(half) * 2, np.arange(half) * 2 + 1])


def _rope_tables(groups):
    pos = np.concatenate([np.tile(np.arange(seq), batch) for batch, seq in groups])
    pos = jnp.asarray(pos, jnp.int32)
    row = (pos // GRID_W).astype(F32)
    col = (pos % GRID_W).astype(F32)
    axis_dim = HEAD_DIM // 2
    inv = ROPE_THETA ** (-jnp.arange(0, axis_dim, 2, dtype=F32) / axis_dim)
    ang = jnp.concatenate([row[:, None] * inv, col[:, None] * inv], axis=-1)
    cos, sin = jnp.cos(ang), jnp.sin(ang)
    return jnp.concatenate([cos, cos], axis=-1), jnp.concatenate([-sin, sin], axis=-1)


def _group_rows(groups):
    out, row0 = [], 0
    for batch, seq in groups:
        out.append((row0, batch, seq))
        row0 += batch * seq
    return out


def _prep_w_in_even(w_in):
    perm = _deinterleave_perm()
    n_qk = N_HEADS_B + N_KV_B
    c0 = 3 * N_HEADS_A * HEAD_DIM
    pmat = np.zeros((HEAD_DIM, HEAD_DIM), np.float32)
    pmat[perm, np.arange(HEAD_DIM)] = 1.0
    w_in_b = w_in.astype(BF16)
    lead = w_in.shape[:2]
    w_qk = w_in_b[:, :, c0:c0 + n_qk * HEAD_DIM].reshape(*lead, n_qk, HEAD_DIM)
    w_qk = jnp.einsum('ldhk,kn->ldhn', w_qk, jnp.asarray(pmat, BF16), preferred_element_type=F32)
    return jnp.concatenate([w_in_b[:, :, :c0], w_qk.astype(BF16).reshape(*lead, n_qk * HEAD_DIM),
                            w_in_b[:, :, c0 + n_qk * HEAD_DIM:]], axis=2)


def _even_layer(x, xb, groups, layer, w_in_b, qk_gain, w_out_b, w1_b, w3_b, w2_b, ln_g, ln_b, rope, dil):
    w_a = N_HEADS_A * HEAD_DIM
    w_bq = N_HEADS_B * HEAD_DIM
    w_bkv = N_KV_B * HEAD_DIM
    perm = _deinterleave_perm()
    col_scale = jnp.concatenate([jnp.full((w_a,), SCALE * LOG2E, F32),
                                 jnp.ones((w_in_b.shape[2] - w_a,), F32)])[None, :]
    proj = _project(xb, w_in_b, layer, col_scale)

    n_b = N_HEADS_B + N_KV_B
    gains = jnp.concatenate([jnp.tile(qk_gain[0][perm][None], (N_HEADS_B, 1)),
                             jnp.tile(qk_gain[1][perm][None], (N_KV_B, 1))]).astype(F32)
    scales = jnp.concatenate([jnp.full((N_HEADS_B, HEAD_DIM), SCALE * LOG2E, F32),
                              jnp.ones((N_KV_B, HEAD_DIM), F32)])
    qk = _qk_prep(proj, gains.reshape(2, n_b // 2, HEAD_DIM), scales.reshape(2, n_b // 2, HEAD_DIM),
                  rope[0], rope[1], 3 * w_a)

    table, halo_blks, win_blks = dil
    oa, ob = [], []
    v_blk0 = (3 * w_a + w_bq + w_bkv) // HEAD_DIM
    for row0, batch, seq in _group_rows(groups):
        oa.append(_win_attn(proj, table, row0=row0, batch=batch, seq=seq, n_heads=N_HEADS_A,
                            q_blk0=0, k_blk0=N_HEADS_A, v_blk0=2 * N_HEADS_A,
                            halo_blks=halo_blks, win_blks=win_blks, hp=HEADS_PER_STEP_A))
        ob.append(_gqa(qk, proj, row0=row0, batch=batch, seq=seq, v_blk0=v_blk0))
    (x,) = _out_ln(x, [oa, ob], w_out_b, layer, ln_g[0][None], ln_b[0][None])
    return _ffn_ln(x, w1_b, w3_b, w2_b, layer, ln_g[1][None], ln_b[1][None])


def _odd_layer(x, xb, groups, layer, w_in_b, rpb, w_out_b, w_router, b_router, w1_b, w3_b, w2_b, ln_g, ln_b,
               final):
    w_c = N_HEADS_C * HEAD_DIM
    col_scale = jnp.concatenate([jnp.full((w_c,), SCALE * LOG2E, F32), jnp.ones((2 * w_c,), F32)])[None, :]
    proj = _project(xb, w_in_b, layer, col_scale)
    table, halo_blks, win_blks = _natten_table(rpb)
    o = []
    for row0, batch, seq in _group_rows(groups):
        o.append(_win_attn(proj, table, row0=row0, batch=batch, seq=seq, n_heads=N_HEADS_C,
                           q_blk0=0, k_blk0=N_HEADS_C, v_blk0=2 * N_HEADS_C,
                           halo_blks=halo_blks, win_blks=win_blks, hp=HEADS_PER_STEP_C))

    wr = jnp.pad(w_router.astype(F32), ((0, 0), (0, ROUTER_LANES - N_EXPERTS)))
    wr_hi = wr.astype(BF16)
    wr_lo = (wr - wr_hi.astype(F32)).astype(BF16)
    br = jnp.concatenate([b_router.astype(F32), jnp.full((ROUTER_LANES - N_EXPERTS,), NEG, F32)])[None, :]
    x, r = _out_ln(x, [o], w_out_b, layer, ln_g[0][None], ln_b[0][None], router=(wr_hi, wr_lo, br))

    p1, p2, ends, tile_expert, last_tile, n_rows = _moe_routing(r, MOE_TM)
    xs = _dispatch(x, p1, p2, ends, n_rows)
    ys = _moe_experts(xs, tile_expert, last_tile, w1_b, w3_b, w2_b, layer)
    gates = jnp.pad(r[:, N_EXPERTS + 2:N_EXPERTS + 4], ((0, 0), (0, LANES - 2)))
    final_rows = [batch * seq for batch, seq in groups] if final else None
    return _combine_ln(x, ys, p1, p2, gates, ln_g[1][None], ln_b[1][None], final_group_rows=final_rows)


def kernel(x_prompt, x_sample, ln_g, ln_b, w_in_even, qk_gain_b, w_out_even, ffn_w1, ffn_w3, ffn_w2,
           w_in_odd, rpb, w_out_odd, w_router, b_router, moe_w1, moe_w3, moe_w2):
    d = x_prompt.shape[-1]
    groups = [(x_prompt.shape[0], x_prompt.shape[1]), (x_sample.shape[0], x_sample.shape[1])]
    x = jnp.concatenate([x_prompt.reshape(-1, d), x_sample.reshape(-1, d)], axis=0)
    xb = x.astype(BF16)
    rope = _rope_tables(groups)
    dil = _dilated_table()
    even_w = (_prep_w_in_even(w_in_even), w_out_even.astype(BF16),
              ffn_w1.astype(BF16), ffn_w3.astype(BF16), ffn_w2.astype(BF16))
    odd_w = (w_in_odd.astype(BF16), w_out_odd.astype(BF16), moe_w1, moe_w3, moe_w2)
    depth = ln_g.shape[0]
    for i in range(depth):
        j = i // 2
        if i % 2 == 0:
            x, xb = _even_layer(x, xb, groups, j, even_w[0], qk_gain_b[j], even_w[1], *even_w[2:],
                                ln_g[i], ln_b[i], rope, dil)
        else:
            x, xb = _odd_layer(x, xb, groups, j, odd_w[0], rpb[j], odd_w[1], w_router[j], b_router[j],
                               *odd_w[2:], ln_g[i], ln_b[i], final=(i == depth - 1))
    if depth % 2 == 0:
        return (x.reshape(x_prompt.shape), xb.reshape(x_sample.shape))
    n_p = x_prompt.shape[0] * x_prompt.shape[1]
    return (x[:n_p].reshape(x_prompt.shape), x[n_p:].reshape(x_sample.shape))
```

```python
import functools
import math

import numpy as np
import jax
import jax.numpy as jnp
from jax import lax
from jax.experimental import pallas as pl
from jax.experimental.pallas import tpu as pltpu

HEAD_DIM = 128
GRID_W = 64
N_HEADS_A = 6
DILATED_BRANCHES = ((128, 1), (512, 4), (2048, 16))
N_HEADS_B = 10
N_KV_B = 2
N_HEADS_C = 16
NA_ROWS = 8
NA_COLS = 16
N_EXPERTS = 8
TOP_K = 2
DEPTH = 4
ROPE_THETA = 10000.0
LN_EPS = 1e-5
QK_EPS = 1e-6
NEG = -1e30
SCALE = HEAD_DIM ** -0.5
LOG2E = math.log2(math.e)
DN_ALPHA = (2 * DEPTH) ** 0.25

V7X_VMEM_BYTES = 64 * 2 ** 20
VMEM_LIMIT = V7X_VMEM_BYTES - 8 * 2 ** 20
LANES = 128
ATTN_TQ = 256
ATTN_ROWS = 128
HEADS_PER_STEP_A = 3
HEADS_PER_STEP_C = 4
ROUTER_LANES = LANES

F32 = jnp.float32
BF16 = jnp.bfloat16


def _params(*sem):
    return pltpu.CompilerParams(dimension_semantics=sem, vmem_limit_bytes=VMEM_LIMIT)


def _tile(n, pref):
    if n <= pref:
        return n
    t = (pref // LANES) * LANES
    while t >= LANES:
        if n % t == 0:
            return t
        t -= LANES
    return n


def _lane_tiles(s):
    return [s[:, t * LANES:(t + 1) * LANES] for t in range(s.shape[1] // LANES)]


def _with_ones(v):
    return jnp.concatenate([v, jnp.ones_like(v)], axis=1)


def _proj_kernel(x_ref, w_ref, cs_ref, o_ref):
    acc = jnp.dot(x_ref[...], w_ref[...], preferred_element_type=F32)
    o_ref[...] = (acc * cs_ref[...]).astype(o_ref.dtype)


def _project(xb, w_stack, layer, col_scale):
    t, k = xb.shape
    n = w_stack.shape[2]
    tm, tn = _tile(t, 1024), _tile(n, 2048)
    return pl.pallas_call(
        _proj_kernel,
        grid=(t // tm, n // tn),
        in_specs=[pl.BlockSpec((tm, k), lambda i, j: (i, 0)),
                  pl.BlockSpec((None, k, tn), lambda i, j: (layer, 0, j)),
                  pl.BlockSpec((1, tn), lambda i, j: (0, j))],
        out_specs=pl.BlockSpec((tm, tn), lambda i, j: (i, j)),
        out_shape=jax.ShapeDtypeStruct((t, n), BF16),
        compiler_params=_params("arbitrary", "arbitrary"),
    )(xb, w_stack, col_scale)


def _layer_norm(z, g, b):
    mu = jnp.mean(z, axis=-1, keepdims=True)
    zc = z - mu
    var = jnp.mean(zc * zc, axis=-1, keepdims=True)
    return zc * lax.rsqrt(var + LN_EPS) * g + b


def _qk_prep_kernel(p_ref, g_ref, sc_ref, cos_ref, sin_ref, o_ref, *, heads):
    c = cos_ref[...]
    s = sin_ref[...]
    for r in range(heads):
        x = p_ref[:, r * HEAD_DIM:(r + 1) * HEAD_DIM].astype(F32)
        ms = jnp.mean(x * x, axis=-1, keepdims=True)
        xn = x * lax.rsqrt(ms + QK_EPS) * g_ref[0, r:r + 1, :]
        y = xn * c + pltpu.roll(xn, HEAD_DIM // 2, 1) * s
        o_ref[:, r * HEAD_DIM:(r + 1) * HEAD_DIM] = (y * sc_ref[0, r:r + 1, :]).astype(o_ref.dtype)


def _qk_prep(proj, gains, scales, cos_t, sin_t, col0):
    t = proj.shape[0]
    n_heads = N_HEADS_B + N_KV_B
    half = n_heads // 2
    wblk = half * HEAD_DIM
    assert col0 % wblk == 0
    tm = _tile(t, 512)
    return pl.pallas_call(
        functools.partial(_qk_prep_kernel, heads=half),
        grid=(t // tm, 2),
        in_specs=[pl.BlockSpec((tm, wblk), lambda i, j: (i, col0 // wblk + j)),
                  pl.BlockSpec((1, half, HEAD_DIM), lambda i, j: (j, 0, 0)),
                  pl.BlockSpec((1, half, HEAD_DIM), lambda i, j: (j, 0, 0)),
                  pl.BlockSpec((tm, HEAD_DIM), lambda i, j: (i, 0)),
                  pl.BlockSpec((tm, HEAD_DIM), lambda i, j: (i, 0))],
        out_specs=pl.BlockSpec((tm, wblk), lambda i, j: (i, j)),
        out_shape=jax.ShapeDtypeStruct((t, n_heads * HEAD_DIM), BF16),
        compiler_params=_params("arbitrary", "arbitrary"),
    )(proj, gains, scales, cos_t, sin_t)


def _window_start_blk(i, halo_blks, win_blks, n_blks):
    return jnp.clip(i - halo_blks, 0, n_blks - win_blks)


def _win_attn_kernel(q_ref, k_ref, v_ref, tb_ref, o_ref, *, halo_blks, win_blks, n_blks, axis):
    i = pl.program_id(axis)
    tq = q_ref.shape[0]
    w = win_blks * tq
    start = pl.multiple_of(_window_start_blk(i, halo_blks, win_blks, n_blks) * tq, tq)
    hp = tb_ref.shape[0]
    for hd in range(hp):
        cols = slice(hd * HEAD_DIM, (hd + 1) * HEAD_DIM)
        k = k_ref[pl.ds(start, w), cols]
        v2 = _with_ones(v_ref[pl.ds(start, w), cols])
        for rb in range(tq // ATTN_ROWS):
            sl = slice(rb * ATTN_ROWS, (rb + 1) * ATTN_ROWS)
            s = lax.dot_general(q_ref[sl, cols], k, (((1,), (1,)), ((), ())), preferred_element_type=F32)
            tiles = _lane_tiles(s + tb_ref[hd, 0, sl, :])
            m = jnp.max(functools.reduce(jnp.maximum, tiles), axis=-1, keepdims=True)
            p = jnp.concatenate([jnp.exp2(t - m) for t in tiles], axis=1).astype(BF16)
            pv = jnp.dot(p, v2, preferred_element_type=F32)
            o_ref[sl, cols] = (pv[:, :HEAD_DIM] / pv[:, HEAD_DIM:]).astype(o_ref.dtype)


def _win_attn(src, table, *, row0, batch, seq, n_heads, q_blk0, k_blk0, v_blk0, halo_blks, win_blks, hp):
    tq = ATTN_TQ
    n_blks = seq // tq
    assert seq % tq == 0 and n_blks >= win_blks and row0 % seq == 0
    assert n_heads % hp == 0 and q_blk0 % hp == 0 and k_blk0 % hp == 0 and v_blk0 % hp == 0
    seq_blk0 = row0 // seq
    qrow0 = row0 // tq
    wh = hp * HEAD_DIM
    qb, kb, vb = q_blk0 // hp, k_blk0 // hp, v_blk0 // hp

    def variant(i):
        return i - _window_start_blk(i, halo_blks, win_blks, n_blks)

    kern = functools.partial(_win_attn_kernel, halo_blks=halo_blks, win_blks=win_blks,
                             n_blks=n_blks, axis=2)
    return pl.pallas_call(
        kern,
        grid=(batch, n_heads // hp, n_blks),
        in_specs=[pl.BlockSpec((tq, wh), lambda b, h, i: (qrow0 + b * n_blks + i, qb + h)),
                  pl.BlockSpec((seq, wh), lambda b, h, i: (seq_blk0 + b, kb + h), pipeline_mode=pl.Buffered(1)),
                  pl.BlockSpec((seq, wh), lambda b, h, i: (seq_blk0 + b, vb + h), pipeline_mode=pl.Buffered(1)),
                  pl.BlockSpec((hp, 1, tq, win_blks * tq), lambda b, h, i: (h, variant(i), 0, 0))],
        out_specs=pl.BlockSpec((tq, wh), lambda b, h, i: (b * n_blks + i, h)),
        out_shape=jax.ShapeDtypeStruct((batch * seq, n_heads * HEAD_DIM), BF16),
        compiler_params=_params("arbitrary", "arbitrary", "arbitrary"),
    )(src, src, src, table)


def _dilated_table():
    tq = ATTN_TQ
    halo = max(w // 2 for w, _ in DILATED_BRANCHES)
    halo_blks = halo // tq
    win_blks = 2 * halo_blks + 1
    v = jnp.arange(win_blks)[:, None, None]
    r = jnp.arange(tq)[None, :, None]
    c = jnp.arange(win_blks * tq)[None, None, :]
    delta = c - v * tq - r
    ad = jnp.abs(delta)
    cnt = jnp.zeros(delta.shape, F32)
    for window, dil in DILATED_BRANCHES:
        cnt = cnt + ((ad <= window // 2) & (delta % dil == 0)).astype(F32)
    slopes = 2.0 ** (-8.0 * jnp.arange(1, N_HEADS_A + 1, dtype=F32) / N_HEADS_A)
    bias = -slopes[:, None, None, None] * ad.astype(F32)[None] + jnp.log(jnp.maximum(cnt, 1.0))[None]
    table = jnp.where(cnt[None] > 0, bias * LOG2E, NEG)
    return table, halo_blks, win_blks


def _natten_table(rpb):
    tq = ATTN_TQ
    rows_per_tile = tq // GRID_W
    halo_blks = 1
    win_blks = 3
    assert NA_ROWS // 2 == rows_per_tile
    win_rows = win_blks * rows_per_tile
    qt = np.arange(win_blks)[:, None] * tq + np.arange(tq)[None, :]
    rq, qc = qt // GRID_W, qt % GRID_W
    kt = np.arange(win_blks * tq)
    rk, kc = kt // GRID_W, kt % GRID_W
    rs = np.clip(rq - NA_ROWS // 2, 0, win_rows - NA_ROWS)
    cs = np.clip(qc - NA_COLS // 2, 0, GRID_W - NA_COLS)
    valid = ((rk[None, None, :] >= rs[..., None]) & (rk[None, None, :] < rs[..., None] + NA_ROWS)
             & (kc[None, None, :] >= cs[..., None]) & (kc[None, None, :] < cs[..., None] + NA_COLS))
    n_ro, n_co = 2 * NA_ROWS - 1, 2 * NA_COLS - 1
    cols = np.arange(GRID_W)
    ci = np.clip(cols[None, :] - cols[:, None] + NA_COLS - 1, 0, n_co - 1)
    onehot = (np.arange(n_co)[:, None, None] == ci[None]).astype(np.float32).reshape(n_co, GRID_W * GRID_W)
    t1 = rpb.astype(F32)
    hi = t1.astype(BF16)
    mid = (t1 - hi.astype(F32)).astype(BF16)
    lo = (t1 - hi.astype(F32) - mid.astype(F32)).astype(BF16)
    oh = jnp.asarray(onehot, BF16)
    blocks = sum(jnp.einsum('hab,bq->haq', part, oh, preferred_element_type=F32) for part in (hi, mid, lo))
    blocks = blocks.reshape(-1, n_ro, GRID_W, GRID_W).transpose(0, 2, 1, 3)
    lo_pad = win_rows - NA_ROWS
    blocks = jnp.pad(blocks, ((0, 0), (0, 0), (lo_pad, lo_pad), (0, 0)))
    flat = blocks.reshape(blocks.shape[0], GRID_W, (n_ro + 2 * lo_pad) * GRID_W)
    first = [lo_pad + NA_ROWS - 1 - r for r in range(win_rows)]
    per_rq = [flat[:, :, a0 * GRID_W:(a0 + win_rows) * GRID_W] for a0 in first]
    bias = jnp.stack(per_rq, axis=1)
    bias = bias.reshape(-1, win_blks, tq, win_blks * tq)
    table = jnp.where(jnp.asarray(valid)[None], bias * LOG2E, NEG)
    return table, halo_blks, win_blks


GQA_TQ = 1024
GQA_TK = 2048
GQA_ROWS = 256


def _gqa_kernel(q_ref, k_ref, v_ref, o_ref, qs_ref, acc_ref, m_ref, *, rep, tk, rows):
    tq = q_ref.shape[0]
    seq = k_ref.shape[0]
    for r in range(rep):
        qs_ref[r * tq:(r + 1) * tq, :] = q_ref[:, r * HEAD_DIM:(r + 1) * HEAD_DIM]
    m_ref[...] = jnp.full(m_ref.shape, -jnp.inf, F32)
    acc_ref[...] = jnp.zeros(acc_ref.shape, F32)

    def body(c, carry):
        off = pl.multiple_of(c * tk, tk)
        k = k_ref[pl.ds(off, tk), :]
        v2 = _with_ones(v_ref[pl.ds(off, tk), :])
        for rb in range(rep * tq // rows):
            sl = slice(rb * rows, (rb + 1) * rows)
            s = lax.dot_general(qs_ref[sl, :], k, (((1,), (1,)), ((), ())), preferred_element_type=F32)
            tiles = _lane_tiles(s)
            m_prev = m_ref[sl, :]
            m_new = jnp.maximum(m_prev, jnp.max(functools.reduce(jnp.maximum, tiles), axis=-1, keepdims=True))
            alpha = jnp.exp2(m_prev - m_new)
            p = jnp.concatenate([jnp.exp2(t - m_new) for t in tiles], axis=1).astype(BF16)
            pv = jnp.dot(p, v2, preferred_element_type=F32)
            acc_ref[sl, :] = jnp.concatenate([alpha, alpha], axis=1) * acc_ref[sl, :] + pv
            m_ref[sl, :] = m_new
        return carry

    lax.fori_loop(0, seq // tk, body, 0)
    acc = acc_ref[...]
    o = acc[:, :HEAD_DIM] / acc[:, HEAD_DIM:]
    for r in range(rep):
        o_ref[:, r * HEAD_DIM:(r + 1) * HEAD_DIM] = o[r * tq:(r + 1) * tq].astype(o_ref.dtype)


def _gqa(qk, proj, *, row0, batch, seq, v_blk0):
    rep = N_HEADS_B // N_KV_B
    tq = GQA_TQ
    tk = _tile(seq, GQA_TK)
    n_blks = seq // tq
    assert seq % tq == 0 and row0 % seq == 0 and (rep * tq) % GQA_ROWS == 0
    seq_blk0 = row0 // seq
    qrow0 = row0 // tq
    wq = rep * HEAD_DIM
    return pl.pallas_call(
        functools.partial(_gqa_kernel, rep=rep, tk=tk, rows=GQA_ROWS),
        grid=(batch, N_KV_B, n_blks),
        in_specs=[pl.BlockSpec((tq, wq), lambda b, g, i: (qrow0 + b * n_blks + i, g)),
                  pl.BlockSpec((seq, HEAD_DIM), lambda b, g, i: (seq_blk0 + b, N_HEADS_B + g)),
                  pl.BlockSpec((seq, HEAD_DIM), lambda b, g, i: (seq_blk0 + b, v_blk0 + g))],
        out_specs=pl.BlockSpec((tq, wq), lambda b, g, i: (b * n_blks + i, g)),
        out_shape=jax.ShapeDtypeStruct((batch * seq, N_HEADS_B * HEAD_DIM), BF16),
        scratch_shapes=[pltpu.VMEM((rep * tq, HEAD_DIM), BF16),
                        pltpu.VMEM((rep * tq, 2 * HEAD_DIM), F32),
                        pltpu.VMEM((rep * tq, HEAD_DIM), F32)],
        compiler_params=_params("arbitrary", "arbitrary", "arbitrary"),
    )(qk, qk, proj)


def _out_ln_kernel(x_ref, *refs, widths, bounds, router):
    i = pl.program_id(0)
    n_groups = len(bounds) + 1
    n_act = len(widths)
    w_ref, g_ref, b_ref = refs[n_act * n_groups:n_act * n_groups + 3]
    rest = refs[n_act * n_groups + 3:]
    h = None
    row0 = 0
    for a, width in enumerate(widths):
        grp = refs[a * n_groups:(a + 1) * n_groups]
        val = grp[-1][...]
        for gi in reversed(range(n_groups - 1)):
            val = jnp.where(i < bounds[gi], grp[gi][...], val)
        part = jnp.dot(val, w_ref[row0:row0 + width, :], preferred_element_type=F32)
        h = part if h is None else h + part
        row0 += width
    y = _layer_norm(DN_ALPHA * x_ref[...] + h, g_ref[...], b_ref[...])
    if not router:
        (y_ref,) = rest
        y_ref[...] = y
        return
    wr_hi_ref, wr_lo_ref, br_ref, y_ref, r_ref = rest
    y_hi = y.astype(BF16)
    y_lo = (y - y_hi.astype(F32)).astype(BF16)
    y_ref[...] = y
    logits = (jnp.dot(y_hi, wr_hi_ref[...], preferred_element_type=F32)
              + jnp.dot(y_lo, wr_hi_ref[...], preferred_element_type=F32)
              + jnp.dot(y_hi, wr_lo_ref[...], preferred_element_type=F32)) + br_ref[...]
    lane = lax.broadcasted_iota(jnp.int32, logits.shape, 1).astype(F32)
    m1 = jnp.max(logits, axis=-1, keepdims=True)
    i1 = jnp.min(jnp.where(logits == m1, lane, float(ROUTER_LANES)), axis=-1, keepdims=True)
    rest_l = jnp.where(lane == i1, -jnp.inf, logits)
    m2 = jnp.max(rest_l, axis=-1, keepdims=True)
    i2 = jnp.min(jnp.where(rest_l == m2, lane, float(ROUTER_LANES)), axis=-1, keepdims=True)
    e2 = jnp.exp(m2 - m1)
    den = 1.0 + e2
    g1 = 1.0 / den
    g2 = e2 / den
    sel = ((lane == i1) | (lane == i2)).astype(F32)
    e = N_EXPERTS
    out = jnp.where(lane < e, sel, 0.0)
    out = jnp.where(lane == e, i1, out)
    out = jnp.where(lane == e + 1, i2, out)
    out = jnp.where(lane == e + 2, g1, out)
    out = jnp.where(lane == e + 3, g2, out)
    r_ref[...] = out


def _out_ln(x, acts, w_stack, layer, g, b, router=None):
    t, d = x.shape
    group_rows = [a.shape[0] for a in acts[0]]
    tm = _tile(math.gcd(*group_rows), 512)
    starts = np.cumsum([0] + [r // tm for r in group_rows])
    row = lambda i: (i, 0)
    const = lambda i: (0, 0)
    in_specs = [pl.BlockSpec((tm, d), row)]
    args = [x]
    for per_group in acts:
        for gi, a in enumerate(per_group):
            lo, n = int(starts[gi]), int(starts[gi + 1] - starts[gi])
            in_specs.append(pl.BlockSpec((tm, a.shape[1]),
                                         lambda i, lo=lo, n=n: (jnp.clip(i - lo, 0, n - 1), 0)))
            args.append(a)
    in_specs += [pl.BlockSpec((None,) + w_stack.shape[1:], lambda i: (layer, 0, 0), pipeline_mode=pl.Buffered(1)),
                 pl.BlockSpec((1, d), const), pl.BlockSpec((1, d), const)]
    args += [w_stack, g, b]
    out_specs = [pl.BlockSpec((tm, d), row)]
    out_shape = [jax.ShapeDtypeStruct((t, d), F32)]
    widths = tuple(per_group[0].shape[1] for per_group in acts)
    bounds = tuple(int(s) for s in starts[1:-1])
    if router is not None:
        in_specs += [pl.BlockSpec((d, ROUTER_LANES), const), pl.BlockSpec((d, ROUTER_LANES), const),
                     pl.BlockSpec((1, ROUTER_LANES), const)]
        out_specs.append(pl.BlockSpec((tm, ROUTER_LANES), row))
        out_shape.append(jax.ShapeDtypeStruct((t, ROUTER_LANES), F32))
        args += list(router)
    return pl.pallas_call(
        functools.partial(_out_ln_kernel, widths=widths, bounds=bounds, router=router is not None),
        grid=(t // tm,),
        in_specs=in_specs, out_specs=out_specs, out_shape=out_shape,
        compiler_params=_params("arbitrary"),
    )(*args)


def _swiglu_accumulate(acc_ref, xb, w1, w3, w2):
    h1 = jnp.dot(xb, w1, preferred_element_type=F32)
    h3 = jnp.dot(xb, w3, preferred_element_type=F32)
    h = (h1 * (1.0 / (1.0 + jnp.exp(-h1)))) * h3
    acc_ref[...] += jnp.dot(h.astype(BF16), w2, preferred_element_type=F32)


def _ffn_ln_kernel(x_ref, w1_ref, w3_ref, w2_ref, g_ref, b_ref, y_ref, yb_ref, xb_ref):
    j = pl.program_id(1)

    @pl.when(j == 0)
    def _():
        xb_ref[...] = x_ref[...].astype(BF16)
        y_ref[...] = jnp.zeros(y_ref.shape, F32)

    _swiglu_accumulate(y_ref, xb_ref[...], w1_ref[...], w3_ref[...], w2_ref[...])

    @pl.when(j == pl.num_programs(1) - 1)
    def _():
        y = _layer_norm(DN_ALPHA * x_ref[...] + y_ref[...], g_ref[...], b_ref[...])
        y_ref[...] = y
        yb_ref[...] = y.astype(BF16)


FFN_TM = 512


def _ffn_ln(x, w1, w3, w2, layer, g, b):
    t, d = x.shape
    f = w1.shape[2]
    tm, tf = _tile(t, FFN_TM), _tile(f, 512)
    return pl.pallas_call(
        _ffn_ln_kernel,
        grid=(t // tm, f // tf),
        in_specs=[pl.BlockSpec((tm, d), lambda i, j: (i, 0)),
                  pl.BlockSpec((None, d, tf), lambda i, j: (layer, 0, j)),
                  pl.BlockSpec((None, d, tf), lambda i, j: (layer, 0, j)),
                  pl.BlockSpec((None, tf, d), lambda i, j: (layer, j, 0)),
                  pl.BlockSpec((1, d), lambda i, j: (0, 0)),
                  pl.BlockSpec((1, d), lambda i, j: (0, 0))],
        out_specs=[pl.BlockSpec((tm, d), lambda i, j: (i, 0)),
                   pl.BlockSpec((tm, d), lambda i, j: (i, 0))],
        out_shape=[jax.ShapeDtypeStruct((t, d), F32), jax.ShapeDtypeStruct((t, d), BF16)],
        scratch_shapes=[pltpu.VMEM((tm, d), BF16)],
        compiler_params=_params("arbitrary", "arbitrary"),
    )(x, w1, w3, w2, g, b)


MOE_TM = 1024
DISPATCH_TM = 512
COMBINE_TM = 512
DMA_ISSUE_UNROLL = 8
ZERO_ROWS = 256


def _dispatch_kernel(p1_ref, p2_ref, ends_ref, x_ref, xs_hbm, zbuf, sem, zsem, *, tm, group_tm):
    i = pl.program_id(0)
    base = i * tm

    @pl.when(i == 0)
    def _():
        zbuf[...] = jnp.zeros(zbuf.shape, zbuf.dtype)
        zrows = zbuf.shape[0]

        def clear_tile(row0):
            for c in range(group_tm // zrows):
                start = pl.multiple_of(row0 + c * zrows, zrows)
                pltpu.make_async_copy(zbuf, xs_hbm.at[pl.ds(start, zrows)], zsem).start()
            for c in range(group_tm // zrows):
                pltpu.make_async_copy(zbuf, xs_hbm.at[pl.ds(0, zrows)], zsem).wait()

        for e in range(N_EXPERTS):
            end = ends_ref[e]
            prev_end = ends_ref[e - 1] if e else 0

            @pl.when(end > prev_end)
            def _():
                clear_tile(end - group_tm)

        for k in range(N_EXPERTS):
            tail = ends_ref[N_EXPERTS - 1] + k * group_tm

            @pl.when(tail < xs_hbm.shape[0])
            def _():
                clear_tile(tail)

    def issue(t, carry):
        src = x_ref.at[pl.ds(t, 1)]
        pltpu.make_async_copy(src, xs_hbm.at[pl.ds(p1_ref[base + t], 1)], sem).start()
        pltpu.make_async_copy(src, xs_hbm.at[pl.ds(p2_ref[base + t], 1)], sem).start()
        return carry

    lax.fori_loop(0, tm, issue, 0, unroll=DMA_ISSUE_UNROLL)
    for _ in range(TOP_K):
        pltpu.make_async_copy(x_ref, xs_hbm.at[pl.ds(0, tm)], sem).wait()


def _dispatch(x, p1, p2, ends, n_rows):
    t, d = x.shape
    tm = _tile(t, DISPATCH_TM)
    grid_spec = pltpu.PrefetchScalarGridSpec(
        num_scalar_prefetch=3,
        grid=(t // tm,),
        in_specs=[pl.BlockSpec((tm, d), lambda i, a, c, e: (i, 0))],
        out_specs=pl.BlockSpec(memory_space=pl.ANY),
        scratch_shapes=[pltpu.VMEM((ZERO_ROWS, d), x.dtype), pltpu.SemaphoreType.DMA(()),
                        pltpu.SemaphoreType.DMA(())])
    return pl.pallas_call(
        functools.partial(_dispatch_kernel, tm=tm, group_tm=MOE_TM),
        grid_spec=grid_spec,
        out_shape=jax.ShapeDtypeStruct((n_rows, d), x.dtype),
        compiler_params=_params("arbitrary"),
    )(p1, p2, ends, x)


def _moe_expert_kernel(te_ref, last_ref, xs_ref, w1_ref, w3_ref, w2_ref, y_ref, xb_ref):
    i = pl.program_id(0)
    j = pl.program_id(1)
    active = i <= last_ref[0]

    @pl.when(j == 0)
    def _():
        y_ref[...] = jnp.zeros(y_ref.shape, F32)

    @pl.when(active)
    def _():
        @pl.when(j == 0)
        def _():
            xb_ref[...] = xs_ref[...].astype(BF16)

        _swiglu_accumulate(y_ref, xb_ref[...], w1_ref[...].astype(BF16), w3_ref[...].astype(BF16),
                           w2_ref[...].astype(BF16))


def _moe_experts(xs, tile_expert, last_tile, w1, w3, w2, layer):
    p, d = xs.shape
    f = w1.shape[3]
    tm, tf = MOE_TM, _tile(f, 256)
    grid_spec = pltpu.PrefetchScalarGridSpec(
        num_scalar_prefetch=2,
        grid=(p // tm, f // tf),
        in_specs=[pl.BlockSpec((tm, d), lambda i, j, te, ac: (jnp.minimum(i, ac[0]), 0)),
                  pl.BlockSpec((None, None, d, tf), lambda i, j, te, ac: (layer, te[i], 0, j)),
                  pl.BlockSpec((None, None, d, tf), lambda i, j, te, ac: (layer, te[i], 0, j)),
                  pl.BlockSpec((None, None, tf, d), lambda i, j, te, ac: (layer, te[i], j, 0))],
        out_specs=pl.BlockSpec((tm, d), lambda i, j, te, ac: (i, 0)),
        scratch_shapes=[pltpu.VMEM((tm, d), BF16)])
    return pl.pallas_call(
        _moe_expert_kernel,
        grid_spec=grid_spec,
        out_shape=jax.ShapeDtypeStruct((p, d), F32),
        compiler_params=_params("arbitrary", "arbitrary"),
    )(tile_expert, last_tile, xs, w1, w3, w2)


def _combine_ln_kernel(p1_ref, p2_ref, x_ref, gt_ref, g_ref, b_ref, ys_hbm, *refs, tm, n_out, bounds):
    out_refs = refs[:n_out]
    buf, sem = refs[n_out:]
    i = pl.program_id(0)
    n = pl.num_programs(0)

    def issue(tile, slot):
        base = tile * tm

        def body(t, carry):
            pltpu.make_async_copy(ys_hbm.at[pl.ds(p1_ref[base + t], 1)],
                                  buf.at[slot, pl.ds(t, 1)], sem.at[slot]).start()
            pltpu.make_async_copy(ys_hbm.at[pl.ds(p2_ref[base + t], 1)],
                                  buf.at[slot, pl.ds(tm + t, 1)], sem.at[slot]).start()
            return carry

        lax.fori_loop(0, tm, body, 0, unroll=DMA_ISSUE_UNROLL)

    @pl.when(i == 0)
    def _():
        issue(0, 0)

    @pl.when(i + 1 < n)
    def _():
        issue(i + 1, (i + 1) % 2)

    slot = i % 2
    pltpu.make_async_copy(ys_hbm.at[pl.ds(0, 2 * tm)], buf.at[slot], sem.at[slot]).wait()
    g1 = gt_ref[:, 0:1]
    g2 = gt_ref[:, 1:2]
    f = g1 * buf[slot, pl.ds(0, tm), :] + g2 * buf[slot, pl.ds(tm, tm), :]
    y = _layer_norm(DN_ALPHA * x_ref[...] + f, g_ref[...], b_ref[...])
    if bounds is None:
        y_ref, yb_ref = out_refs
        y_ref[...] = y
        yb_ref[...] = y.astype(BF16)
    else:
        edges = (0,) + bounds + (None,)
        for gi, o_ref in enumerate(out_refs):
            lo, hi = edges[gi], edges[gi + 1]
            in_group = (i >= lo) if hi is None else ((i >= lo) & (i < hi))

            @pl.when(in_group)
            def _(o_ref=o_ref):
                o_ref[...] = y


def _combine_ln(x, ys, p1, p2, gates, g, b, final_group_rows=None):
    t, d = x.shape
    rows = [t] if final_group_rows is None else list(final_group_rows)
    tm = _tile(math.gcd(*rows) if len(rows) > 1 else t, COMBINE_TM)
    row = lambda i, a, c: (i, 0)
    const = lambda i, a, c: (0, 0)
    if final_group_rows is None:
        bounds = None
        out_specs = [pl.BlockSpec((tm, d), row), pl.BlockSpec((tm, d), row)]
        out_shape = [jax.ShapeDtypeStruct((t, d), F32), jax.ShapeDtypeStruct((t, d), BF16)]
    else:
        starts = np.cumsum([0] + [r // tm for r in rows])
        bounds = tuple(int(s) for s in starts[1:-1])
        out_specs, out_shape = [], []
        for gi, r in enumerate(rows):
            lo, n = int(starts[gi]), int(starts[gi + 1] - starts[gi])
            out_specs.append(pl.BlockSpec((tm, d), lambda i, a, c, lo=lo, n=n: (jnp.clip(i - lo, 0, n - 1), 0)))
            out_shape.append(jax.ShapeDtypeStruct((r, d), F32))
    grid_spec = pltpu.PrefetchScalarGridSpec(
        num_scalar_prefetch=2,
        grid=(t // tm,),
        in_specs=[pl.BlockSpec((tm, d), row), pl.BlockSpec((tm, LANES), row),
                  pl.BlockSpec((1, d), const), pl.BlockSpec((1, d), const),
                  pl.BlockSpec(memory_space=pl.ANY)],
        out_specs=out_specs,
        scratch_shapes=[pltpu.VMEM((2, 2 * tm, d), F32), pltpu.SemaphoreType.DMA((2,))])
    return pl.pallas_call(
        functools.partial(_combine_ln_kernel, tm=tm, n_out=len(out_specs), bounds=bounds),
        grid_spec=grid_spec,
        out_shape=out_shape,
        compiler_params=_params("arbitrary"),
    )(p1, p2, x, gates, g, b, ys)


def _moe_routing(r, tm):
    t = r.shape[0]
    e = N_EXPERTS
    sel = r[:, :e].astype(jnp.int32)
    i1 = r[:, e].astype(jnp.int32)
    i2 = r[:, e + 1].astype(jnp.int32)
    cnt = jnp.cumsum(sel, axis=0)
    rank = cnt - sel
    padded = ((cnt[-1] + tm - 1) // tm) * tm
    ends = jnp.cumsum(padded)
    pos = (ends - padded)[None, :] + rank
    lane = jnp.arange(e, dtype=jnp.int32)[None, :]
    p1 = jnp.sum(jnp.where(lane == i1[:, None], pos, 0), axis=1).astype(jnp.int32)
    p2 = jnp.sum(jnp.where(lane == i2[:, None], pos, 0), axis=1).astype(jnp.int32)
    n_rows = TOP_K * t + e * tm
    tile_start = jnp.arange(n_rows // tm, dtype=jnp.int32) * tm
    tile_expert = jnp.sum((tile_start[:, None] >= ends[None, :]).astype(jnp.int32), axis=1)
    tile_expert = jnp.minimum(tile_expert, e - 1).astype(jnp.int32)
    last_tile = (ends[-1:] // tm - 1).astype(jnp.int32)
    return p1, p2, ends.astype(jnp.int32), tile_expert, last_tile, n_rows


def _deinterleave_perm():
    half = HEAD_DIM // 2
    return np.concatenate([np.arange(half) * 2, np.arange(half) * 2 + 1])


def _rope_tables(groups):
    pos = np.concatenate([np.tile(np.arange(seq), batch) for batch, seq in groups])
    pos = jnp.asarray(pos, jnp.int32)
    row = (pos // GRID_W).astype(F32)
    col = (pos % GRID_W).astype(F32)
    axis_dim = HEAD_DIM // 2
    inv = ROPE_THETA ** (-jnp.arange(0, axis_dim, 2, dtype=F32) / axis_dim)
    ang = jnp.concatenate([row[:, None] * inv, col[:, None] * inv], axis=-1)
    cos, sin = jnp.cos(ang), jnp.sin(ang)
    return jnp.concatenate([cos, cos], axis=-1), jnp.concatenate([-sin, sin], axis=-1)


def _group_rows(groups):
    out, row0 = [], 0
    for batch, seq in groups:
        out.append((row0, batch, seq))
        row0 += batch * seq
    return out


def _prep_w_in_even(w_in):
    perm = _deinterleave_perm()
    n_qk = N_HEADS_B + N_KV_B
    c0 = 3 * N_HEADS_A * HEAD_DIM
    pmat = np.zeros((HEAD_DIM, HEAD_DIM), np.float32)
    pmat[perm, np.arange(HEAD_DIM)] = 1.0
    w_in_b = w_in.astype(BF16)
    lead = w_in.shape[:2]
    w_qk = w_in_b[:, :, c0:c0 + n_qk * HEAD_DIM].reshape(*lead, n_qk, HEAD_DIM)
    w_qk = jnp.einsum('ldhk,kn->ldhn', w_qk, jnp.asarray(pmat, BF16), preferred_element_type=F32)
    return jnp.concatenate([w_in_b[:, :, :c0], w_qk.astype(BF16).reshape(*lead, n_qk * HEAD_DIM),
                            w_in_b[:, :, c0 + n_qk * HEAD_DIM:]], axis=2)


def _even_layer(x, xb, groups, layer, w_in_b, qk_gain, w_out_b, w1_b, w3_b, w2_b, ln_g, ln_b, rope, dil):
    w_a = N_HEADS_A * HEAD_DIM
    w_bq = N_HEADS_B * HEAD_DIM
    w_bkv = N_KV_B * HEAD_DIM
    perm = _deinterleave_perm()
    col_scale = jnp.concatenate([jnp.full((w_a,), SCALE * LOG2E, F32),
                                 jnp.ones((w_in_b.shape[2] - w_a,), F32)])[None, :]
    proj = _project(xb, w_in_b, layer, col_scale)

    n_b = N_HEADS_B + N_KV_B
    gains = jnp.concatenate([jnp.tile(qk_gain[0][perm][None], (N_HEADS_B, 1)),
                             jnp.tile(qk_gain[1][perm][None], (N_KV_B, 1))]).astype(F32)
    scales = jnp.concatenate([jnp.full((N_HEADS_B, HEAD_DIM), SCALE * LOG2E, F32),
                              jnp.ones((N_KV_B, HEAD_DIM), F32)])
    qk = _qk_prep(proj, gains.reshape(2, n_b // 2, HEAD_DIM), scales.reshape(2, n_b // 2, HEAD_DIM),
                  rope[0], rope[1], 3 * w_a)

    table, halo_blks, win_blks = dil
    oa, ob = [], []
    v_blk0 = (3 * w_a + w_bq + w_bkv) // HEAD_DIM
    for row0, batch, seq in _group_rows(groups):
        oa.append(_win_attn(proj, table, row0=row0, batch=batch, seq=seq, n_heads=N_HEADS_A,
                            q_blk0=0, k_blk0=N_HEADS_A, v_blk0=2 * N_HEADS_A,
                            halo_blks=halo_blks, win_blks=win_blks, hp=HEADS_PER_STEP_A))
        ob.append(_gqa(qk, proj, row0=row0, batch=batch, seq=seq, v_blk0=v_blk0))
    (x,) = _out_ln(x, [oa, ob], w_out_b, layer, ln_g[0][None], ln_b[0][None])
    return _ffn_ln(x, w1_b, w3_b, w2_b, layer, ln_g[1][None], ln_b[1][None])


def _odd_layer(x, xb, groups, layer, w_in_b, rpb, w_out_b, w_router, b_router, w1_b, w3_b, w2_b, ln_g, ln_b,
               final):
    w_c = N_HEADS_C * HEAD_DIM
    col_scale = jnp.concatenate([jnp.full((w_c,), SCALE * LOG2E, F32), jnp.ones((2 * w_c,), F32)])[None, :]
    proj = _project(xb, w_in_b, layer, col_scale)
    table, halo_blks, win_blks = _natten_table(rpb)
    o = []
    for row0, batch, seq in _group_rows(groups):
        o.append(_win_attn(proj, table, row0=row0, batch=batch, seq=seq, n_heads=N_HEADS_C,
                           q_blk0=0, k_blk0=N_HEADS_C, v_blk0=2 * N_HEADS_C,
                           halo_blks=halo_blks, win_blks=win_blks, hp=HEADS_PER_STEP_C))

    wr = jnp.pad(w_router.astype(F32), ((0, 0), (0, ROUTER_LANES - N_EXPERTS)))
    wr_hi = wr.astype(BF16)
    wr_lo = (wr - wr_hi.astype(F32)).astype(BF16)
    br = jnp.concatenate([b_router.astype(F32), jnp.full((ROUTER_LANES - N_EXPERTS,), NEG, F32)])[None, :]
    x, r = _out_ln(x, [o], w_out_b, layer, ln_g[0][None], ln_b[0][None], router=(wr_hi, wr_lo, br))

    p1, p2, ends, tile_expert, last_tile, n_rows = _moe_routing(r, MOE_TM)
    xs = _dispatch(x, p1, p2, ends, n_rows)
    ys = _moe_experts(xs, tile_expert, last_tile, w1_b, w3_b, w2_b, layer)
    gates = jnp.pad(r[:, N_EXPERTS + 2:N_EXPERTS + 4], ((0, 0), (0, LANES - 2)))
    final_rows = [batch * seq for batch, seq in groups] if final else None
    return _combine_ln(x, ys, p1, p2, gates, ln_g[1][None], ln_b[1][None], final_group_rows=final_rows)


def kernel(x_prompt, x_sample, ln_g, ln_b, w_in_even, qk_gain_b, w_out_even, ffn_w1, ffn_w3, ffn_w2,
           w_in_odd, rpb, w_out_odd, w_router, b_router, moe_w1, moe_w3, moe_w2):
    d = x_prompt.shape[-1]
    groups = [(x_prompt.shape[0], x_prompt.shape[1]), (x_sample.shape[0], x_sample.shape[1])]
    x = jnp.concatenate([x_prompt.reshape(-1, d), x_sample.reshape(-1, d)], axis=0)
    xb = x.astype(BF16)
    rope = _rope_tables(groups)
    dil = _dilated_table()
    even_w = (_prep_w_in_even(w_in_even), w_out_even.astype(BF16),
              ffn_w1.astype(BF16), ffn_w3.astype(BF16), ffn_w2.astype(BF16))
    odd_w = (w_in_odd.astype(BF16), w_out_odd.astype(BF16), moe_w1, moe_w3, moe_w2)
    depth = ln_g.shape[0]
    for i in range(depth):
        j = i // 2
        if i % 2 == 0:
            x, xb = _even_layer(x, xb, groups, j, even_w[0], qk_gain_b[j], even_w[1], *even_w[2:],
                                ln_g[i], ln_b[i], rope, dil)
        elif i < depth - 1:
            x, xb = _odd_layer(x, xb, groups, j, odd_w[0], rpb[j], odd_w[1], w_router[j], b_router[j],
                               *odd_w[2:], ln_g[i], ln_b[i], final=False)
        else:
            y_prompt, y_sample = _odd_layer(x, xb, groups, j, odd_w[0], rpb[j], odd_w[1], w_router[j],
                                            b_router[j], *odd_w[2:], ln_g[i], ln_b[i], final=True)
            return (y_prompt.reshape(x_prompt.shape), y_sample.reshape(x_sample.shape))
    n_p = x_prompt.shape[0] * x_prompt.shape[1]
    return (x[:n_p].reshape(x_prompt.shape), x[n_p:].reshape(x_sample.shape))
```
